```python
import jax
import jax.numpy as jnp
from jax import lax
import numpy as np

D_MODEL = 1024
BATCH = 16
SEQ = 2048
DEPTH = 1

RWKV_HEADS = 8
RWKV_HEAD = 64
RWKV_WIDTH = RWKV_HEADS * RWKV_HEAD
DECAY_LORA = 64
AAA_LORA = 64
GATE_LORA = 128
RWKV_COLS = 3 * RWKV_WIDTH + DECAY_LORA + AAA_LORA + GATE_LORA
RWKV_SPLITS = [RWKV_WIDTH, 2 * RWKV_WIDTH, 3 * RWKV_WIDTH,
               3 * RWKV_WIDTH + DECAY_LORA, 3 * RWKV_WIDTH + DECAY_LORA + AAA_LORA]
DECAY_SCALE = 0.606531
LN_X_EPS = 64e-5
ATT_HEADS = 8
ATT_HEAD = 64
ATT_WIDTH = ATT_HEADS * ATT_HEAD
ATT_COLS = 3 * ATT_WIDTH
MOBA_BLOCK = 256
MOBA_TOPK = 3
Q_BLOCK = 128
ROPE_THETA = 10000.0
GATE_COLS = 2 * D_MODEL
IN_COLS = RWKV_COLS + ATT_COLS + GATE_COLS
N_GROUPS = 4
EXPERTS_PER_GROUP = 8
N_EXPERTS = N_GROUPS * EXPERTS_PER_GROUP
EXPERT_TOPK = 2
D_EXPERT = D_MODEL // 2
NORM_EPS = 1e-6
NEG_INF = -1e30

kernel_name = 'hybrid_rwkv7_moba_hmoe'


def rms_norm(x, g, eps=NORM_EPS):
    xf = x.astype(jnp.float32)
    y = xf * lax.rsqrt(jnp.mean(xf * xf, axis=-1, keepdims=True) + eps)
    return (y * g.astype(jnp.float32)).astype(x.dtype)


def token_shift(z, mu):
    z_prev = jnp.pad(z, ((0, 0), (1, 0), (0, 0)))[:, :-1]
    return z + (z_prev - z) * mu


def rope(x, positions):
    half = x.shape[-1] // 2
    inv_freq = ROPE_THETA ** (-jnp.arange(half, dtype=jnp.float32) / half)
    ang = positions[:, None] * inv_freq[None, :]
    cos = jnp.cos(ang)[None, :, None, :]
    sin = jnp.sin(ang)[None, :, None, :]
    xf = x.astype(jnp.float32)
    x1, x2 = xf[..., :half], xf[..., half:]
    return jnp.concatenate([x1 * cos - x2 * sin, x2 * cos + x1 * sin], axis=-1).astype(x.dtype)


def rwkv7_time_mix(z, mu_shift, w0, w_lora_up, a0, a_lora_up, g_lora_up,
                   k_k, k_a, r_k, lnx_g, lnx_b):
    f32 = jnp.float32
    out_dtype = z.dtype
    B, S, _ = z.shape
    H, N = RWKV_HEADS, RWKV_HEAD
    z = token_shift(z.astype(f32), mu_shift.astype(f32))
    r, k, v, dw, da, dg = jnp.split(z, RWKV_SPLITS, axis=-1)
    log_w = -DECAY_SCALE * jax.nn.sigmoid(w0.astype(f32) + jnp.tanh(dw) @ w_lora_up.astype(f32))
    a = jax.nn.sigmoid(a0.astype(f32) + da @ a_lora_up.astype(f32))
    g = jax.nn.sigmoid(dg) @ g_lora_up.astype(f32)

    def heads(t):
        return t.reshape(B, S, H, N)

    kk = heads(k * k_k.astype(f32))
    kk = kk / jnp.maximum(jnp.sqrt(jnp.sum(kk * kk, axis=-1, keepdims=True)), 1e-12)
    k = heads(k * (1.0 + (a - 1.0) * k_a.astype(f32)))
    r, v, a, w = heads(r), heads(v), heads(a), heads(jnp.exp(log_w))

    def step(state, inp):
        r_t, w_t, k_t, v_t, a_t, b_t = inp
        sa = jnp.einsum('bhij,bhj->bhi', state, a_t)
        state = (state * w_t[:, :, None, :] + sa[..., None] * b_t[:, :, None, :]
                 + v_t[..., None] * k_t[:, :, None, :])
        y_t = jnp.einsum('bhij,bhj->bhi', state, r_t)
        return state, y_t

    xs = tuple(jnp.moveaxis(t, 1, 0) for t in (r, w, k, v, -kk, kk * a))
    s0 = jnp.zeros((B, H, N, N), f32)
    _, ys = lax.scan(step, s0, xs)
    y = jnp.moveaxis(ys, 0, 1)
    mean = jnp.mean(y, axis=-1, keepdims=True)
    var = jnp.mean(jnp.square(y - mean), axis=-1, keepdims=True)
    yn = ((y - mean) * lax.rsqrt(var + LN_X_EPS)).reshape(B, S, H * N)
    yn = yn * lnx_g.astype(f32) + lnx_b.astype(f32)
    bonus = (jnp.sum(r * k * r_k.astype(f32), axis=-1, keepdims=True) * v).reshape(B, S, H * N)
    return ((yn + bonus) * g).astype(out_dtype)


def moba_attention(z, q_norm_g, k_norm_g):
    f32 = jnp.float32
    B, S, _ = z.shape
    H, hd = ATT_HEADS, ATT_HEAD
    q, k, v = jnp.split(z, 3, axis=-1)
    q = q.reshape(B, S, H, hd)
    k = k.reshape(B, S, H, hd)
    v = v.reshape(B, S, H, hd)
    q = rms_norm(q, q_norm_g)
    k = rms_norm(k, k_norm_g)
    pos = jnp.arange(S, dtype=f32)
    q = rope(q, pos)
    k = rope(k, pos)

    n_blk = -(-S // MOBA_BLOCK)
    s_pad = n_blk * MOBA_BLOCK
    padw = ((0, 0), (0, s_pad - S), (0, 0), (0, 0))
    q = jnp.pad(q, padw).transpose(0, 2, 1, 3)
    k = jnp.pad(k, padw).transpose(0, 2, 1, 3)
    v = jnp.pad(v, padw).transpose(0, 2, 1, 3)
    kb = k.reshape(B, H, n_blk, MOBA_BLOCK, hd)
    vb = v.reshape(B, H, n_blk, MOBA_BLOCK, hd)

    k_mean = jnp.mean(kb.astype(f32), axis=3)
    gate = jnp.einsum('bhsd,bhnd->bhsn', q.astype(f32), k_mean)
    q_blk = jnp.arange(s_pad) // MOBA_BLOCK
    past = jnp.arange(n_blk)[None, :] < q_blk[:, None]
    gate = jnp.where(past, gate, NEG_INF)
    n_sel = min(MOBA_TOPK, n_blk)
    top_val, top_idx = lax.top_k(gate, n_sel)
    top_ok = top_val > 0.5 * NEG_INF

    n_qc = s_pad // Q_BLOCK

    def to_chunks(t):
        t = t.reshape((B, H, n_qc, Q_BLOCK) + t.shape[3:])
        t = jnp.moveaxis(t, 2, 1)
        return t.reshape((B * n_qc, H, Q_BLOCK) + t.shape[4:])

    qc, ic, oc = to_chunks(q), to_chunks(top_idx), to_chunks(top_ok)
    b_ids = jnp.repeat(jnp.arange(B, dtype=jnp.int32), n_qc)
    c_ids = jnp.tile(jnp.arange(n_qc, dtype=jnp.int32), B)
    h_ids = jnp.arange(H)[:, None, None]
    scale = ATT_HEAD ** -0.5

    def attend(args):
        q_c, idx_c, ok_c, b, ci = args
        kbb = kb[b]
        vbb = vb[b]
        k_sel = kbb[h_ids, idx_c]
        v_sel = vbb[h_ids, idx_c]
        s_sel = jnp.einsum('hqd,hqnld->hqnl', q_c, k_sel).astype(f32) * scale
        s_sel = jnp.where(ok_c[..., None], s_sel, NEG_INF).reshape(H, Q_BLOCK, n_sel * MOBA_BLOCK)
        own = (ci * Q_BLOCK) // MOBA_BLOCK
        k_own = lax.dynamic_index_in_dim(kbb, own, axis=1, keepdims=False)
        v_own = lax.dynamic_index_in_dim(vbb, own, axis=1, keepdims=False)
        s_own = jnp.einsum('hqd,hld->hql', q_c, k_own).astype(f32) * scale
        q_pos = ci * Q_BLOCK + jnp.arange(Q_BLOCK)
        k_pos = own * MOBA_BLOCK + jnp.arange(MOBA_BLOCK)
        s_own = jnp.where(k_pos[None, None, :] <= q_pos[None, :, None], s_own, NEG_INF)
        p = jax.nn.softmax(jnp.concatenate([s_sel, s_own], axis=-1), axis=-1).astype(v_own.dtype)
        p_sel = p[..., :n_sel * MOBA_BLOCK].reshape(H, Q_BLOCK, n_sel, MOBA_BLOCK)
        p_own = p[..., n_sel * MOBA_BLOCK:]
        return (jnp.einsum('hqnl,hqnld->hqd', p_sel, v_sel)
                + jnp.einsum('hql,hld->hqd', p_own, v_own))

    out = lax.map(attend, (qc, ic, oc, b_ids, c_ids))
    out = out.reshape(B, n_qc, H, Q_BLOCK, hd).transpose(0, 1, 3, 2, 4).reshape(B, s_pad, H * hd)
    return out[:, :S]


def hierarchical_moe(h, w_rg, b_rg, w_re, b_re, w1, w3, w2):
    f32 = jnp.float32
    B, S, D = h.shape
    T = B * S
    ht = h.reshape(T, D)
    p_group = jax.nn.softmax((ht @ w_rg + b_rg).astype(f32), axis=-1)
    pg_top, g_idx = lax.top_k(p_group, 1)
    e_logits = (ht @ w_re + b_re).astype(f32).reshape(T, N_GROUPS, EXPERTS_PER_GROUP)
    e_logits = jnp.take_along_axis(e_logits, g_idx[:, :, None], axis=1)[:, 0]
    p_exp = jax.nn.softmax(e_logits, axis=-1)
    pe_top, e_idx = lax.top_k(p_exp, EXPERT_TOPK)
    weights = pg_top * pe_top / jnp.sum(pe_top, axis=-1, keepdims=True)
    expert = g_idx * EXPERTS_PER_GROUP + e_idx
    flat = expert.reshape(-1)
    order = jnp.argsort(flat)
    tok = order // EXPERT_TOPK
    sizes = jnp.bincount(flat, length=N_EXPERTS).astype(jnp.int32)
    xs = ht[tok]
    hid = jax.nn.silu(lax.ragged_dot(xs, w1, sizes)) * lax.ragged_dot(xs, w3, sizes)
    ys = lax.ragged_dot(hid, w2, sizes) * weights.reshape(-1)[order][:, None].astype(h.dtype)
    out = jax.ops.segment_sum(ys, tok, num_segments=T)
    return out.reshape(B, S, D)


def setup_inputs(seed: int = 0) -> dict:
    key = jax.random.key(seed)
    ks = jax.random.split(key, 32)
    f32 = jnp.float32
    L = DEPTH

    def nrm(k, shape, scale):
        return jax.random.normal(k, shape, f32) * scale

    def gain(k, shape):
        return 1.0 + 0.05 * jax.random.normal(k, shape, f32)

    return {
        'x': nrm(ks[0], (BATCH, SEQ, D_MODEL), 1.0),
        'c': nrm(ks[1], (BATCH, D_MODEL), 1.0),
        'w_ada': nrm(ks[2], (L, D_MODEL, 6 * D_MODEL), 0.02),
        'b_ada': nrm(ks[3], (L, 6 * D_MODEL), 0.02),
        'ln1_g': gain(ks[4], (L, D_MODEL)),
        'ln2_g': gain(ks[5], (L, D_MODEL)),
        'w_in': nrm(ks[6], (L, D_MODEL, IN_COLS), D_MODEL ** -0.5),
        'mu_shift': jax.random.uniform(ks[7], (L, RWKV_COLS), f32, 0.1, 0.9),
        'w0': jax.random.uniform(ks[8], (L, RWKV_WIDTH), f32, -5.0, 1.0),
        'w_lora_up': nrm(ks[9], (L, DECAY_LORA, RWKV_WIDTH), 0.1),
        'a0': nrm(ks[10], (L, RWKV_WIDTH), 0.5),
        'a_lora_up': nrm(ks[11], (L, AAA_LORA, RWKV_WIDTH), 0.1),
        'g_lora_up': nrm(ks[12], (L, GATE_LORA, RWKV_WIDTH), GATE_LORA ** -0.5),
        'k_k': 0.85 + 0.05 * jax.random.normal(ks[13], (L, RWKV_WIDTH), f32),
        'k_a': gain(ks[14], (L, RWKV_WIDTH)),
        'r_k': nrm(ks[15], (L, RWKV_HEADS, RWKV_HEAD), 0.1),
        'lnx_g': gain(ks[16], (L, RWKV_WIDTH)),
        'lnx_b': nrm(ks[17], (L, RWKV_WIDTH), 0.01),
        'q_norm_g': gain(ks[18], (L, ATT_HEAD)),
        'k_norm_g': gain(ks[19], (L, ATT_HEAD)),
        'w_br_rwkv': nrm(ks[20], (L, RWKV_WIDTH, D_MODEL), RWKV_WIDTH ** -0.5),
        'w_br_moba': nrm(ks[21], (L, ATT_WIDTH, D_MODEL), ATT_WIDTH ** -0.5),
        'w_out': nrm(ks[22], (L, D_MODEL, D_MODEL), D_MODEL ** -0.5),
        'w_rg': nrm(ks[23], (L, D_MODEL, N_GROUPS), D_MODEL ** -0.5),
        'b_rg': nrm(ks[24], (L, N_GROUPS), 0.01),
        'w_re': nrm(ks[25], (L, D_MODEL, N_EXPERTS), D_MODEL ** -0.5),
        'b_re': nrm(ks[26], (L, N_EXPERTS), 0.01),
        'w1': nrm(ks[27], (L, N_EXPERTS, D_MODEL, D_EXPERT), D_MODEL ** -0.5),
        'w3': nrm(ks[28], (L, N_EXPERTS, D_MODEL, D_EXPERT), D_MODEL ** -0.5),
        'w2': nrm(ks[29], (L, N_EXPERTS, D_EXPERT, D_MODEL), D_EXPERT ** -0.5),
    }


def reference(x, c, w_ada, b_ada, ln1_g, ln2_g, w_in, mu_shift, w0, w_lora_up, a0,
              a_lora_up, g_lora_up, k_k, k_a, r_k, lnx_g, lnx_b, q_norm_g, k_norm_g,
              w_br_rwkv, w_br_moba, w_out, w_rg, b_rg, w_re, b_re, w1, w3, w2):
    for l in range(DEPTH):
        mod = jnp.einsum('bd,de->be', jax.nn.silu(c), w_ada[l]) + b_ada[l]
        shift1, scale1, gate1, shift2, scale2, gate2 = jnp.split(mod[:, None, :], 6, axis=-1)

        h = rms_norm(x, ln1_g[l]) * (1.0 + scale1) + shift1
        z = jnp.einsum('bsd,de->bse', h, w_in[l])
        z_rwkv = z[..., :RWKV_COLS]
        z_att = z[..., RWKV_COLS:RWKV_COLS + ATT_COLS]
        z_gate = z[..., RWKV_COLS + ATT_COLS:]
        y_a = rwkv7_time_mix(z_rwkv, mu_shift[l], w0[l], w_lora_up[l], a0[l], a_lora_up[l],
                             g_lora_up[l], k_k[l], k_a[l], r_k[l], lnx_g[l], lnx_b[l])
        y_b = moba_attention(z_att, q_norm_g[l], k_norm_g[l])
        g_a, g_b = jnp.split(jax.nn.sigmoid(z_gate), 2, axis=-1)
        merged = (g_a * jnp.einsum('bsc,cd->bsd', y_a, w_br_rwkv[l])
                  + g_b * jnp.einsum('bsc,cd->bsd', y_b, w_br_moba[l]))
        x = x + gate1 * jnp.einsum('bsd,de->bse', merged, w_out[l])

        h2 = rms_norm(x, ln2_g[l]) * (1.0 + scale2) + shift2
        x = x + gate2 * hierarchical_moe(h2, w_rg[l], b_rg[l], w_re[l], b_re[l], w1[l], w3[l], w2[l])
    return x
```

```python
import functools

import jax
import jax.numpy as jnp
import numpy as np
from jax import lax
from jax.experimental import pallas as pl
from jax.experimental.pallas import tpu as pltpu

F32 = jnp.float32
BF16 = jnp.bfloat16
HIGHEST = lax.Precision.HIGHEST

D_MODEL = 1024
RWKV_HEADS = 8
HEAD = 64
WIDTH = RWKV_HEADS * HEAD
DECAY_LORA = 64
AAA_LORA = 64
GATE_LORA = 128
RWKV_COLS = 3 * WIDTH + DECAY_LORA + AAA_LORA + GATE_LORA
ATT_COLS = 3 * WIDTH
GATE_COLS = 2 * D_MODEL
IN_COLS = RWKV_COLS + ATT_COLS + GATE_COLS
DECAY_SCALE = 0.606531
LN_X_EPS = 64e-5
MOBA_BLOCK = 256
MOBA_TOPK = 3
ROPE_THETA = 10000.0
N_GROUPS = 4
EXPERTS_PER_GROUP = 8
N_EXPERTS = N_GROUPS * EXPERTS_PER_GROUP
D_EXPERT = D_MODEL // 2
NORM_EPS = 1e-6
NEG_INF = -1e30

LANES = 128
PAIRS = WIDTH // LANES
CHUNK = 64
SOLVE_BLOCK = 16
VMEM_LIMIT = 56 * 1024 * 1024

ROUTER_LANES = LANES
EXPERT_LANE0 = N_GROUPS


def _dot(a, b):
    return jnp.dot(a.astype(BF16), b.astype(BF16), preferred_element_type=F32)


def _dot_nt(a, b):
    return lax.dot_general(a.astype(BF16), b.astype(BF16), (((1,), (1,)), ((), ())),
                           preferred_element_type=F32)


def _dot_f32(a, b):
    return jnp.dot(a, b, precision=HIGHEST, preferred_element_type=F32)


def _seg_sum(x, ones_bd):
    hi = x.astype(BF16)
    lo = (x - hi.astype(F32)).astype(BF16)
    return (jnp.dot(hi, ones_bd, preferred_element_type=F32)
            + jnp.dot(lo, ones_bd, preferred_element_type=F32))


def _iota(shape, axis):
    return lax.broadcasted_iota(jnp.int32, shape, axis)


def _mod_kernel(c_ref, w_ref, b_ref, o_ref):
    c = c_ref[...]
    o_ref[...] = _dot_f32(c * jax.nn.sigmoid(c), w_ref[...]) + b_ref[...]


def _mod_call(c, w_ada, b_ada):
    batch = c.shape[0]
    n_out = w_ada.shape[1]
    tn = D_MODEL
    return pl.pallas_call(
        _mod_kernel,
        grid=(n_out // tn,),
        in_specs=[pl.BlockSpec((batch, D_MODEL), lambda j: (0, 0)),
                  pl.BlockSpec((D_MODEL, tn), lambda j: (0, j)),
                  pl.BlockSpec((1, tn), lambda j: (0, j))],
        out_specs=pl.BlockSpec((batch, tn), lambda j: (0, j)),
        out_shape=jax.ShapeDtypeStruct((batch, n_out), F32),
        name="adaln_mod",
    )(c, w_ada, b_ada.reshape(1, n_out))


def _swap_halves(x):
    first = (_iota(x.shape, 1) & (HEAD - 1)) < HEAD // 2
    up = pltpu.roll(x, LANES - HEAD // 2, axis=1)
    down = pltpu.roll(x, HEAD // 2, axis=1)
    return jnp.where(first, up, down)


def _head_norm_rope(x, gain, cos, sin_signed, ones_bd):
    ms = _seg_sum(x * x, ones_bd) * (1.0 / HEAD)
    y = x * lax.rsqrt(ms + NORM_EPS) * gain
    cols = []
    for p in range(PAIRS):
        sl = slice(p * LANES, (p + 1) * LANES)
        yb = y[:, sl]
        cols.append(yb * cos[:, sl] + _swap_halves(yb) * sin_signed[:, sl])
    return jnp.concatenate(cols, axis=1)


def _inproj_kernel(x_ref, shift_ref, scale_ref, g_ref, w_ref, qg_ref, kg_ref, cos_ref, sin_ref,
                   ones_ref, zr_ref, q_ref, k_ref, v_ref, gate_ref, kmean_ref):
    x = x_ref[0]
    ms = jnp.mean(x * x, axis=-1, keepdims=True)
    h = x * lax.rsqrt(ms + NORM_EPS) * g_ref[...]
    h = (h * (1.0 + scale_ref[0]) + shift_ref[0]).astype(BF16)

    zr_ref[0] = jnp.dot(h, w_ref[:, 0:RWKV_COLS], preferred_element_type=F32)

    za = jnp.dot(h, w_ref[:, RWKV_COLS:RWKV_COLS + ATT_COLS], preferred_element_type=F32)
    ones_bd = ones_ref[...]
    cos = cos_ref[...]
    sin = sin_ref[...]
    q = _head_norm_rope(za[:, 0:WIDTH], qg_ref[...], cos, sin, ones_bd)
    k = _head_norm_rope(za[:, WIDTH:2 * WIDTH], kg_ref[...], cos, sin, ones_bd)
    q_ref[0] = (q * (HEAD ** -0.5)).astype(BF16)
    k_ref[0] = k.astype(BF16)
    v_ref[0] = za[:, 2 * WIDTH:3 * WIDTH].astype(BF16)
    kmean_ref[0] = jnp.mean(k, axis=0, keepdims=True)

    zg = jnp.dot(h, w_ref[:, RWKV_COLS + ATT_COLS:IN_COLS], preferred_element_type=F32)
    gate_ref[0] = jax.nn.sigmoid(zg).astype(BF16)


def _inproj_call(x, shift1, scale1, ln1_g, w_in_bf16, q_gain, k_gain, cos, sin_signed, ones_bd):
    batch, seq, _ = x.shape
    tm = MOBA_BLOCK
    n_t = seq // tm
    row = lambda width: pl.BlockSpec((1, tm, width), lambda b, i: (b, i, 0))
    per_batch = pl.BlockSpec((1, 1, D_MODEL), lambda b, i: (b, 0, 0))
    const = lambda shape: pl.BlockSpec(shape, lambda b, i: (0,) * len(shape))
    return pl.pallas_call(
        _inproj_kernel,
        grid=(batch, n_t),
        in_specs=[row(D_MODEL), per_batch, per_batch, const((1, D_MODEL)),
                  const((D_MODEL, IN_COLS)), const((1, WIDTH)), const((1, WIDTH)),
                  pl.BlockSpec((tm, WIDTH), lambda b, i: (i, 0)),
                  pl.BlockSpec((tm, WIDTH), lambda b, i: (i, 0)),
                  const((WIDTH, WIDTH))],
        out_specs=[row(RWKV_COLS), row(WIDTH), row(WIDTH), row(WIDTH), row(GATE_COLS),
                   pl.BlockSpec((1, 1, WIDTH), lambda b, i: (b * n_t + i, 0, 0))],
        out_shape=[jax.ShapeDtypeStruct((batch, seq, RWKV_COLS), F32),
                   jax.ShapeDtypeStruct((batch, seq, WIDTH), BF16),
                   jax.ShapeDtypeStruct((batch, seq, WIDTH), BF16),
                   jax.ShapeDtypeStruct((batch, seq, WIDTH), BF16),
                   jax.ShapeDtypeStruct((batch, seq, GATE_COLS), BF16),
                   jax.ShapeDtypeStruct((batch * n_t, 1, WIDTH), F32)],
        compiler_params=pltpu.CompilerParams(
            dimension_semantics=("arbitrary", "arbitrary"), vmem_limit_bytes=VMEM_LIMIT),
        name="inproj",
    )(x, shift1, scale1, ln1_g, w_in_bf16, q_gain, k_gain, cos, sin_signed, ones_bd)


def _stack_heads(x):
    first = _iota(x.shape, 1) < HEAD
    return jnp.concatenate([jnp.where(first, x, 0.0), jnp.where(first, 0.0, x)], axis=0)


def _add_eye(x, eye):
    return jnp.where(eye, x + 1.0, x)


def _rwkv_kernel(z_ref, mu_ref, w0_ref, a0_ref, lora_ref, glora_ref, kk_ref, ka_ref, rk_ref,
                 lng_ref, lnb_ref, ones_ref, tri_ref, o_ref, prev_ref, state_ref):
    c = pl.program_id(1)

    @pl.when(c == 0)
    def _():
        prev_ref[...] = jnp.zeros_like(prev_ref)
        state_ref[...] = jnp.zeros_like(state_ref)

    z = z_ref[0]
    row = _iota(z.shape, 0)
    z_prev = jnp.where(row == 0, prev_ref[...], pltpu.roll(z, 1, axis=0))
    prev_ref[...] = z[CHUNK - 1:CHUNK, :]
    zs = z + (z_prev - z) * mu_ref[...]

    r = zs[:, 0:WIDTH]
    k = zs[:, WIDTH:2 * WIDTH]
    v = zs[:, 2 * WIDTH:3 * WIDTH]
    lo = 3 * WIDTH
    d_wa = zs[:, lo:lo + DECAY_LORA + AAA_LORA]
    d_g = zs[:, lo + DECAY_LORA + AAA_LORA:RWKV_COLS]
    is_decay = _iota(d_wa.shape, 1) < DECAY_LORA
    pre = _dot_f32(jnp.where(is_decay, jnp.tanh(d_wa), d_wa), lora_ref[...])
    log_w = -DECAY_SCALE * jax.nn.sigmoid(w0_ref[...] + pre[:, 0:WIDTH])
    a = jax.nn.sigmoid(a0_ref[...] + pre[:, WIDTH:2 * WIDTH])
    g = _dot_f32(jax.nn.sigmoid(d_g), glora_ref[...])

    ones_bd = ones_ref[...]
    kk = k * kk_ref[...]
    kk = kk / jnp.maximum(jnp.sqrt(_seg_sum(kk * kk, ones_bd)), 1e-12)
    k = k * (1.0 + (a - 1.0) * ka_ref[...])
    bonus = _seg_sum(r * k * rk_ref[...], ones_bd) * v

    cl = _dot_f32(tri_ref[...], log_w)
    cl_last = cl[CHUNK - 1:CHUNK, :]
    a_t = -kk * jnp.exp(cl - log_w)
    e_neg = jnp.exp(-cl)
    b_t = kk * a * e_neg
    k_t = k * e_neg
    r_t = r * jnp.exp(cl)
    e_end = jnp.exp(cl_last - cl)
    b_end = kk * a * e_end
    k_end = k * e_end
    p_end = jnp.exp(cl_last)

    n2 = 2 * CHUNK
    ri = _iota((n2, n2), 0)
    ci = _iota((n2, n2), 1)
    eye = ri == ci
    strict = ri > ci
    incl = ri >= ci
    same_blk = (ri // SOLVE_BLOCK) == (ci // SOLVE_BLOCK)

    ys = []
    for p in range(PAIRS):
        sl = slice(p * LANES, (p + 1) * LANES)
        sa, sb, sk, sr, sv = (_stack_heads(t[:, sl]) for t in (a_t, b_t, k_t, r_t, v))
        sb_end, sk_end = _stack_heads(b_end[:, sl]), _stack_heads(k_end[:, sl])

        a_ab = jnp.where(strict, _dot_nt(sa, sb), 0.0)
        a_ak = jnp.where(strict, _dot_nt(sa, sk), 0.0)
        a_rb = jnp.where(incl, _dot_nt(sr, sb), 0.0)
        a_rk = jnp.where(incl, _dot_nt(sr, sk), 0.0)

        d1 = jnp.where(same_blk, a_ab, 0.0)
        e1 = a_ab - d1
        d2 = _dot(d1, d1)
        d4 = _dot(d2, d2)
        d8 = _dot(d4, d4)
        t_d = _dot(_dot(_add_eye(d1, eye), _add_eye(d2, eye)),
                   _dot(_add_eye(d4, eye), _add_eye(d8, eye)))
        g1 = _dot(t_d, e1)
        g2 = _dot(g1, g1)
        t_inv = _dot(_dot(_add_eye(g1, eye), _add_eye(g2, eye)), t_d)

        w_m = _dot(t_inv, sa)
        u0 = _dot(t_inv, _dot(a_ak, sv))
        psi = sr + _dot(a_rb, w_m)
        y0 = _dot(a_rb, u0) + _dot(a_rk, sv)
        sb_end_t = sb_end.T
        pe = p_end[:, sl]
        phi = jnp.where(eye, pe, 0.0) + _dot(sb_end_t, w_m)
        z0 = _dot(sb_end_t, u0) + _dot(sk_end.T, sv)

        state = state_ref[p]
        y = _dot_f32(psi, state) + y0
        state_ref[p] = _dot_f32(phi, state) + z0
        ys.append(y[0:CHUNK, :] + y[CHUNK:n2, :])

    y = jnp.concatenate(ys, axis=1)
    mean = _seg_sum(y, ones_bd) * (1.0 / HEAD)
    yc = y - mean
    var = _seg_sum(yc * yc, ones_bd) * (1.0 / HEAD)
    yn = yc * lax.rsqrt(var + LN_X_EPS) * lng_ref[...] + lnb_ref[...]
    o_ref[0] = ((yn + bonus) * g).astype(o_ref.dtype)


def _rwkv_call(z_rwkv, mu_shift, w0, a0, lora_cat, g_lora_up, k_k, k_a, r_k, lnx_g, lnx_b,
               ones_bd, tri):
    batch, seq, _ = z_rwkv.shape
    const = lambda shape: pl.BlockSpec(shape, lambda b, c: (0,) * len(shape))
    vec = const((1, WIDTH))
    return pl.pallas_call(
        _rwkv_kernel,
        grid=(batch, seq // CHUNK),
        in_specs=[pl.BlockSpec((1, CHUNK, RWKV_COLS), lambda b, c: (b, c, 0)),
                  const((1, RWKV_COLS)), vec, vec,
                  const((DECAY_LORA + AAA_LORA, 2 * WIDTH)), const((GATE_LORA, WIDTH)),
                  vec, vec, vec, vec, vec, const((WIDTH, WIDTH)), const((CHUNK, CHUNK))],
        out_specs=pl.BlockSpec((1, CHUNK, WIDTH), lambda b, c: (b, c, 0)),
        out_shape=jax.ShapeDtypeStruct((batch, seq, WIDTH), BF16),
        scratch_shapes=[pltpu.VMEM((1, RWKV_COLS), F32),
                        pltpu.VMEM((PAIRS, LANES, LANES), F32)],
        compiler_params=pltpu.CompilerParams(
            dimension_semantics=("arbitrary", "arbitrary"), vmem_limit_bytes=VMEM_LIMIT),
        name="rwkv7",
    )(z_rwkv, mu_shift, w0, a0, lora_cat, g_lora_up, k_k, k_a, r_k, lnx_g, lnx_b, ones_bd, tri)


def _moba_kernel(q_ref, k_ref, v_ref, km_ref, o_ref):
    qi = pl.program_id(2)
    tq = q_ref.shape[1]
    q = q_ref[0]
    lane = _iota((tq, LANES), 1)
    first = lane < HEAD
    n_blk = km_ref.shape[1]
    km = jnp.concatenate([km_ref[0], jnp.zeros((LANES - n_blk, LANES), F32)], axis=0)

    qh = [jnp.where(first, q, jnp.zeros_like(q)), jnp.where(first, jnp.zeros_like(q), q)]

    valid = lane < qi
    biases = []
    for h in range(2):
        gate = lax.dot_general(qh[h].astype(F32), km, (((1,), (1,)), ((), ())),
                               precision=HIGHEST, preferred_element_type=F32)
        gate = jnp.where(valid, gate, NEG_INF)
        bias = jnp.full((tq, LANES), NEG_INF, F32)
        for n in range(n_blk):
            g_n = gate[:, n:n + 1]
            ahead = valid & ((gate > g_n) | ((gate == g_n) & (lane < n)))
            rank = jnp.sum(ahead.astype(F32), axis=-1, keepdims=True)
            chosen = (rank < MOBA_TOPK) & (qi > n)
            bias = jnp.where((lane == n) & chosen, 0.0, bias)
        biases.append(bias)

    row = _iota((tq, MOBA_BLOCK), 0)
    col = _iota((tq, MOBA_BLOCK), 1)
    causal = col <= row

    def body(kb, carry):
        k_blk = k_ref[0, pl.ds(kb * MOBA_BLOCK, MOBA_BLOCK), :]
        v_blk = v_ref[0, pl.ds(kb * MOBA_BLOCK, MOBA_BLOCK), :]
        own = kb == qi
        keep = jnp.logical_or(jnp.logical_not(own), causal)
        out = []
        for h in range(2):
            m, l, acc = carry[h]
            s = lax.dot_general(qh[h], k_blk, (((1,), (1,)), ((), ())),
                                preferred_element_type=F32)
            b_col = jnp.sum(jnp.where(lane == kb, biases[h], 0.0), axis=-1, keepdims=True)
            b_col = jnp.where(own, 0.0, b_col)
            s = jnp.where(keep, s + b_col, NEG_INF)
            m_new = jnp.maximum(m, jnp.max(s, axis=-1, keepdims=True))
            alpha = jnp.exp(m - m_new)
            p = jnp.exp(s - m_new)
            l = alpha * l + jnp.sum(p, axis=-1, keepdims=True)
            acc = alpha * acc + jnp.dot(p.astype(BF16), v_blk, preferred_element_type=F32)
            out.append((m_new, l, acc))
        return tuple(out)

    init = tuple((jnp.full((tq, 1), NEG_INF, F32), jnp.zeros((tq, 1), F32),
                  jnp.zeros((tq, LANES), F32)) for _ in range(2))
    (_, l0, acc0), (_, l1, acc1) = lax.fori_loop(0, qi + 1, body, init)
    o_ref[0] = jnp.where(first, acc0 / l0, acc1 / l1).astype(o_ref.dtype)


def _moba_call(q, k, v, kmean):
    batch, seq, _ = q.shape
    tq = MOBA_BLOCK
    n_blk = seq // MOBA_BLOCK
    return pl.pallas_call(
        _moba_kernel,
        grid=(batch, PAIRS, seq // tq),
        in_specs=[pl.BlockSpec((1, tq, LANES), lambda b, p, i: (b, i, p)),
                  pl.BlockSpec((1, seq, LANES), lambda b, p, i: (b, 0, p)),
                  pl.BlockSpec((1, seq, LANES), lambda b, p, i: (b, 0, p)),
                  pl.BlockSpec((1, n_blk, LANES), lambda b, p, i: (b, 0, p))],
        out_specs=pl.BlockSpec((1, tq, LANES), lambda b, p, i: (b, i, p)),
        out_shape=jax.ShapeDtypeStruct((batch, seq, WIDTH), BF16),
        compiler_params=pltpu.CompilerParams(
            dimension_semantics=("arbitrary", "arbitrary", "arbitrary"),
            vmem_limit_bytes=VMEM_LIMIT),
        name="moba",
    )(q, k, v, kmean)


def _merge_kernel(x_ref, ya_ref, yb_ref, gate_ref, wa_ref, wb_ref, wo_ref, g1_ref, ln_ref,
                  scale_ref, shift_ref, wr_ref, br_ref, x1_ref, h2_ref, route_ref):
    ya = jnp.dot(ya_ref[...], wa_ref[...], preferred_element_type=F32)
    yb = jnp.dot(yb_ref[...], wb_ref[...], preferred_element_type=F32)
    gates = gate_ref[...]
    merged = (gates[:, 0:D_MODEL].astype(F32) * ya + gates[:, D_MODEL:GATE_COLS].astype(F32) * yb)
    x1 = x_ref[...] + g1_ref[0] * jnp.dot(merged.astype(BF16), wo_ref[...],
                                          preferred_element_type=F32)
    x1_ref[...] = x1

    ms = jnp.mean(x1 * x1, axis=-1, keepdims=True)
    h2 = x1 * lax.rsqrt(ms + NORM_EPS) * ln_ref[...]
    h2 = h2 * (1.0 + scale_ref[0]) + shift_ref[0]
    h2_ref[...] = h2.astype(BF16)

    logits = _dot_f32(h2, wr_ref[...]) + br_ref[...]
    lane = _iota(logits.shape, 1)
    lane_f = lane.astype(F32)
    far = float(ROUTER_LANES)

    def top(vals):
        m = jnp.max(vals, axis=-1, keepdims=True)
        idx = jnp.min(jnp.where(vals == m, lane_f, far), axis=-1, keepdims=True)
        return m, idx

    grp = jnp.where(lane < N_GROUPS, logits, NEG_INF)
    g_max, g_idx = top(grp)
    p_group = 1.0 / jnp.sum(jnp.exp(grp - g_max), axis=-1, keepdims=True)

    e_lo = EXPERT_LANE0 + EXPERTS_PER_GROUP * g_idx
    in_grp = (lane_f >= e_lo) & (lane_f < e_lo + EXPERTS_PER_GROUP)
    el = jnp.where(in_grp, logits, NEG_INF)
    m1, i1 = top(el)
    m2, i2 = top(jnp.where(lane_f == i1, NEG_INF, el))
    ratio = jnp.exp(m2 - m1)
    w_first = p_group / (1.0 + ratio)
    w_second = w_first * ratio
    route_ref[...] = jnp.where(lane_f == i1, w_first, jnp.where(lane_f == i2, w_second, 0.0))


def _merge_call(x2d, ya, yb, gates, w_br_rwkv, w_br_moba, w_out, gate1, ln2_g, scale2, shift2,
                w_router, b_router, seq):
    tokens = x2d.shape[0]
    tm = 256
    per_seq = seq // tm
    row = lambda width: pl.BlockSpec((tm, width), lambda i: (i, 0))
    per_batch = pl.BlockSpec((1, 1, D_MODEL), lambda i: (i // per_seq, 0, 0))
    const = lambda shape: pl.BlockSpec(shape, lambda i: (0,) * len(shape))
    return pl.pallas_call(
        _merge_kernel,
        grid=(tokens // tm,),
        in_specs=[row(D_MODEL), row(WIDTH), row(WIDTH), row(GATE_COLS),
                  const((WIDTH, D_MODEL)), const((WIDTH, D_MODEL)), const((D_MODEL, D_MODEL)),
                  per_batch, const((1, D_MODEL)), per_batch, per_batch,
                  const((D_MODEL, ROUTER_LANES)), const((1, ROUTER_LANES))],
        out_specs=[row(D_MODEL), row(D_MODEL), row(ROUTER_LANES)],
        out_shape=[jax.ShapeDtypeStruct((tokens, D_MODEL), F32),
                   jax.ShapeDtypeStruct((tokens, D_MODEL), BF16),
                   jax.ShapeDtypeStruct((tokens, ROUTER_LANES), F32)],
        compiler_params=pltpu.CompilerParams(
            dimension_semantics=("arbitrary",), vmem_limit_bytes=VMEM_LIMIT),
        name="merge",
    )(x2d, ya, yb, gates, w_br_rwkv, w_br_moba, w_out, gate1, ln2_g, scale2, shift2,
      w_router, b_router)


def _moe_kernel(h_ref, route_ref, w1_ref, w3_ref, w2_ref, x1_ref, g2_ref, o_ref, acc_ref):
    e = pl.program_id(1)

    @pl.when(e == 0)
    def _():
        acc_ref[...] = jnp.zeros_like(acc_ref)

    h = h_ref[...]
    a = jnp.dot(h, w1_ref[0], preferred_element_type=F32)
    b = jnp.dot(h, w3_ref[0], preferred_element_type=F32)
    hid = (a * jax.nn.sigmoid(a) * b).astype(BF16)
    y = jnp.dot(hid, w2_ref[0], preferred_element_type=F32)
    route = route_ref[...]
    lane = _iota(route.shape, 1)
    w_col = jnp.sum(jnp.where(lane == e + EXPERT_LANE0, route, 0.0), axis=-1, keepdims=True)
    acc_ref[...] += w_col * y

    @pl.when(e == N_EXPERTS - 1)
    def _():
        o_ref[...] = x1_ref[...] + g2_ref[0] * acc_ref[...]


def _moe_call(h2, route, w1, w3, w2, x1, gate2, seq):
    tokens = h2.shape[0]
    tm = 1024
    per_seq = seq // tm
    row = lambda width: pl.BlockSpec((tm, width), lambda i, e: (i, 0))
    return pl.pallas_call(
        _moe_kernel,
        grid=(tokens // tm, N_EXPERTS),
        in_specs=[row(D_MODEL), row(ROUTER_LANES),
                  pl.BlockSpec((1, D_MODEL, D_EXPERT), lambda i, e: (e, 0, 0)),
                  pl.BlockSpec((1, D_MODEL, D_EXPERT), lambda i, e: (e, 0, 0)),
                  pl.BlockSpec((1, D_EXPERT, D_MODEL), lambda i, e: (e, 0, 0)),
                  row(D_MODEL),
                  pl.BlockSpec((1, 1, D_MODEL), lambda i, e: (i // per_seq, 0, 0))],
        out_specs=row(D_MODEL),
        out_shape=jax.ShapeDtypeStruct((tokens, D_MODEL), F32),
        scratch_shapes=[pltpu.VMEM((tm, D_MODEL), F32)],
        compiler_params=pltpu.CompilerParams(
            dimension_semantics=("arbitrary", "arbitrary"), vmem_limit_bytes=VMEM_LIMIT),
        name="moe_dense",
    )(h2, route, w1, w3, w2, x1, gate2)


def _rope_tables(seq):
    half = HEAD // 2
    inv_freq = ROPE_THETA ** (-jnp.arange(half, dtype=F32) / half)
    ang = jnp.arange(seq, dtype=F32)[:, None] * inv_freq[None, :]
    cos = jnp.cos(ang)
    sin = jnp.sin(ang)
    cos_head = jnp.concatenate([cos, cos], axis=1)
    sin_head = jnp.concatenate([-sin, sin], axis=1)
    return jnp.tile(cos_head, (1, RWKV_HEADS)), jnp.tile(sin_head, (1, RWKV_HEADS))


def _layer(x, c, w_ada, b_ada, ln1_g, ln2_g, w_in, mu_shift, w0, w_lora_up, a0, a_lora_up,
           g_lora_up, k_k, k_a, r_k, lnx_g, lnx_b, q_norm_g, k_norm_g, w_br_rwkv, w_br_moba,
           w_out, w_rg, b_rg, w_re, b_re, w1, w3, w2):
    batch, seq, _ = x.shape
    vec = lambda a: a.reshape(1, -1)

    mod = _mod_call(c, w_ada, b_ada)
    shift1, scale1, gate1, shift2, scale2, gate2 = (
        m.reshape(batch, 1, D_MODEL) for m in jnp.split(mod, 6, axis=-1))

    idx = np.arange(WIDTH)
    ones_bd = jnp.asarray(idx[:, None] // HEAD == idx[None, :] // HEAD, BF16)
    tri = jnp.asarray(np.tril(np.ones((CHUNK, CHUNK))), F32)
    cos, sin_signed = _rope_tables(seq)

    z_rwkv, q, k, v, gates, kmean = _inproj_call(
        x, shift1, scale1, vec(ln1_g), w_in.astype(BF16),
        vec(jnp.tile(q_norm_g, RWKV_HEADS)), vec(jnp.tile(k_norm_g, RWKV_HEADS)),
        cos, sin_signed, ones_bd)

    lora_cat = jnp.zeros((DECAY_LORA + AAA_LORA, 2 * WIDTH), F32)
    lora_cat = lora_cat.at[:DECAY_LORA, :WIDTH].set(w_lora_up).at[DECAY_LORA:, WIDTH:].set(a_lora_up)
    y_a = _rwkv_call(z_rwkv, vec(mu_shift), vec(w0), vec(a0), lora_cat, g_lora_up, vec(k_k),
                     vec(k_a), vec(r_k), vec(lnx_g), vec(lnx_b), ones_bd, tri)

    y_b = _moba_call(q, k, v, kmean.reshape(batch, seq // MOBA_BLOCK, WIDTH))

    tokens = batch * seq
    w_router = jnp.zeros((D_MODEL, ROUTER_LANES), F32)
    w_router = w_router.at[:, :N_GROUPS].set(w_rg).at[:, EXPERT_LANE0:EXPERT_LANE0 + N_EXPERTS].set(w_re)
    b_router = jnp.zeros((1, ROUTER_LANES), F32)
    b_router = b_router.at[0, :N_GROUPS].set(b_rg).at[0, EXPERT_LANE0:EXPERT_LANE0 + N_EXPERTS].set(b_re)
    x1, h2, route = _merge_call(
        x.reshape(tokens, D_MODEL), y_a.reshape(tokens, WIDTH), y_b.reshape(tokens, WIDTH),
        gates.reshape(tokens, GATE_COLS), w_br_rwkv.astype(BF16), w_br_moba.astype(BF16),
        w_out.astype(BF16), gate1, vec(ln2_g), scale2, shift2, w_router, b_router, seq)

    out = _moe_call(h2, route, w1.astype(BF16), w3.astype(BF16), w2.astype(BF16), x1, gate2, seq)
    return out.reshape(batch, seq, D_MODEL)


def kernel(x, c, w_ada, b_ada, ln1_g, ln2_g, w_in, mu_shift, w0, w_lora_up, a0, a_lora_up,
           g_lora_up, k_k, k_a, r_k, lnx_g, lnx_b, q_norm_g, k_norm_g, w_br_rwkv, w_br_moba,
           w_out, w_rg, b_rg, w_re, b_re, w1, w3, w2):
    assert w_ada.shape[0] == 1, "single-layer problem"
    layer_params = (w_ada, b_ada, ln1_g, ln2_g, w_in, mu_shift, w0, w_lora_up, a0, a_lora_up,
                    g_lora_up, k_k, k_a, r_k, lnx_g, lnx_b, q_norm_g, k_norm_g, w_br_rwkv,
                    w_br_moba, w_out, w_rg, b_rg, w_re, b_re, w1, w3, w2)
    return _layer(x, c, *(p[0] for p in layer_params))
```

```python
import functools

import jax
import jax.numpy as jnp
import numpy as np
from jax import lax
from jax.experimental import pallas as pl
from jax.experimental.pallas import tpu as pltpu

F32 = jnp.float32
BF16 = jnp.bfloat16
HIGHEST = lax.Precision.HIGHEST

D_MODEL = 1024
RWKV_HEADS = 8
HEAD = 64
WIDTH = RWKV_HEADS * HEAD
DECAY_LORA = 64
AAA_LORA = 64
GATE_LORA = 128
RWKV_COLS = 3 * WIDTH + DECAY_LORA + AAA_LORA + GATE_LORA
ATT_COLS = 3 * WIDTH
GATE_COLS = 2 * D_MODEL
IN_COLS = RWKV_COLS + ATT_COLS + GATE_COLS
DECAY_SCALE = 0.606531
LN_X_EPS = 64e-5
MOBA_BLOCK = 256
MOBA_TOPK = 3
ROPE_THETA = 10000.0
N_GROUPS = 4
EXPERTS_PER_GROUP = 8
N_EXPERTS = N_GROUPS * EXPERTS_PER_GROUP
D_EXPERT = D_MODEL // 2
NORM_EPS = 1e-6
NEG_INF = -1e30

LANES = 128
PAIRS = WIDTH // LANES
CHUNK = 64
SOLVE_BLOCK = 16
RWKV_ROWS = 2 * CHUNK
MOBA_SUB = 128
VMEM_LIMIT = 56 * 1024 * 1024

ROUTER_LANES = LANES
EXPERT_LANE0 = N_GROUPS


def _dot(a, b):
    return jnp.dot(a.astype(BF16), b.astype(BF16), preferred_element_type=F32)


def _dot_nt(a, b):
    return lax.dot_general(a.astype(BF16), b.astype(BF16), (((1,), (1,)), ((), ())),
                           preferred_element_type=F32)


def _dot_f32(a, b):
    return jnp.dot(a, b, precision=HIGHEST, preferred_element_type=F32)


def _seg_sum(x, ones_bd):
    hi = x.astype(BF16)
    lo = (x - hi.astype(F32)).astype(BF16)
    return (jnp.dot(hi, ones_bd, preferred_element_type=F32)
            + jnp.dot(lo, ones_bd, preferred_element_type=F32))


def _iota(shape, axis):
    return lax.broadcasted_iota(jnp.int32, shape, axis)


def _mod_kernel(c_ref, w_ref, b_ref, o_ref):
    c = c_ref[...]
    o_ref[...] = _dot_f32(c * jax.nn.sigmoid(c), w_ref[...]) + b_ref[...]


def _mod_call(c, w_ada, b_ada):
    batch = c.shape[0]
    n_out = w_ada.shape[1]
    tn = D_MODEL
    return pl.pallas_call(
        _mod_kernel,
        grid=(n_out // tn,),
        in_specs=[pl.BlockSpec((batch, D_MODEL), lambda j: (0, 0)),
                  pl.BlockSpec((D_MODEL, tn), lambda j: (0, j)),
                  pl.BlockSpec((1, tn), lambda j: (0, j))],
        out_specs=pl.BlockSpec((batch, tn), lambda j: (0, j)),
        out_shape=jax.ShapeDtypeStruct((batch, n_out), F32),
        name="adaln_mod",
    )(c, w_ada, b_ada.reshape(1, n_out))


def _swap_halves(x):
    first = (_iota(x.shape, 1) & (HEAD - 1)) < HEAD // 2
    up = pltpu.roll(x, LANES - HEAD // 2, axis=1)
    down = pltpu.roll(x, HEAD // 2, axis=1)
    return jnp.where(first, up, down)


def _head_norm_rope(x, gain, cos, sin_signed, ones_bd):
    ms = _seg_sum(x * x, ones_bd) * (1.0 / HEAD)
    y = x * lax.rsqrt(ms + NORM_EPS) * gain
    cols = []
    for p in range(PAIRS):
        sl = slice(p * LANES, (p + 1) * LANES)
        yb = y[:, sl]
        cols.append(yb * cos[:, sl] + _swap_halves(yb) * sin_signed[:, sl])
    return jnp.concatenate(cols, axis=1)


def _inproj_kernel(x_ref, shift_ref, scale_ref, g_ref, w_ref, qg_ref, kg_ref, cos_ref, sin_ref,
                   ones_ref, zr_ref, q_ref, k_ref, v_ref, gate_ref, kmean_ref):
    x = x_ref[0]
    ms = jnp.mean(x * x, axis=-1, keepdims=True)
    h = x * lax.rsqrt(ms + NORM_EPS) * g_ref[...]
    h = (h * (1.0 + scale_ref[0]) + shift_ref[0]).astype(BF16)

    zr_ref[0] = jnp.dot(h, w_ref[:, 0:RWKV_COLS], preferred_element_type=F32)

    za = jnp.dot(h, w_ref[:, RWKV_COLS:RWKV_COLS + ATT_COLS], preferred_element_type=F32)
    ones_bd = ones_ref[...]
    cos = cos_ref[...]
    sin = sin_ref[...]
    q = _head_norm_rope(za[:, 0:WIDTH], qg_ref[...], cos, sin, ones_bd)
    k = _head_norm_rope(za[:, WIDTH:2 * WIDTH], kg_ref[...], cos, sin, ones_bd)
    q_ref[0] = (q * (HEAD ** -0.5)).astype(BF16)
    k_ref[0] = k.astype(BF16)
    v_ref[0] = za[:, 2 * WIDTH:3 * WIDTH].astype(BF16)
    kmean_ref[0] = jnp.mean(k, axis=0, keepdims=True)

    zg = jnp.dot(h, w_ref[:, RWKV_COLS + ATT_COLS:IN_COLS], preferred_element_type=F32)
    gate_ref[0] = jax.nn.sigmoid(zg).astype(BF16)


def _inproj_call(x, shift1, scale1, ln1_g, w_in_bf16, q_gain, k_gain, cos, sin_signed, ones_bd):
    batch, seq, _ = x.shape
    tm = MOBA_BLOCK
    n_t = seq // tm
    row = lambda width: pl.BlockSpec((1, tm, width), lambda b, i: (b, i, 0))
    per_batch = pl.BlockSpec((1, 1, D_MODEL), lambda b, i: (b, 0, 0))
    const = lambda shape: pl.BlockSpec(shape, lambda b, i: (0,) * len(shape))
    return pl.pallas_call(
        _inproj_kernel,
        grid=(batch, n_t),
        in_specs=[row(D_MODEL), per_batch, per_batch, const((1, D_MODEL)),
                  const((D_MODEL, IN_COLS)), const((1, WIDTH)), const((1, WIDTH)),
                  pl.BlockSpec((tm, WIDTH), lambda b, i: (i, 0)),
                  pl.BlockSpec((tm, WIDTH), lambda b, i: (i, 0)),
                  const((WIDTH, WIDTH))],
        out_specs=[row(RWKV_COLS), row(WIDTH), row(WIDTH), row(WIDTH), row(GATE_COLS),
                   pl.BlockSpec((1, 1, WIDTH), lambda b, i: (b * n_t + i, 0, 0))],
        out_shape=[jax.ShapeDtypeStruct((batch, seq, RWKV_COLS), F32),
                   jax.ShapeDtypeStruct((batch, seq, WIDTH), BF16),
                   jax.ShapeDtypeStruct((batch, seq, WIDTH), BF16),
                   jax.ShapeDtypeStruct((batch, seq, WIDTH), BF16),
                   jax.ShapeDtypeStruct((batch, seq, GATE_COLS), BF16),
                   jax.ShapeDtypeStruct((batch * n_t, 1, WIDTH), F32)],
        compiler_params=pltpu.CompilerParams(
            dimension_semantics=("arbitrary", "arbitrary"), vmem_limit_bytes=VMEM_LIMIT),
        name="inproj",
    )(x, shift1, scale1, ln1_g, w_in_bf16, q_gain, k_gain, cos, sin_signed, ones_bd)


def _stack_heads(x):
    first = _iota(x.shape, 1) < HEAD
    return jnp.concatenate([jnp.where(first, x, 0.0), jnp.where(first, 0.0, x)], axis=0)


def _add_eye(x, eye):
    return jnp.where(eye, x + 1.0, x)


def _rwkv_kernel(z_ref, mu_ref, w0_ref, a0_ref, lora_ref, glora_ref, kk_ref, ka_ref, rk_ref,
                 lng_ref, lnb_ref, ones_ref, tri_ref, o_ref, prev_ref, state_ref):
    c = pl.program_id(1)

    @pl.when(c == 0)
    def _():
        prev_ref[...] = jnp.zeros_like(prev_ref)
        state_ref[...] = jnp.zeros_like(state_ref)

    z = z_ref[0]
    rows = z.shape[0]
    n_chunks = rows // CHUNK
    row = _iota(z.shape, 0)
    z_prev = jnp.where(row == 0, prev_ref[...], pltpu.roll(z, 1, axis=0))
    prev_ref[...] = z[rows - 1:rows, :]
    zs = z + (z_prev - z) * mu_ref[...]

    r = zs[:, 0:WIDTH]
    k = zs[:, WIDTH:2 * WIDTH]
    v = zs[:, 2 * WIDTH:3 * WIDTH]
    lo = 3 * WIDTH
    d_wa = zs[:, lo:lo + DECAY_LORA + AAA_LORA]
    d_g = zs[:, lo + DECAY_LORA + AAA_LORA:RWKV_COLS]
    is_decay = _iota(d_wa.shape, 1) < DECAY_LORA
    pre = _dot_f32(jnp.where(is_decay, jnp.tanh(d_wa), d_wa), lora_ref[...])
    log_w = -DECAY_SCALE * jax.nn.sigmoid(w0_ref[...] + pre[:, 0:WIDTH])
    a = jax.nn.sigmoid(a0_ref[...] + pre[:, WIDTH:2 * WIDTH])
    g = _dot_f32(jax.nn.sigmoid(d_g), glora_ref[...])

    ones_bd = ones_ref[...]
    kk = k * kk_ref[...]
    kk = kk / jnp.maximum(jnp.sqrt(_seg_sum(kk * kk, ones_bd)), 1e-12)
    k = k * (1.0 + (a - 1.0) * ka_ref[...])
    bonus = _seg_sum(r * k * rk_ref[...], ones_bd) * v

    cl = _dot_f32(tri_ref[...], log_w)
    p_end_rows = [cl[(c_i + 1) * CHUNK - 1:(c_i + 1) * CHUNK, :] for c_i in range(n_chunks)]
    cl_last = p_end_rows[0]
    chunk_of_row = _iota(cl.shape, 0) // CHUNK
    for c_i in range(1, n_chunks):
        cl_last = jnp.where(chunk_of_row == c_i, p_end_rows[c_i], cl_last)
    a_t = -kk * jnp.exp(cl - log_w)
    e_neg = jnp.exp(-cl)
    b_t = kk * a * e_neg
    k_t = k * e_neg
    r_t = r * jnp.exp(cl)
    e_end = jnp.exp(cl_last - cl)
    b_end = kk * a * e_end
    k_end = k * e_end
    p_end = jnp.exp(cl_last)

    n2 = 2 * CHUNK
    ri = _iota((n2, n2), 0)
    ci = _iota((n2, n2), 1)
    eye = ri == ci
    same_blk = (ri // SOLVE_BLOCK) == (ci // SOLVE_BLOCK)
    ri4 = _iota((2 * n2, 2 * n2), 0)
    ci4 = _iota((2 * n2, 2 * n2), 1) & (n2 - 1)
    causal4 = ((ri4 < n2) & (ri4 > ci4)) | ((ri4 >= n2) & ((ri4 - n2) >= ci4))
    zeros = jnp.zeros((n2, n2), F32)

    items = [(c_i, p) for c_i in range(n_chunks) for p in range(PAIRS)]

    def tile(t, item):
        c_i, p = item
        return t[c_i * CHUNK:(c_i + 1) * CHUNK, p * LANES:(p + 1) * LANES]

    sa = [_stack_heads(tile(a_t, it)) for it in items]
    sr = [_stack_heads(tile(r_t, it)) for it in items]
    sv = [_stack_heads(tile(v, it)) for it in items]
    big = [jnp.where(causal4,
                     _dot_nt(jnp.concatenate([sa[i], sr[i]], axis=0),
                             jnp.concatenate([_stack_heads(tile(b_t, it)),
                                              _stack_heads(tile(k_t, it))], axis=0)), 0.0)
           for i, it in enumerate(items)]
    a_ab = [b[0:n2, 0:n2] for b in big]
    a_ak = [b[0:n2, n2:2 * n2] for b in big]
    a_rbk = [b[n2:2 * n2, :] for b in big]

    d1 = [jnp.where(same_blk, a, 0.0) for a in a_ab]
    e1 = [a - d for a, d in zip(a_ab, d1)]
    d2 = [_dot(d, d) for d in d1]
    d4 = [_dot(d, d) for d in d2]
    d8 = [_dot(d, d) for d in d4]
    p12 = [_dot(_add_eye(x, eye), _add_eye(y, eye)) for x, y in zip(d1, d2)]
    p48 = [_dot(_add_eye(x, eye), _add_eye(y, eye)) for x, y in zip(d4, d8)]
    t_d = [_dot(x, y) for x, y in zip(p12, p48)]
    g1 = [_dot(t, e) for t, e in zip(t_d, e1)]
    g2 = [_dot(g, g) for g in g1]
    gx = [_dot(_add_eye(x, eye), _add_eye(y, eye)) for x, y in zip(g1, g2)]
    t_inv = [_dot(x, t) for x, t in zip(gx, t_d)]

    akv = [_dot(a, s) for a, s in zip(a_ak, sv)]
    wu = [_dot(t, jnp.concatenate([s, x], axis=1)) for t, s, x in zip(t_inv, sa, akv)]
    rhs = [jnp.concatenate([w, jnp.concatenate([zeros, s], axis=1)], axis=0)
           for w, s in zip(wu, sv)]
    out_c = [_dot(a, x) for a, x in zip(a_rbk, rhs)]
    end_t = [jnp.concatenate([_stack_heads(tile(b_end, it)).T,
                              _stack_heads(tile(k_end, it)).T], axis=1) for it in items]
    end_c = [_dot(e, x) for e, x in zip(end_t, rhs)]

    y_rows = []
    for c_i in range(n_chunks):
        y_cols = []
        for p in range(PAIRS):
            i = c_i * PAIRS + p
            psi = sr[i] + out_c[i][:, 0:n2]
            pe = p_end[c_i * CHUNK:c_i * CHUNK + 1, p * LANES:(p + 1) * LANES]
            phi = jnp.where(eye, pe, 0.0) + end_c[i][:, 0:n2]
            state = state_ref[p]
            both = _dot_f32(jnp.concatenate([psi, phi], axis=0), state)
            y = both[0:n2, :] + out_c[i][:, n2:2 * n2]
            state_ref[p] = both[n2:2 * n2, :] + end_c[i][:, n2:2 * n2]
            y_cols.append(y[0:CHUNK, :] + y[CHUNK:n2, :])
        y_rows.append(jnp.concatenate(y_cols, axis=1))
    y = jnp.concatenate(y_rows, axis=0)
    mean = _seg_sum(y, ones_bd) * (1.0 / HEAD)
    yc = y - mean
    var = _seg_sum(yc * yc, ones_bd) * (1.0 / HEAD)
    yn = yc * lax.rsqrt(var + LN_X_EPS) * lng_ref[...] + lnb_ref[...]
    o_ref[0] = ((yn + bonus) * g).astype(o_ref.dtype)


def _rwkv_call(z_rwkv, mu_shift, w0, a0, lora_cat, g_lora_up, k_k, k_a, r_k, lnx_g, lnx_b,
               ones_bd, tri):
    batch, seq, _ = z_rwkv.shape
    rows = RWKV_ROWS
    const = lambda shape: pl.BlockSpec(shape, lambda b, c: (0,) * len(shape))
    vec = const((1, WIDTH))
    return pl.pallas_call(
        _rwkv_kernel,
        grid=(batch, seq // rows),
        in_specs=[pl.BlockSpec((1, rows, RWKV_COLS), lambda b, c: (b, c, 0)),
                  const((1, RWKV_COLS)), vec, vec,
                  const((DECAY_LORA + AAA_LORA, 2 * WIDTH)), const((GATE_LORA, WIDTH)),
                  vec, vec, vec, vec, vec, const((WIDTH, WIDTH)), const((rows, rows))],
        out_specs=pl.BlockSpec((1, rows, WIDTH), lambda b, c: (b, c, 0)),
        out_shape=jax.ShapeDtypeStruct((batch, seq, WIDTH), BF16),
        scratch_shapes=[pltpu.VMEM((1, RWKV_COLS), F32),
                        pltpu.VMEM((PAIRS, LANES, LANES), F32)],
        compiler_params=pltpu.CompilerParams(
            dimension_semantics=("arbitrary", "arbitrary"), vmem_limit_bytes=VMEM_LIMIT),
        name="rwkv7",
    )(z_rwkv, mu_shift, w0, a0, lora_cat, g_lora_up, k_k, k_a, r_k, lnx_g, lnx_b, ones_bd, tri)


def _moba_kernel(q_ref, k_ref, v_ref, km_ref, o_ref, bias_ref, m_ref, l_ref, acc_ref):
    qi = pl.program_id(2)
    tq = q_ref.shape[1]
    n_blk = km_ref.shape[1]
    q = q_ref[0]
    first = _iota((tq, LANES), 1) < HEAD
    zero = jnp.zeros_like(q)
    qh = [jnp.where(first, q, zero), jnp.where(first, zero, q)]

    km = km_ref[0]
    blk = _iota((n_blk, tq), 0)
    valid = blk < qi
    for h in range(2):
        gate = lax.dot_general(km, qh[h].astype(F32), (((1,), (1,)), ((), ())),
                               precision=HIGHEST, preferred_element_type=F32)
        gate = jnp.where(valid, gate, NEG_INF)
        bias = jnp.full((n_blk, tq), NEG_INF, F32)
        for n in range(n_blk):
            g_n = gate[n:n + 1, :]
            ahead = valid & ((gate > g_n) | ((gate == g_n) & (blk < n)))
            rank = jnp.sum(ahead.astype(F32), axis=0, keepdims=True)
            chosen = (rank < MOBA_TOPK) & (qi > n)
            bias = jnp.where((blk == n) & chosen, 0.0, bias)
        bias_ref[h] = bias

    m_ref[...] = jnp.full(m_ref.shape, NEG_INF, F32)
    l_ref[...] = jnp.zeros(l_ref.shape, F32)
    acc_ref[...] = jnp.zeros(acc_ref.shape, F32)
    eye = (_iota((tq, tq), 0) == _iota((tq, tq), 1)).astype(BF16)
    tiles = [(h, r0) for h in range(2) for r0 in range(0, tq, MOBA_SUB)]
    q_tiles = [qh[h][r0:r0 + MOBA_SUB] for h, r0 in tiles]

    def update(scores, v_blk):
        n_keys = v_blk.shape[0]
        v_ext = jnp.concatenate([v_blk, jnp.ones((n_keys, LANES), BF16)], axis=1)
        refs = [(h, slice(r0, r0 + MOBA_SUB)) for h, r0 in tiles]
        m_old = [m_ref[h, rows, :] for h, rows in refs]
        l_old = [l_ref[h, rows, :] for h, rows in refs]
        acc_old = [acc_ref[h, rows, :] for h, rows in refs]
        m_new = [jnp.maximum(m, jnp.broadcast_to(jnp.max(s, axis=-1, keepdims=True),
                                                 (MOBA_SUB, LANES)))
                 for m, s in zip(m_old, scores)]
        alpha = [jnp.exp(m - n) for m, n in zip(m_old, m_new)]
        p = [jnp.exp(s - jnp.concatenate([n] * (n_keys // LANES), axis=1)).astype(BF16)
             for s, n in zip(scores, m_new)]
        pv = [jnp.dot(x, v_ext, preferred_element_type=F32) for x in p]
        for (h, rows), m, a, l, acc, y in zip(refs, m_new, alpha, l_old, acc_old, pv):
            m_ref[h, rows, :] = m
            l_ref[h, rows, :] = a * l + y[:, LANES:2 * LANES]
            acc_ref[h, rows, :] = a * acc + y[:, 0:LANES]

    def past_blocks(j, carry):
        start = pl.multiple_of(j * (2 * MOBA_BLOCK), 2 * MOBA_BLOCK)
        k_blk = k_ref[0, pl.ds(start, 2 * MOBA_BLOCK), :]
        v_blk = v_ref[0, pl.ds(start, 2 * MOBA_BLOCK), :]
        bias_rows = [jnp.concatenate(
            [jnp.broadcast_to(bias_ref[h, pl.ds(2 * j + i, 1), :], (MOBA_BLOCK, tq))
             for i in range(2)], axis=0).astype(BF16) for h in range(2)]
        scores = [_dot_nt(qt, k_blk) + _dot_nt(eye[r0:r0 + MOBA_SUB], bias_rows[h])
                  for qt, (h, r0) in zip(q_tiles, tiles)]
        update(scores, v_blk)
        return carry

    lax.fori_loop(0, (qi + 1) // 2, past_blocks, 0)

    start = pl.multiple_of(qi * MOBA_BLOCK, MOBA_BLOCK)
    k_blk = k_ref[0, pl.ds(start, MOBA_BLOCK), :]
    v_blk = v_ref[0, pl.ds(start, MOBA_BLOCK), :]
    row = _iota((MOBA_SUB, MOBA_BLOCK), 0)
    col = _iota((MOBA_SUB, MOBA_BLOCK), 1)
    update([jnp.where(col <= row + r0, _dot_nt(qt, k_blk), NEG_INF)
            for qt, (h, r0) in zip(q_tiles, tiles)], v_blk)

    o_ref[0] = jnp.where(first, acc_ref[0] / l_ref[0], acc_ref[1] / l_ref[1]).astype(o_ref.dtype)


def _moba_call(q, k, v, kmean):
    batch, seq, _ = q.shape
    tq = MOBA_BLOCK
    n_blk = seq // MOBA_BLOCK
    return pl.pallas_call(
        _moba_kernel,
        grid=(batch, PAIRS, seq // tq),
        in_specs=[pl.BlockSpec((1, tq, LANES), lambda b, p, i: (b, i, p)),
                  pl.BlockSpec((1, seq, LANES), lambda b, p, i: (b, 0, p)),
                  pl.BlockSpec((1, seq, LANES), lambda b, p, i: (b, 0, p)),
                  pl.BlockSpec((1, n_blk, LANES), lambda b, p, i: (b, 0, p))],
        out_specs=pl.BlockSpec((1, tq, LANES), lambda b, p, i: (b, i, p)),
        out_shape=jax.ShapeDtypeStruct((batch, seq, WIDTH), BF16),
        scratch_shapes=[pltpu.VMEM((2, n_blk, tq), F32),
                        pltpu.VMEM((2, tq, LANES), F32),
                        pltpu.VMEM((2, tq, LANES), F32),
                        pltpu.VMEM((2, tq, LANES), F32)],
        compiler_params=pltpu.CompilerParams(
            dimension_semantics=("arbitrary", "arbitrary", "arbitrary"),
            vmem_limit_bytes=VMEM_LIMIT),
        name="moba",
    )(q, k, v, kmean)


def _merge_kernel(x_ref, ya_ref, yb_ref, gate_ref, wa_ref, wb_ref, wo_ref, g1_ref, ln_ref,
                  scale_ref, shift_ref, wr_ref, br_ref, x1_ref, h2_ref, route_ref):
    ya = jnp.dot(ya_ref[...], wa_ref[...], preferred_element_type=F32)
    yb = jnp.dot(yb_ref[...], wb_ref[...], preferred_element_type=F32)
    gates = gate_ref[...]
    merged = (gates[:, 0:D_MODEL].astype(F32) * ya + gates[:, D_MODEL:GATE_COLS].astype(F32) * yb)
    x1 = x_ref[...] + g1_ref[0] * jnp.dot(merged.astype(BF16), wo_ref[...],
                                          preferred_element_type=F32)
    x1_ref[...] = x1

    ms = jnp.mean(x1 * x1, axis=-1, keepdims=True)
    h2 = x1 * lax.rsqrt(ms + NORM_EPS) * ln_ref[...]
    h2 = h2 * (1.0 + scale_ref[0]) + shift_ref[0]
    h2_ref[...] = h2.astype(BF16)

    logits = _dot_f32(h2, wr_ref[...]) + br_ref[...]
    lane = _iota(logits.shape, 1)
    lane_f = lane.astype(F32)
    far = float(ROUTER_LANES)

    def top(vals):
        m = jnp.max(vals, axis=-1, keepdims=True)
        idx = jnp.min(jnp.where(vals == m, lane_f, far), axis=-1, keepdims=True)
        return m, idx

    grp = jnp.where(lane < N_GROUPS, logits, NEG_INF)
    g_max, g_idx = top(grp)
    p_group = 1.0 / jnp.sum(jnp.exp(grp - g_max), axis=-1, keepdims=True)

    e_lo = EXPERT_LANE0 + EXPERTS_PER_GROUP * g_idx
    in_grp = (lane_f >= e_lo) & (lane_f < e_lo + EXPERTS_PER_GROUP)
    el = jnp.where(in_grp, logits, NEG_INF)
    m1, i1 = top(el)
    m2, i2 = top(jnp.where(lane_f == i1, NEG_INF, el))
    ratio = jnp.exp(m2 - m1)
    w_first = p_group / (1.0 + ratio)
    w_second = w_first * ratio
    route_ref[...] = jnp.where(lane_f == i1, w_first, jnp.where(lane_f == i2, w_second, 0.0))


def _merge_call(x2d, ya, yb, gates, w_br_rwkv, w_br_moba, w_out, gate1, ln2_g, scale2, shift2,
                w_router, b_router, seq):
    tokens = x2d.shape[0]
    tm = 256
    per_seq = seq // tm
    row = lambda width: pl.BlockSpec((tm, width), lambda i: (i, 0))
    per_batch = pl.BlockSpec((1, 1, D_MODEL), lambda i: (i // per_seq, 0, 0))
    const = lambda shape: pl.BlockSpec(shape, lambda i: (0,) * len(shape))
    return pl.pallas_call(
        _merge_kernel,
        grid=(tokens // tm,),
        in_specs=[row(D_MODEL), row(WIDTH), row(WIDTH), row(GATE_COLS),
                  const((WIDTH, D_MODEL)), const((WIDTH, D_MODEL)), const((D_MODEL, D_MODEL)),
                  per_batch, const((1, D_MODEL)), per_batch, per_batch,
                  const((D_MODEL, ROUTER_LANES)), const((1, ROUTER_LANES))],
        out_specs=[row(D_MODEL), row(D_MODEL), row(ROUTER_LANES)],
        out_shape=[jax.ShapeDtypeStruct((tokens, D_MODEL), F32),
                   jax.ShapeDtypeStruct((tokens, D_MODEL), BF16),
                   jax.ShapeDtypeStruct((tokens, ROUTER_LANES), F32)],
        compiler_params=pltpu.CompilerParams(
            dimension_semantics=("arbitrary",), vmem_limit_bytes=VMEM_LIMIT),
        name="merge",
    )(x2d, ya, yb, gates, w_br_rwkv, w_br_moba, w_out, gate1, ln2_g, scale2, shift2,
      w_router, b_router)


def _moe_kernel(h_ref, route_ref, w1_ref, w3_ref, w2_ref, x1_ref, g2_ref, o_ref, acc_ref):
    e = pl.program_id(1)

    @pl.when(e == 0)
    def _():
        acc_ref[...] = jnp.zeros_like(acc_ref)

    h = h_ref[...]
    a = jnp.dot(h, w1_ref[0], preferred_element_type=F32)
    b = jnp.dot(h, w3_ref[0], preferred_element_type=F32)
    hid = (a * jax.nn.sigmoid(a) * b).astype(BF16)
    y = jnp.dot(hid, w2_ref[0], preferred_element_type=F32)
    route = route_ref[...]
    lane = _iota(route.shape, 1)
    w_col = jnp.sum(jnp.where(lane == e + EXPERT_LANE0, route, 0.0), axis=-1, keepdims=True)
    acc_ref[...] += w_col * y

    @pl.when(e == N_EXPERTS - 1)
    def _():
        o_ref[...] = x1_ref[...] + g2_ref[0] * acc_ref[...]


def _moe_call(h2, route, w1, w3, w2, x1, gate2, seq):
    tokens = h2.shape[0]
    tm = 1024
    per_seq = seq // tm
    row = lambda width: pl.BlockSpec((tm, width), lambda i, e: (i, 0))
    return pl.pallas_call(
        _moe_kernel,
        grid=(tokens // tm, N_EXPERTS),
        in_specs=[row(D_MODEL), row(ROUTER_LANES),
                  pl.BlockSpec((1, D_MODEL, D_EXPERT), lambda i, e: (e, 0, 0)),
                  pl.BlockSpec((1, D_MODEL, D_EXPERT), lambda i, e: (e, 0, 0)),
                  pl.BlockSpec((1, D_EXPERT, D_MODEL), lambda i, e: (e, 0, 0)),
                  row(D_MODEL),
                  pl.BlockSpec((1, 1, D_MODEL), lambda i, e: (i // per_seq, 0, 0))],
        out_specs=row(D_MODEL),
        out_shape=jax.ShapeDtypeStruct((tokens, D_MODEL), F32),
        scratch_shapes=[pltpu.VMEM((tm, D_MODEL), F32)],
        compiler_params=pltpu.CompilerParams(
            dimension_semantics=("arbitrary", "arbitrary"), vmem_limit_bytes=VMEM_LIMIT),
        name="moe_dense",
    )(h2, route, w1, w3, w2, x1, gate2)


def _rope_tables(seq):
    half = HEAD // 2
    inv_freq = ROPE_THETA ** (-jnp.arange(half, dtype=F32) / half)
    ang = jnp.arange(seq, dtype=F32)[:, None] * inv_freq[None, :]
    cos = jnp.cos(ang)
    sin = jnp.sin(ang)
    cos_head = jnp.concatenate([cos, cos], axis=1)
    sin_head = jnp.concatenate([-sin, sin], axis=1)
    return jnp.tile(cos_head, (1, RWKV_HEADS)), jnp.tile(sin_head, (1, RWKV_HEADS))


def _layer(x, c, w_ada, b_ada, ln1_g, ln2_g, w_in, mu_shift, w0, w_lora_up, a0, a_lora_up,
           g_lora_up, k_k, k_a, r_k, lnx_g, lnx_b, q_norm_g, k_norm_g, w_br_rwkv, w_br_moba,
           w_out, w_rg, b_rg, w_re, b_re, w1, w3, w2):
    batch, seq, _ = x.shape
    vec = lambda a: a.reshape(1, -1)

    mod = _mod_call(c, w_ada, b_ada)
    shift1, scale1, gate1, shift2, scale2, gate2 = (
        m.reshape(batch, 1, D_MODEL) for m in jnp.split(mod, 6, axis=-1))

    idx = np.arange(WIDTH)
    ones_bd = jnp.asarray(idx[:, None] // HEAD == idx[None, :] // HEAD, BF16)
    t_idx = np.arange(RWKV_ROWS)
    tri = jnp.asarray((t_idx[:, None] >= t_idx[None, :])
                      & (t_idx[:, None] // CHUNK == t_idx[None, :] // CHUNK), F32)
    cos, sin_signed = _rope_tables(seq)

    z_rwkv, q, k, v, gates, kmean = _inproj_call(
        x, shift1, scale1, vec(ln1_g), w_in.astype(BF16),
        vec(jnp.tile(q_norm_g, RWKV_HEADS)), vec(jnp.tile(k_norm_g, RWKV_HEADS)),
        cos, sin_signed, ones_bd)

    lora_cat = jnp.zeros((DECAY_LORA + AAA_LORA, 2 * WIDTH), F32)
    lora_cat = lora_cat.at[:DECAY_LORA, :WIDTH].set(w_lora_up).at[DECAY_LORA:, WIDTH:].set(a_lora_up)
    y_a = _rwkv_call(z_rwkv, vec(mu_shift), vec(w0), vec(a0), lora_cat, g_lora_up, vec(k_k),
                     vec(k_a), vec(r_k), vec(lnx_g), vec(lnx_b), ones_bd, tri)

    y_b = _moba_call(q, k, v, kmean.reshape(batch, seq // MOBA_BLOCK, WIDTH))

    tokens = batch * seq
    w_router = jnp.zeros((D_MODEL, ROUTER_LANES), F32)
    w_router = w_router.at[:, :N_GROUPS].set(w_rg).at[:, EXPERT_LANE0:EXPERT_LANE0 + N_EXPERTS].set(w_re)
    b_router = jnp.zeros((1, ROUTER_LANES), F32)
    b_router = b_router.at[0, :N_GROUPS].set(b_rg).at[0, EXPERT_LANE0:EXPERT_LANE0 + N_EXPERTS].set(b_re)
    x1, h2, route = _merge_call(
        x.reshape(tokens, D_MODEL), y_a.reshape(tokens, WIDTH), y_b.reshape(tokens, WIDTH),
        gates.reshape(tokens, GATE_COLS), w_br_rwkv.astype(BF16), w_br_moba.astype(BF16),
        w_out.astype(BF16), gate1, vec(ln2_g), scale2, shift2, w_router, b_router, seq)

    out = _moe_call(h2, route, w1.astype(BF16), w3.astype(BF16), w2.astype(BF16), x1, gate2, seq)
    return out.reshape(batch, seq, D_MODEL)


def kernel(x, c, w_ada, b_ada, ln1_g, ln2_g, w_in, mu_shift, w0, w_lora_up, a0, a_lora_up,
           g_lora_up, k_k, k_a, r_k, lnx_g, lnx_b, q_norm_g, k_norm_g, w_br_rwkv, w_br_moba,
           w_out, w_rg, b_rg, w_re, b_re, w1, w3, w2):
    assert w_ada.shape[0] == 1, "single-layer problem"
    layer_params = (w_ada, b_ada, ln1_g, ln2_g, w_in, mu_shift, w0, w_lora_up, a0, a_lora_up,
                    g_lora_up, k_k, k_a, r_k, lnx_g, lnx_b, q_norm_g, k_norm_g, w_br_rwkv,
                    w_br_moba, w_out, w_rg, b_rg, w_re, b_re, w1, w3, w2)
    return _layer(x, c, *(p[0] for p in layer_params))
```

```python
import functools

import jax
import jax.numpy as jnp
import numpy as np
from jax import lax
from jax.experimental import pallas as pl
from jax.experimental.pallas import tpu as pltpu

F32 = jnp.float32
BF16 = jnp.bfloat16
HIGHEST = lax.Precision.HIGHEST

D_MODEL = 1024
RWKV_HEADS = 8
HEAD = 64
WIDTH = RWKV_HEADS * HEAD
DECAY_LORA = 64
AAA_LORA = 64
GATE_LORA = 128
RWKV_COLS = 3 * WIDTH + DECAY_LORA + AAA_LORA + GATE_LORA
ATT_COLS = 3 * WIDTH
GATE_COLS = 2 * D_MODEL
IN_COLS = RWKV_COLS + ATT_COLS + GATE_COLS
DECAY_SCALE = 0.606531
LN_X_EPS = 64e-5
MOBA_BLOCK = 256
MOBA_TOPK = 3
ROPE_THETA = 10000.0
N_GROUPS = 4
EXPERTS_PER_GROUP = 8
N_EXPERTS = N_GROUPS * EXPERTS_PER_GROUP
D_EXPERT = D_MODEL // 2
NORM_EPS = 1e-6
NEG_INF = -1e30

LANES = 128
PAIRS = WIDTH // LANES
CHUNK = 64
SOLVE_BLOCK = 16
RWKV_ROWS = 2 * CHUNK
MOBA_SUB = 128
VMEM_LIMIT = 56 * 1024 * 1024

ROUTER_LANES = LANES
EXPERT_LANE0 = N_GROUPS


def _dot(a, b):
    return jnp.dot(a.astype(BF16), b.astype(BF16), preferred_element_type=F32)


def _dot_nt(a, b):
    return lax.dot_general(a.astype(BF16), b.astype(BF16), (((1,), (1,)), ((), ())),
                           preferred_element_type=F32)


def _dot_f32(a, b):
    return jnp.dot(a, b, precision=HIGHEST, preferred_element_type=F32)


def _seg_sum(x, ones_bd):
    hi = x.astype(BF16)
    lo = (x - hi.astype(F32)).astype(BF16)
    return (jnp.dot(hi, ones_bd, preferred_element_type=F32)
            + jnp.dot(lo, ones_bd, preferred_element_type=F32))


def _iota(shape, axis):
    return lax.broadcasted_iota(jnp.int32, shape, axis)


def _mod_kernel(c_ref, w_ref, b_ref, o_ref):
    c = c_ref[...]
    o_ref[...] = _dot_f32(c * jax.nn.sigmoid(c), w_ref[...]) + b_ref[...]


def _mod_call(c, w_ada, b_ada):
    batch = c.shape[0]
    n_out = w_ada.shape[1]
    tn = D_MODEL
    return pl.pallas_call(
        _mod_kernel,
        grid=(n_out // tn,),
        in_specs=[pl.BlockSpec((batch, D_MODEL), lambda j: (0, 0)),
                  pl.BlockSpec((D_MODEL, tn), lambda j: (0, j)),
                  pl.BlockSpec((1, tn), lambda j: (0, j))],
        out_specs=pl.BlockSpec((batch, tn), lambda j: (0, j)),
        out_shape=jax.ShapeDtypeStruct((batch, n_out), F32),
        name="adaln_mod",
    )(c, w_ada, b_ada.reshape(1, n_out))


def _swap_halves(x):
    first = (_iota(x.shape, 1) & (HEAD - 1)) < HEAD // 2
    up = pltpu.roll(x, LANES - HEAD // 2, axis=1)
    down = pltpu.roll(x, HEAD // 2, axis=1)
    return jnp.where(first, up, down)


def _head_norm_rope(x, gain, cos, sin_signed, ones_bd):
    ms = _seg_sum(x * x, ones_bd) * (1.0 / HEAD)
    y = x * lax.rsqrt(ms + NORM_EPS) * gain
    cols = []
    for p in range(PAIRS):
        sl = slice(p * LANES, (p + 1) * LANES)
        yb = y[:, sl]
        cols.append(yb * cos[:, sl] + _swap_halves(yb) * sin_signed[:, sl])
    return jnp.concatenate(cols, axis=1)


def _inproj_kernel(x_ref, shift_ref, scale_ref, g_ref, w_ref, qg_ref, kg_ref, cos_ref, sin_ref,
                   ones_ref, zr_ref, q_ref, k_ref, v_ref, gate_ref, kmean_ref):
    x = x_ref[0]
    ms = jnp.mean(x * x, axis=-1, keepdims=True)
    h = x * lax.rsqrt(ms + NORM_EPS) * g_ref[...]
    h = (h * (1.0 + scale_ref[0]) + shift_ref[0]).astype(BF16)

    zr_ref[0] = jnp.dot(h, w_ref[:, 0:RWKV_COLS], preferred_element_type=F32)

    za = jnp.dot(h, w_ref[:, RWKV_COLS:RWKV_COLS + ATT_COLS], preferred_element_type=F32)
    ones_bd = ones_ref[...]
    cos = cos_ref[...]
    sin = sin_ref[...]
    q = _head_norm_rope(za[:, 0:WIDTH], qg_ref[...], cos, sin, ones_bd)
    k = _head_norm_rope(za[:, WIDTH:2 * WIDTH], kg_ref[...], cos, sin, ones_bd)
    q_ref[0] = (q * (HEAD ** -0.5)).astype(BF16)
    k_ref[0] = k.astype(BF16)
    v_ref[0] = za[:, 2 * WIDTH:3 * WIDTH].astype(BF16)
    kmean_ref[0] = jnp.mean(k, axis=0, keepdims=True)

    zg = jnp.dot(h, w_ref[:, RWKV_COLS + ATT_COLS:IN_COLS], preferred_element_type=F32)
    gate_ref[0] = jax.nn.sigmoid(zg).astype(BF16)


def _inproj_call(x, shift1, scale1, ln1_g, w_in_bf16, q_gain, k_gain, cos, sin_signed, ones_bd):
    batch, seq, _ = x.shape
    tm = MOBA_BLOCK
    n_t = seq // tm
    row = lambda width: pl.BlockSpec((1, tm, width), lambda b, i: (b, i, 0))
    per_batch = pl.BlockSpec((1, 1, D_MODEL), lambda b, i: (b, 0, 0))
    const = lambda shape: pl.BlockSpec(shape, lambda b, i: (0,) * len(shape))
    return pl.pallas_call(
        _inproj_kernel,
        grid=(batch, n_t),
        in_specs=[row(D_MODEL), per_batch, per_batch, const((1, D_MODEL)),
                  const((D_MODEL, IN_COLS)), const((1, WIDTH)), const((1, WIDTH)),
                  pl.BlockSpec((tm, WIDTH), lambda b, i: (i, 0)),
                  pl.BlockSpec((tm, WIDTH), lambda b, i: (i, 0)),
                  const((WIDTH, WIDTH))],
        out_specs=[row(RWKV_COLS), row(WIDTH), row(WIDTH), row(WIDTH), row(GATE_COLS),
                   pl.BlockSpec((1, 1, WIDTH), lambda b, i: (b * n_t + i, 0, 0))],
        out_shape=[jax.ShapeDtypeStruct((batch, seq, RWKV_COLS), F32),
                   jax.ShapeDtypeStruct((batch, seq, WIDTH), BF16),
                   jax.ShapeDtypeStruct((batch, seq, WIDTH), BF16),
                   jax.ShapeDtypeStruct((batch, seq, WIDTH), BF16),
                   jax.ShapeDtypeStruct((batch, seq, GATE_COLS), BF16),
                   jax.ShapeDtypeStruct((batch * n_t, 1, WIDTH), F32)],
        compiler_params=pltpu.CompilerParams(
            dimension_semantics=("arbitrary", "arbitrary"), vmem_limit_bytes=VMEM_LIMIT),
        name="inproj",
    )(x, shift1, scale1, ln1_g, w_in_bf16, q_gain, k_gain, cos, sin_signed, ones_bd)


def _stack_heads(x):
    first = _iota(x.shape, 1) < HEAD
    return jnp.concatenate([jnp.where(first, x, 0.0), jnp.where(first, 0.0, x)], axis=0)


def _add_eye(x, eye):
    return jnp.where(eye, x + 1.0, x)


def _rwkv_kernel(z_ref, mu_ref, w0_ref, a0_ref, lora_ref, glora_ref, kk_ref, ka_ref, rk_ref,
                 lng_ref, lnb_ref, ones_ref, tri_ref, o_ref, prev_ref, state_ref):
    c = pl.program_id(1)

    @pl.when(c == 0)
    def _():
        prev_ref[...] = jnp.zeros_like(prev_ref)
        state_ref[...] = jnp.zeros_like(state_ref)

    z = z_ref[0]
    rows = z.shape[0]
    n_chunks = rows // CHUNK
    row = _iota(z.shape, 0)
    z_prev = jnp.where(row == 0, prev_ref[...], pltpu.roll(z, 1, axis=0))
    prev_ref[...] = z[rows - 1:rows, :]
    zs = z + (z_prev - z) * mu_ref[...]

    r = zs[:, 0:WIDTH]
    k = zs[:, WIDTH:2 * WIDTH]
    v = zs[:, 2 * WIDTH:3 * WIDTH]
    lo = 3 * WIDTH
    d_wa = zs[:, lo:lo + DECAY_LORA + AAA_LORA]
    d_g = zs[:, lo + DECAY_LORA + AAA_LORA:RWKV_COLS]
    is_decay = _iota(d_wa.shape, 1) < DECAY_LORA
    pre = _dot_f32(jnp.where(is_decay, jnp.tanh(d_wa), d_wa), lora_ref[...])
    log_w = -DECAY_SCALE * jax.nn.sigmoid(w0_ref[...] + pre[:, 0:WIDTH])
    a = jax.nn.sigmoid(a0_ref[...] + pre[:, WIDTH:2 * WIDTH])
    g = _dot_f32(jax.nn.sigmoid(d_g), glora_ref[...])

    ones_bd = ones_ref[...]
    kk = k * kk_ref[...]
    kk = kk / jnp.maximum(jnp.sqrt(_seg_sum(kk * kk, ones_bd)), 1e-12)
    k = k * (1.0 + (a - 1.0) * ka_ref[...])
    bonus = _seg_sum(r * k * rk_ref[...], ones_bd) * v

    cl = _dot_f32(tri_ref[...], log_w)
    p_end_rows = [cl[(c_i + 1) * CHUNK - 1:(c_i + 1) * CHUNK, :] for c_i in range(n_chunks)]
    cl_last = p_end_rows[0]
    chunk_of_row = _iota(cl.shape, 0) // CHUNK
    for c_i in range(1, n_chunks):
        cl_last = jnp.where(chunk_of_row == c_i, p_end_rows[c_i], cl_last)
    a_t = -kk * jnp.exp(cl - log_w)
    e_neg = jnp.exp(-cl)
    b_t = kk * a * e_neg
    k_t = k * e_neg
    r_t = r * jnp.exp(cl)
    e_end = jnp.exp(cl_last - cl)
    b_end = kk * a * e_end
    k_end = k * e_end
    p_end = jnp.exp(cl_last)

    n2 = 2 * CHUNK
    ri = _iota((n2, n2), 0)
    ci = _iota((n2, n2), 1)
    eye = ri == ci
    same_blk = (ri // SOLVE_BLOCK) == (ci // SOLVE_BLOCK)
    ri4 = _iota((2 * n2, 2 * n2), 0)
    ci4 = _iota((2 * n2, 2 * n2), 1) & (n2 - 1)
    causal4 = ((ri4 < n2) & (ri4 > ci4)) | ((ri4 >= n2) & ((ri4 - n2) >= ci4))
    zeros = jnp.zeros((n2, n2), F32)

    items = [(c_i, p) for c_i in range(n_chunks) for p in range(PAIRS)]

    def tile(t, item):
        c_i, p = item
        return t[c_i * CHUNK:(c_i + 1) * CHUNK, p * LANES:(p + 1) * LANES]

    sa = [_stack_heads(tile(a_t, it)) for it in items]
    sr = [_stack_heads(tile(r_t, it)) for it in items]
    sv = [_stack_heads(tile(v, it)) for it in items]
    big = [jnp.where(causal4,
                     _dot_nt(jnp.concatenate([sa[i], sr[i]], axis=0),
                             jnp.concatenate([_stack_heads(tile(b_t, it)),
                                              _stack_heads(tile(k_t, it))], axis=0)), 0.0)
           for i, it in enumerate(items)]
    a_ab = [b[0:n2, 0:n2] for b in big]
    a_ak = [b[0:n2, n2:2 * n2] for b in big]
    a_rbk = [b[n2:2 * n2, :] for b in big]

    d1 = [jnp.where(same_blk, a, 0.0) for a in a_ab]
    e1 = [a - d for a, d in zip(a_ab, d1)]
    d2 = [_dot(d, d) for d in d1]
    d4 = [_dot(d, d) for d in d2]
    d8 = [_dot(d, d) for d in d4]
    p12 = [_dot(_add_eye(x, eye), _add_eye(y, eye)) for x, y in zip(d1, d2)]
    p48 = [_dot(_add_eye(x, eye), _add_eye(y, eye)) for x, y in zip(d4, d8)]
    t_d = [_dot(x, y) for x, y in zip(p12, p48)]
    g1 = [_dot(t, e) for t, e in zip(t_d, e1)]
    g2 = [_dot(g, g) for g in g1]
    gx = [_dot(_add_eye(x, eye), _add_eye(y, eye)) for x, y in zip(g1, g2)]
    t_inv = [_dot(x, t) for x, t in zip(gx, t_d)]

    akv = [_dot(a, s) for a, s in zip(a_ak, sv)]
    wu = [_dot(t, jnp.concatenate([s, x], axis=1)) for t, s, x in zip(t_inv, sa, akv)]
    rhs = [jnp.concatenate([w, jnp.concatenate([zeros, s], axis=1)], axis=0)
           for w, s in zip(wu, sv)]
    out_c = [_dot(a, x) for a, x in zip(a_rbk, rhs)]
    end_t = [jnp.concatenate([_stack_heads(tile(b_end, it)).T,
                              _stack_heads(tile(k_end, it)).T], axis=1) for it in items]
    end_c = [_dot(e, x) for e, x in zip(end_t, rhs)]

    y_rows = []
    for c_i in range(n_chunks):
        y_cols = []
        for p in range(PAIRS):
            i = c_i * PAIRS + p
            psi = sr[i] + out_c[i][:, 0:n2]
            pe = p_end[c_i * CHUNK:c_i * CHUNK + 1, p * LANES:(p + 1) * LANES]
            phi = jnp.where(eye, pe, 0.0) + end_c[i][:, 0:n2]
            state = state_ref[p]
            both = _dot_f32(jnp.concatenate([psi, phi], axis=0), state)
            y = both[0:n2, :] + out_c[i][:, n2:2 * n2]
            state_ref[p] = both[n2:2 * n2, :] + end_c[i][:, n2:2 * n2]
            y_cols.append(y[0:CHUNK, :] + y[CHUNK:n2, :])
        y_rows.append(jnp.concatenate(y_cols, axis=1))
    y = jnp.concatenate(y_rows, axis=0)
    mean = _seg_sum(y, ones_bd) * (1.0 / HEAD)
    yc = y - mean
    var = _seg_sum(yc * yc, ones_bd) * (1.0 / HEAD)
    yn = yc * lax.rsqrt(var + LN_X_EPS) * lng_ref[...] + lnb_ref[...]
    o_ref[0] = ((yn + bonus) * g).astype(o_ref.dtype)


def _rwkv_call(z_rwkv, mu_shift, w0, a0, lora_cat, g_lora_up, k_k, k_a, r_k, lnx_g, lnx_b,
               ones_bd, tri):
    batch, seq, _ = z_rwkv.shape
    rows = RWKV_ROWS
    const = lambda shape: pl.BlockSpec(shape, lambda b, c: (0,) * len(shape))
    vec = const((1, WIDTH))
    return pl.pallas_call(
        _rwkv_kernel,
        grid=(batch, seq // rows),
        in_specs=[pl.BlockSpec((1, rows, RWKV_COLS), lambda b, c: (b, c, 0)),
                  const((1, RWKV_COLS)), vec, vec,
                  const((DECAY_LORA + AAA_LORA, 2 * WIDTH)), const((GATE_LORA, WIDTH)),
                  vec, vec, vec, vec, vec, const((WIDTH, WIDTH)), const((rows, rows))],
        out_specs=pl.BlockSpec((1, rows, WIDTH), lambda b, c: (b, c, 0)),
        out_shape=jax.ShapeDtypeStruct((batch, seq, WIDTH), BF16),
        scratch_shapes=[pltpu.VMEM((1, RWKV_COLS), F32),
                        pltpu.VMEM((PAIRS, LANES, LANES), F32)],
        compiler_params=pltpu.CompilerParams(
            dimension_semantics=("arbitrary", "arbitrary"), vmem_limit_bytes=VMEM_LIMIT),
        name="rwkv7",
    )(z_rwkv, mu_shift, w0, a0, lora_cat, g_lora_up, k_k, k_a, r_k, lnx_g, lnx_b, ones_bd, tri)


def _moba_kernel(q_ref, k_ref, v_ref, km_ref, o_ref, bias_ref, m_ref, l_ref, acc_ref):
    qi = pl.program_id(2)
    tq = q_ref.shape[1]
    n_blk = km_ref.shape[1]
    q = q_ref[0]
    first = _iota((tq, LANES), 1) < HEAD
    zero = jnp.zeros_like(q)
    qh = [jnp.where(first, q, zero), jnp.where(first, zero, q)]

    km = km_ref[0]
    blk = _iota((n_blk, tq), 0)
    valid = blk < qi
    for h in range(2):
        gate = lax.dot_general(km, qh[h].astype(F32), (((1,), (1,)), ((), ())),
                               precision=HIGHEST, preferred_element_type=F32)
        gate = jnp.where(valid, gate, NEG_INF)
        bias = jnp.full((n_blk, tq), NEG_INF, F32)
        for n in range(n_blk):
            g_n = gate[n:n + 1, :]
            ahead = valid & ((gate > g_n) | ((gate == g_n) & (blk < n)))
            rank = jnp.sum(ahead.astype(F32), axis=0, keepdims=True)
            chosen = (rank < MOBA_TOPK) & (qi > n)
            bias = jnp.where((blk == n) & chosen, 0.0, bias)
        bias_ref[h] = bias

    m_ref[...] = jnp.full(m_ref.shape, NEG_INF, F32)
    l_ref[...] = jnp.zeros(l_ref.shape, F32)
    acc_ref[...] = jnp.zeros(acc_ref.shape, F32)
    eye = (_iota((tq, tq), 0) == _iota((tq, tq), 1)).astype(BF16)
    tiles = [(h, r0) for h in range(2) for r0 in range(0, tq, MOBA_SUB)]
    q_tiles = [qh[h][r0:r0 + MOBA_SUB] for h, r0 in tiles]

    def update(scores, v_blk):
        n_keys = v_blk.shape[0]
        v_ext = jnp.concatenate([v_blk, jnp.ones((n_keys, LANES), BF16)], axis=1)
        refs = [(h, slice(r0, r0 + MOBA_SUB)) for h, r0 in tiles]
        m_old = [m_ref[h, rows, :] for h, rows in refs]
        l_old = [l_ref[h, rows, :] for h, rows in refs]
        acc_old = [acc_ref[h, rows, :] for h, rows in refs]
        m_new = [jnp.maximum(m, jnp.broadcast_to(jnp.max(s, axis=-1, keepdims=True),
                                                 (MOBA_SUB, LANES)))
                 for m, s in zip(m_old, scores)]
        alpha = [jnp.exp(m - n) for m, n in zip(m_old, m_new)]
        p = [jnp.exp(s - jnp.concatenate([n] * (n_keys // LANES), axis=1)).astype(BF16)
             for s, n in zip(scores, m_new)]
        pv = [jnp.dot(x, v_ext, preferred_element_type=F32) for x in p]
        for (h, rows), m, a, l, acc, y in zip(refs, m_new, alpha, l_old, acc_old, pv):
            m_ref[h, rows, :] = m
            l_ref[h, rows, :] = a * l + y[:, LANES:2 * LANES]
            acc_ref[h, rows, :] = a * acc + y[:, 0:LANES]

    def past_blocks(j, carry):
        start = pl.multiple_of(j * (2 * MOBA_BLOCK), 2 * MOBA_BLOCK)
        k_blk = k_ref[0, pl.ds(start, 2 * MOBA_BLOCK), :]
        v_blk = v_ref[0, pl.ds(start, 2 * MOBA_BLOCK), :]
        bias_rows = [jnp.concatenate(
            [jnp.broadcast_to(bias_ref[h, pl.ds(2 * j + i, 1), :], (MOBA_BLOCK, tq))
             for i in range(2)], axis=0).astype(BF16) for h in range(2)]
        scores = [_dot_nt(qt, k_blk) + _dot_nt(eye[r0:r0 + MOBA_SUB], bias_rows[h])
                  for qt, (h, r0) in zip(q_tiles, tiles)]
        update(scores, v_blk)
        return carry

    lax.fori_loop(0, (qi + 1) // 2, past_blocks, 0)

    start = pl.multiple_of(qi * MOBA_BLOCK, MOBA_BLOCK)
    k_blk = k_ref[0, pl.ds(start, MOBA_BLOCK), :]
    v_blk = v_ref[0, pl.ds(start, MOBA_BLOCK), :]
    row = _iota((MOBA_SUB, MOBA_BLOCK), 0)
    col = _iota((MOBA_SUB, MOBA_BLOCK), 1)
    update([jnp.where(col <= row + r0, _dot_nt(qt, k_blk), NEG_INF)
            for qt, (h, r0) in zip(q_tiles, tiles)], v_blk)

    o_ref[0] = jnp.where(first, acc_ref[0] / l_ref[0], acc_ref[1] / l_ref[1]).astype(o_ref.dtype)


def _moba_call(q, k, v, kmean):
    batch, seq, _ = q.shape
    tq = MOBA_BLOCK
    n_blk = seq // MOBA_BLOCK
    return pl.pallas_call(
        _moba_kernel,
        grid=(batch, PAIRS, seq // tq),
        in_specs=[pl.BlockSpec((1, tq, LANES), lambda b, p, i: (b, i, p)),
                  pl.BlockSpec((1, seq, LANES), lambda b, p, i: (b, 0, p)),
                  pl.BlockSpec((1, seq, LANES), lambda b, p, i: (b, 0, p)),
                  pl.BlockSpec((1, n_blk, LANES), lambda b, p, i: (b, 0, p))],
        out_specs=pl.BlockSpec((1, tq, LANES), lambda b, p, i: (b, i, p)),
        out_shape=jax.ShapeDtypeStruct((batch, seq, WIDTH), BF16),
        scratch_shapes=[pltpu.VMEM((2, n_blk, tq), F32),
                        pltpu.VMEM((2, tq, LANES), F32),
                        pltpu.VMEM((2, tq, LANES), F32),
                        pltpu.VMEM((2, tq, LANES), F32)],
        compiler_params=pltpu.CompilerParams(
            dimension_semantics=("arbitrary", "arbitrary", "arbitrary"),
            vmem_limit_bytes=VMEM_LIMIT),
        name="moba",
    )(q, k, v, kmean)


def _pack_bf16_pair(lo, hi):
    lo_bits = lax.bitcast_convert_type(lo.astype(BF16).astype(F32), jnp.uint32)
    hi_bits = lax.bitcast_convert_type(hi.astype(BF16).astype(F32), jnp.uint32)
    return (lo_bits >> 16) | (hi_bits & jnp.uint32(0xFFFF0000))


def _unpack_bf16_pair(u):
    lo = lax.bitcast_convert_type(u << 16, F32)
    hi = lax.bitcast_convert_type(u & jnp.uint32(0xFFFF0000), F32)
    return lo, hi


def _merge_kernel(x_ref, ya_ref, yb_ref, gate_ref, wa_ref, wb_ref, wo_ref, g1_ref, ln_ref,
                  scale_ref, shift_ref, wr_ref, br_ref, tri_ref, x1_ref, h2_ref, route_ref,
                  count_ref, carry_ref):
    @pl.when(pl.program_id(0) == 0)
    def _():
        carry_ref[...] = jnp.zeros_like(carry_ref)

    ya = jnp.dot(ya_ref[...], wa_ref[...], preferred_element_type=F32)
    yb = jnp.dot(yb_ref[...], wb_ref[...], preferred_element_type=F32)
    gates = gate_ref[...]
    merged = (gates[:, 0:D_MODEL].astype(F32) * ya + gates[:, D_MODEL:GATE_COLS].astype(F32) * yb)
    x1 = x_ref[...] + g1_ref[0] * jnp.dot(merged.astype(BF16), wo_ref[...],
                                          preferred_element_type=F32)
    x1_ref[...] = x1

    ms = jnp.mean(x1 * x1, axis=-1, keepdims=True)
    h2 = x1 * lax.rsqrt(ms + NORM_EPS) * ln_ref[...]
    h2 = h2 * (1.0 + scale_ref[0]) + shift_ref[0]
    half = D_MODEL // 2
    h2_ref[...] = _pack_bf16_pair(h2[:, 0:half], h2[:, half:D_MODEL])

    logits = _dot_f32(h2, wr_ref[...]) + br_ref[...]
    lane = _iota(logits.shape, 1)
    lane_f = lane.astype(F32)
    far = float(ROUTER_LANES)

    def top(vals):
        m = jnp.max(vals, axis=-1, keepdims=True)
        idx = jnp.min(jnp.where(vals == m, lane_f, far), axis=-1, keepdims=True)
        return m, idx

    grp = jnp.where(lane < N_GROUPS, logits, NEG_INF)
    g_max, g_idx = top(grp)
    p_group = 1.0 / jnp.sum(jnp.exp(grp - g_max), axis=-1, keepdims=True)

    e_lo = EXPERT_LANE0 + EXPERTS_PER_GROUP * g_idx
    in_grp = (lane_f >= e_lo) & (lane_f < e_lo + EXPERTS_PER_GROUP)
    el = jnp.where(in_grp, logits, NEG_INF)
    m1, i1 = top(el)
    m2, i2 = top(jnp.where(lane_f == i1, NEG_INF, el))
    ratio = jnp.exp(m2 - m1)
    w_first = p_group / (1.0 + ratio)
    w_second = w_first * ratio

    first = lane_f == i1
    second = lane_f == i2
    hits = (first | second).astype(BF16)
    before = carry_ref[...] + jnp.dot(tri_ref[...], hits, preferred_element_type=F32)
    rank1 = jnp.sum(jnp.where(first, before, 0.0), axis=-1, keepdims=True)
    rank2 = jnp.sum(jnp.where(second, before, 0.0), axis=-1, keepdims=True)
    carry = carry_ref[...] + jnp.sum(hits.astype(F32), axis=0, keepdims=True)
    carry_ref[...] = carry
    count_ref[...] = carry

    fields = (i1 - EXPERT_LANE0, i2 - EXPERT_LANE0, w_first, w_second, rank1, rank2)
    route = jnp.zeros(logits.shape, F32)
    for n, field in enumerate(fields):
        route = jnp.where(lane == n, field, route)
    route_ref[...] = route


R_EXPERT1, R_EXPERT2, R_WEIGHT1, R_WEIGHT2, R_RANK1, R_RANK2 = range(6)
MERGE_ROWS = 256


def _merge_call(x2d, ya, yb, gates, w_br_rwkv, w_br_moba, w_out, gate1, ln2_g, scale2, shift2,
                w_router, b_router, tri, seq):
    tokens = x2d.shape[0]
    tm = MERGE_ROWS
    per_seq = seq // tm
    row = lambda width: pl.BlockSpec((tm, width), lambda i: (i, 0))
    per_batch = pl.BlockSpec((1, 1, D_MODEL), lambda i: (i // per_seq, 0, 0))
    const = lambda shape: pl.BlockSpec(shape, lambda i: (0,) * len(shape))
    return pl.pallas_call(
        _merge_kernel,
        grid=(tokens // tm,),
        in_specs=[row(D_MODEL), row(WIDTH), row(WIDTH), row(GATE_COLS),
                  const((WIDTH, D_MODEL)), const((WIDTH, D_MODEL)), const((D_MODEL, D_MODEL)),
                  per_batch, const((1, D_MODEL)), per_batch, per_batch,
                  const((D_MODEL, ROUTER_LANES)), const((1, ROUTER_LANES)), const((tm, tm))],
        out_specs=[row(D_MODEL), row(D_MODEL // 2), row(ROUTER_LANES),
                   const((1, ROUTER_LANES))],
        out_shape=[jax.ShapeDtypeStruct((tokens, D_MODEL), F32),
                   jax.ShapeDtypeStruct((tokens, D_MODEL // 2), jnp.uint32),
                   jax.ShapeDtypeStruct((tokens, ROUTER_LANES), F32),
                   jax.ShapeDtypeStruct((1, ROUTER_LANES), F32)],
        scratch_shapes=[pltpu.VMEM((1, ROUTER_LANES), F32)],
        compiler_params=pltpu.CompilerParams(
            dimension_semantics=("arbitrary",), vmem_limit_bytes=VMEM_LIMIT),
        name="merge",
    )(x2d, ya, yb, gates, w_br_rwkv, w_br_moba, w_out, gate1, ln2_g, scale2, shift2,
      w_router, b_router, tri)


EXPERT_ROWS = 256
DISPATCH_TOKENS = 1024
COMBINE_TOKENS = 256


def _dispatch_kernel(d1_ref, d2_ref, h_hbm, xs_init_hbm, xs_hbm, sem):
    del xs_init_hbm
    base = pl.program_id(0) * DISPATCH_TOKENS

    def issue(t, carry):
        tok = base + t
        src = h_hbm.at[pl.ds(tok, 1)]
        pltpu.make_async_copy(src, xs_hbm.at[pl.ds(d1_ref[tok], 1)], sem).start()
        pltpu.make_async_copy(src, xs_hbm.at[pl.ds(d2_ref[tok], 1)], sem).start()
        return carry

    lax.fori_loop(0, DISPATCH_TOKENS, issue, 0)
    n_rows = 2 * DISPATCH_TOKENS
    pltpu.make_async_copy(h_hbm.at[pl.ds(0, n_rows)], xs_hbm.at[pl.ds(0, n_rows)], sem).wait()


def _dispatch_call(dest1, dest2, h2p, n_rows):
    tokens = h2p.shape[0]
    any_spec = pl.BlockSpec(memory_space=pl.ANY)
    return pl.pallas_call(
        _dispatch_kernel,
        grid_spec=pltpu.PrefetchScalarGridSpec(
            num_scalar_prefetch=2,
            grid=(tokens // DISPATCH_TOKENS,),
            in_specs=[any_spec, any_spec],
            out_specs=any_spec,
            scratch_shapes=[pltpu.SemaphoreType.DMA(())]),
        out_shape=jax.ShapeDtypeStruct((n_rows, D_MODEL // 2), jnp.uint32),
        input_output_aliases={3: 0},
        compiler_params=pltpu.CompilerParams(dimension_semantics=("arbitrary",)),
        name="moe_dispatch",
    )(dest1, dest2, h2p, jnp.zeros((n_rows, D_MODEL // 2), jnp.uint32))


def _expert_kernel(te_ref, nu_ref, xs_ref, w1_ref, w3_ref, w2_ref, ys_ref):
    half = D_MODEL // 2

    @pl.when(pl.program_id(0) < nu_ref[0])
    def _():
        x_lo, x_hi = _unpack_bf16_pair(xs_ref[...])
        x_lo = x_lo.astype(BF16)
        x_hi = x_hi.astype(BF16)

        def proj(w_ref):
            return (jnp.dot(x_lo, w_ref[0, 0:half, :], preferred_element_type=F32)
                    + jnp.dot(x_hi, w_ref[0, half:D_MODEL, :], preferred_element_type=F32))

        a = proj(w1_ref)
        hid = (a * jax.nn.sigmoid(a) * proj(w3_ref)).astype(BF16)
        y = jnp.dot(hid, w2_ref[0], preferred_element_type=F32)
        ys_ref[...] = _pack_bf16_pair(y[:, 0:half], y[:, half:D_MODEL])

    @pl.when(pl.program_id(0) >= nu_ref[0])
    def _():
        ys_ref[...] = jnp.zeros_like(ys_ref)


def _expert_call(tile_expert, n_used, xs, w1, w3, w2):
    n_rows = xs.shape[0]
    half = D_MODEL // 2
    w_spec = lambda shape: pl.BlockSpec((1,) + shape, lambda j, te, nu: (te[j], 0, 0))
    return pl.pallas_call(
        _expert_kernel,
        grid_spec=pltpu.PrefetchScalarGridSpec(
            num_scalar_prefetch=2,
            grid=(n_rows // EXPERT_ROWS,),
            in_specs=[pl.BlockSpec((EXPERT_ROWS, half), lambda j, te, nu: (j, 0)),
                      w_spec((D_MODEL, D_EXPERT)), w_spec((D_MODEL, D_EXPERT)),
                      w_spec((D_EXPERT, D_MODEL))],
            out_specs=pl.BlockSpec((EXPERT_ROWS, half), lambda j, te, nu: (j, 0))),
        out_shape=jax.ShapeDtypeStruct((n_rows, half), jnp.uint32),
        compiler_params=pltpu.CompilerParams(
            dimension_semantics=("arbitrary",), vmem_limit_bytes=VMEM_LIMIT),
        name="moe_experts",
    )(tile_expert, n_used, xs, w1, w3, w2)


def _combine_kernel(d1_ref, d2_ref, ys_hbm, x1_ref, route_ref, g2_ref, o_ref, buf_ref, sem):
    i = pl.program_id(0)
    n_steps = pl.num_programs(0)
    tc = COMBINE_TOKENS
    half = D_MODEL // 2

    def issue(step, slot):
        base = step * tc

        def one(t, carry):
            tok = base + t
            pltpu.make_async_copy(ys_hbm.at[pl.ds(d1_ref[tok], 1)],
                                  buf_ref.at[slot, pl.ds(t, 1)], sem.at[slot]).start()
            pltpu.make_async_copy(ys_hbm.at[pl.ds(d2_ref[tok], 1)],
                                  buf_ref.at[slot, pl.ds(tc + t, 1)], sem.at[slot]).start()
            return carry

        lax.fori_loop(0, tc, one, 0)

    slot = i % 2

    @pl.when(i == 0)
    def _():
        issue(0, 0)

    @pl.when(i + 1 < n_steps)
    def _():
        issue(i + 1, 1 - slot)

    pltpu.make_async_copy(ys_hbm.at[pl.ds(0, 2 * tc)], buf_ref.at[slot], sem.at[slot]).wait()

    rows = buf_ref[slot]
    a_lo, a_hi = _unpack_bf16_pair(rows[0:tc])
    b_lo, b_hi = _unpack_bf16_pair(rows[tc:2 * tc])
    route = route_ref[...]
    w_a = route[:, R_WEIGHT1:R_WEIGHT1 + 1]
    w_b = route[:, R_WEIGHT2:R_WEIGHT2 + 1]
    g2 = g2_ref[0]
    o_ref[:, 0:half] = x1_ref[:, 0:half] + g2[:, 0:half] * (w_a * a_lo + w_b * b_lo)
    o_ref[:, half:D_MODEL] = (x1_ref[:, half:D_MODEL]
                              + g2[:, half:D_MODEL] * (w_a * a_hi + w_b * b_hi))


def _combine_call(dest1, dest2, ys, x1, route, gate2, seq):
    tokens = x1.shape[0]
    tc = COMBINE_TOKENS
    per_seq = seq // tc
    half = D_MODEL // 2
    return pl.pallas_call(
        _combine_kernel,
        grid_spec=pltpu.PrefetchScalarGridSpec(
            num_scalar_prefetch=2,
            grid=(tokens // tc,),
            in_specs=[pl.BlockSpec(memory_space=pl.ANY),
                      pl.BlockSpec((tc, D_MODEL), lambda i, d1, d2: (i, 0)),
                      pl.BlockSpec((tc, ROUTER_LANES), lambda i, d1, d2: (i, 0)),
                      pl.BlockSpec((1, 1, D_MODEL), lambda i, d1, d2: (i // per_seq, 0, 0))],
            out_specs=pl.BlockSpec((tc, D_MODEL), lambda i, d1, d2: (i, 0)),
            scratch_shapes=[pltpu.VMEM((2, 2 * tc, half), jnp.uint32),
                            pltpu.SemaphoreType.DMA((2,))]),
        out_shape=jax.ShapeDtypeStruct((tokens, D_MODEL), F32),
        compiler_params=pltpu.CompilerParams(
            dimension_semantics=("arbitrary",), vmem_limit_bytes=VMEM_LIMIT),
        name="moe_combine",
    )(dest1, dest2, ys, x1, route, gate2)


def _moe(h2p, route, counts, x1, gate2, w1, w3, w2, seq):
    tokens = h2p.shape[0]
    n_rows = 2 * tokens + N_EXPERTS * EXPERT_ROWS
    n_rows -= n_rows % EXPERT_ROWS
    n_tiles = n_rows // EXPERT_ROWS

    count = counts[0, EXPERT_LANE0:EXPERT_LANE0 + N_EXPERTS].astype(jnp.int32)
    padded = (count + EXPERT_ROWS - 1) // EXPERT_ROWS * EXPERT_ROWS
    ends = jnp.cumsum(padded)
    starts = ends - padded
    expert1 = route[:, R_EXPERT1].astype(jnp.int32)
    expert2 = route[:, R_EXPERT2].astype(jnp.int32)
    dest1 = starts[expert1] + route[:, R_RANK1].astype(jnp.int32)
    dest2 = starts[expert2] + route[:, R_RANK2].astype(jnp.int32)
    tile_start = jnp.arange(n_tiles, dtype=jnp.int32) * EXPERT_ROWS
    tile_expert = jnp.minimum(jnp.searchsorted(ends, tile_start, side="right"),
                              N_EXPERTS - 1).astype(jnp.int32)
    n_used = (ends[-1:] // EXPERT_ROWS).astype(jnp.int32)

    xs = _dispatch_call(dest1, dest2, h2p, n_rows)
    ys = _expert_call(tile_expert, n_used, xs, w1, w3, w2)
    return _combine_call(dest1, dest2, ys, x1, route, gate2, seq)


def _rope_tables(seq):
    half = HEAD // 2
    inv_freq = ROPE_THETA ** (-jnp.arange(half, dtype=F32) / half)
    ang = jnp.arange(seq, dtype=F32)[:, None] * inv_freq[None, :]
    cos = jnp.cos(ang)
    sin = jnp.sin(ang)
    cos_head = jnp.concatenate([cos, cos], axis=1)
    sin_head = jnp.concatenate([-sin, sin], axis=1)
    return jnp.tile(cos_head, (1, RWKV_HEADS)), jnp.tile(sin_head, (1, RWKV_HEADS))


def _layer(x, c, w_ada, b_ada, ln1_g, ln2_g, w_in, mu_shift, w0, w_lora_up, a0, a_lora_up,
           g_lora_up, k_k, k_a, r_k, lnx_g, lnx_b, q_norm_g, k_norm_g, w_br_rwkv, w_br_moba,
           w_out, w_rg, b_rg, w_re, b_re, w1, w3, w2):
    batch, seq, _ = x.shape
    vec = lambda a: a.reshape(1, -1)

    mod = _mod_call(c, w_ada, b_ada)
    shift1, scale1, gate1, shift2, scale2, gate2 = (
        m.reshape(batch, 1, D_MODEL) for m in jnp.split(mod, 6, axis=-1))

    idx = np.arange(WIDTH)
    ones_bd = jnp.asarray(idx[:, None] // HEAD == idx[None, :] // HEAD, BF16)
    t_idx = np.arange(RWKV_ROWS)
    tri = jnp.asarray((t_idx[:, None] >= t_idx[None, :])
                      & (t_idx[:, None] // CHUNK == t_idx[None, :] // CHUNK), F32)
    cos, sin_signed = _rope_tables(seq)

    z_rwkv, q, k, v, gates, kmean = _inproj_call(
        x, shift1, scale1, vec(ln1_g), w_in.astype(BF16),
        vec(jnp.tile(q_norm_g, RWKV_HEADS)), vec(jnp.tile(k_norm_g, RWKV_HEADS)),
        cos, sin_signed, ones_bd)

    lora_cat = jnp.zeros((DECAY_LORA + AAA_LORA, 2 * WIDTH), F32)
    lora_cat = lora_cat.at[:DECAY_LORA, :WIDTH].set(w_lora_up).at[DECAY_LORA:, WIDTH:].set(a_lora_up)
    y_a = _rwkv_call(z_rwkv, vec(mu_shift), vec(w0), vec(a0), lora_cat, g_lora_up, vec(k_k),
                     vec(k_a), vec(r_k), vec(lnx_g), vec(lnx_b), ones_bd, tri)

    y_b = _moba_call(q, k, v, kmean.reshape(batch, seq // MOBA_BLOCK, WIDTH))

    tokens = batch * seq
    w_router = jnp.zeros((D_MODEL, ROUTER_LANES), F32)
    w_router = w_router.at[:, :N_GROUPS].set(w_rg).at[:, EXPERT_LANE0:EXPERT_LANE0 + N_EXPERTS].set(w_re)
    b_router = jnp.zeros((1, ROUTER_LANES), F32)
    b_router = b_router.at[0, :N_GROUPS].set(b_rg).at[0, EXPERT_LANE0:EXPERT_LANE0 + N_EXPERTS].set(b_re)
    m_idx = np.arange(MERGE_ROWS)
    tri_strict = jnp.asarray(m_idx[:, None] > m_idx[None, :], BF16)
    x1, h2p, route, counts = _merge_call(
        x.reshape(tokens, D_MODEL), y_a.reshape(tokens, WIDTH), y_b.reshape(tokens, WIDTH),
        gates.reshape(tokens, GATE_COLS), w_br_rwkv.astype(BF16), w_br_moba.astype(BF16),
        w_out.astype(BF16), gate1, vec(ln2_g), scale2, shift2, w_router, b_router, tri_strict,
        seq)

    out = _moe(h2p, route, counts, x1, gate2, w1.astype(BF16), w3.astype(BF16), w2.astype(BF16),
               seq)
    return out.reshape(batch, seq, D_MODEL)


def kernel(x, c, w_ada, b_ada, ln1_g, ln2_g, w_in, mu_shift, w0, w_lora_up, a0, a_lora_up,
           g_lora_up, k_k, k_a, r_k, lnx_g, lnx_b, q_norm_g, k_norm_g, w_br_rwkv, w_br_moba,
           w_out, w_rg, b_rg, w_re, b_re, w1, w3, w2):
    assert w_ada.shape[0] == 1, "single-layer problem"
    layer_params = (w_ada, b_ada, ln1_g, ln2_g, w_in, mu_shift, w0, w_lora_up, a0, a_lora_up,
                    g_lora_up, k_k, k_a, r_k, lnx_g, lnx_b, q_norm_g, k_norm_g, w_br_rwkv,
                    w_br_moba, w_out, w_rg, b_rg, w_re, b_re, w1, w3, w2)
    return _layer(x, c, *(p[0] for p in layer_params))
```

```python
import functools

import jax
import jax.numpy as jnp
import numpy as np
from jax import lax
from jax.experimental import pallas as pl
from jax.experimental.pallas import tpu as pltpu

F32 = jnp.float32
BF16 = jnp.bfloat16
HIGHEST = lax.Precision.HIGHEST

D_MODEL = 1024
RWKV_HEADS = 8
HEAD = 64
WIDTH = RWKV_HEADS * HEAD
DECAY_LORA = 64
AAA_LORA = 64
GATE_LORA = 128
RWKV_COLS = 3 * WIDTH + DECAY_LORA + AAA_LORA + GATE_LORA
ATT_COLS = 3 * WIDTH
GATE_COLS = 2 * D_MODEL
IN_COLS = RWKV_COLS + ATT_COLS + GATE_COLS
DECAY_SCALE = 0.606531
LN_X_EPS = 64e-5
MOBA_BLOCK = 256
MOBA_TOPK = 3
ROPE_THETA = 10000.0
N_GROUPS = 4
EXPERTS_PER_GROUP = 8
N_EXPERTS = N_GROUPS * EXPERTS_PER_GROUP
D_EXPERT = D_MODEL // 2
NORM_EPS = 1e-6
NEG_INF = -1e30

LANES = 128
PAIRS = WIDTH // LANES
CHUNK = 64
SOLVE_BLOCK = 16
RWKV_ROWS = 2 * CHUNK
MOBA_SUB = 128
VMEM_LIMIT = 56 * 1024 * 1024

ROUTER_LANES = LANES
EXPERT_LANE0 = N_GROUPS


def _dot(a, b):
    return jnp.dot(a.astype(BF16), b.astype(BF16), preferred_element_type=F32)


def _dot_nt(a, b):
    return lax.dot_general(a.astype(BF16), b.astype(BF16), (((1,), (1,)), ((), ())),
                           preferred_element_type=F32)


def _dot_f32(a, b):
    return jnp.dot(a, b, precision=HIGHEST, preferred_element_type=F32)


def _split_bf16(a):
    hi = a.astype(BF16)
    return hi, (a - hi.astype(F32)).astype(BF16)


def _dot_x3(a, b):
    a_hi, a_lo = _split_bf16(a)
    b_hi, b_lo = _split_bf16(b)
    return (jnp.dot(a_hi, b_hi, preferred_element_type=F32)
            + jnp.dot(a_hi, b_lo, preferred_element_type=F32)
            + jnp.dot(a_lo, b_hi, preferred_element_type=F32))


def _seg_sum(x, ones_bd):
    hi = x.astype(BF16)
    lo = (x - hi.astype(F32)).astype(BF16)
    return (jnp.dot(hi, ones_bd, preferred_element_type=F32)
            + jnp.dot(lo, ones_bd, preferred_element_type=F32))


def _iota(shape, axis):
    return lax.broadcasted_iota(jnp.int32, shape, axis)


def _mod_kernel(c_ref, w_ref, b_ref, o_ref):
    c = c_ref[...]
    o_ref[...] = _dot_f32(c * jax.nn.sigmoid(c), w_ref[...]) + b_ref[...]


def _mod_call(c, w_ada, b_ada):
    batch = c.shape[0]
    n_out = w_ada.shape[1]
    tn = D_MODEL
    return pl.pallas_call(
        _mod_kernel,
        grid=(n_out // tn,),
        in_specs=[pl.BlockSpec((batch, D_MODEL), lambda j: (0, 0)),
                  pl.BlockSpec((D_MODEL, tn), lambda j: (0, j)),
                  pl.BlockSpec((1, tn), lambda j: (0, j))],
        out_specs=pl.BlockSpec((batch, tn), lambda j: (0, j)),
        out_shape=jax.ShapeDtypeStruct((batch, n_out), F32),
        name="adaln_mod",
    )(c, w_ada, b_ada.reshape(1, n_out))


def _swap_halves(x):
    first = (_iota(x.shape, 1) & (HEAD - 1)) < HEAD // 2
    up = pltpu.roll(x, LANES - HEAD // 2, axis=1)
    down = pltpu.roll(x, HEAD // 2, axis=1)
    return jnp.where(first, up, down)


def _head_norm_rope(x, gain, cos, sin_signed, ones_bd):
    ms = _seg_sum(x * x, ones_bd) * (1.0 / HEAD)
    y = x * lax.rsqrt(ms + NORM_EPS) * gain
    cols = []
    for p in range(PAIRS):
        sl = slice(p * LANES, (p + 1) * LANES)
        yb = y[:, sl]
        cols.append(yb * cos[:, sl] + _swap_halves(yb) * sin_signed[:, sl])
    return jnp.concatenate(cols, axis=1)


def _inproj_kernel(x_ref, shift_ref, scale_ref, g_ref, w_ref, qg_ref, kg_ref, cos_ref, sin_ref,
                   ones_ref, zr_ref, q_ref, k_ref, v_ref, gate_ref, kmean_ref):
    x = x_ref[0]
    ms = jnp.mean(x * x, axis=-1, keepdims=True)
    h = x * lax.rsqrt(ms + NORM_EPS) * g_ref[...]
    h = (h * (1.0 + scale_ref[0]) + shift_ref[0]).astype(BF16)

    zr_ref[0] = jnp.dot(h, w_ref[:, 0:RWKV_COLS], preferred_element_type=F32)

    za = jnp.dot(h, w_ref[:, RWKV_COLS:RWKV_COLS + ATT_COLS], preferred_element_type=F32)
    ones_bd = ones_ref[...]
    cos = cos_ref[...]
    sin = sin_ref[...]
    q = _head_norm_rope(za[:, 0:WIDTH], qg_ref[...], cos, sin, ones_bd)
    k = _head_norm_rope(za[:, WIDTH:2 * WIDTH], kg_ref[...], cos, sin, ones_bd)
    q_ref[0] = (q * (HEAD ** -0.5)).astype(BF16)
    k_ref[0] = k.astype(BF16)
    v_ref[0] = za[:, 2 * WIDTH:3 * WIDTH].astype(BF16)
    kmean_ref[0] = jnp.mean(k, axis=0, keepdims=True)

    zg = jnp.dot(h, w_ref[:, RWKV_COLS + ATT_COLS:IN_COLS], preferred_element_type=F32)
    gate_ref[0] = jax.nn.sigmoid(zg).astype(BF16)


def _inproj_call(x, shift1, scale1, ln1_g, w_in_bf16, q_gain, k_gain, cos, sin_signed, ones_bd):
    batch, seq, _ = x.shape
    tm = MOBA_BLOCK
    n_t = seq // tm
    row = lambda width: pl.BlockSpec((1, tm, width), lambda b, i: (b, i, 0))
    per_batch = pl.BlockSpec((1, 1, D_MODEL), lambda b, i: (b, 0, 0))
    const = lambda shape: pl.BlockSpec(shape, lambda b, i: (0,) * len(shape))
    return pl.pallas_call(
        _inproj_kernel,
        grid=(batch, n_t),
        in_specs=[row(D_MODEL), per_batch, per_batch, const((1, D_MODEL)),
                  const((D_MODEL, IN_COLS)), const((1, WIDTH)), const((1, WIDTH)),
                  pl.BlockSpec((tm, WIDTH), lambda b, i: (i, 0)),
                  pl.BlockSpec((tm, WIDTH), lambda b, i: (i, 0)),
                  const((WIDTH, WIDTH))],
        out_specs=[row(RWKV_COLS), row(WIDTH), row(WIDTH), row(WIDTH), row(GATE_COLS),
                   pl.BlockSpec((1, 1, WIDTH), lambda b, i: (b * n_t + i, 0, 0))],
        out_shape=[jax.ShapeDtypeStruct((batch, seq, RWKV_COLS), F32),
                   jax.ShapeDtypeStruct((batch, seq, WIDTH), BF16),
                   jax.ShapeDtypeStruct((batch, seq, WIDTH), BF16),
                   jax.ShapeDtypeStruct((batch, seq, WIDTH), BF16),
                   jax.ShapeDtypeStruct((batch, seq, GATE_COLS), BF16),
                   jax.ShapeDtypeStruct((batch * n_t, 1, WIDTH), F32)],
        compiler_params=pltpu.CompilerParams(
            dimension_semantics=("arbitrary", "arbitrary"), vmem_limit_bytes=VMEM_LIMIT),
        name="inproj",
    )(x, shift1, scale1, ln1_g, w_in_bf16, q_gain, k_gain, cos, sin_signed, ones_bd)


def _stack_heads(x):
    first = _iota(x.shape, 1) < HEAD
    return jnp.concatenate([jnp.where(first, x, 0.0), jnp.where(first, 0.0, x)], axis=0)


def _add_eye(x, eye):
    return jnp.where(eye, x + 1.0, x)


def _rwkv_kernel(z_ref, mu_ref, w0_ref, a0_ref, lora_ref, glora_ref, kk_ref, ka_ref, rk_ref,
                 lng_ref, lnb_ref, ones_ref, tri_ref, o_ref, prev_ref, state_ref):
    c = pl.program_id(1)

    @pl.when(c == 0)
    def _():
        prev_ref[...] = jnp.zeros_like(prev_ref)
        state_ref[...] = jnp.zeros_like(state_ref)

    z = z_ref[0]
    rows = z.shape[0]
    n_chunks = rows // CHUNK
    row = _iota(z.shape, 0)
    z_prev = jnp.where(row == 0, prev_ref[...], pltpu.roll(z, 1, axis=0))
    prev_ref[...] = z[rows - 1:rows, :]
    zs = z + (z_prev - z) * mu_ref[...]

    r = zs[:, 0:WIDTH]
    k = zs[:, WIDTH:2 * WIDTH]
    v = zs[:, 2 * WIDTH:3 * WIDTH]
    lo = 3 * WIDTH
    d_wa = zs[:, lo:lo + DECAY_LORA + AAA_LORA]
    d_g = zs[:, lo + DECAY_LORA + AAA_LORA:RWKV_COLS]
    is_decay = _iota(d_wa.shape, 1) < DECAY_LORA
    pre = _dot_x3(jnp.where(is_decay, jnp.tanh(d_wa), d_wa), lora_ref[...])
    log_w = -DECAY_SCALE * jax.nn.sigmoid(w0_ref[...] + pre[:, 0:WIDTH])
    a = jax.nn.sigmoid(a0_ref[...] + pre[:, WIDTH:2 * WIDTH])
    g = _dot_x3(jax.nn.sigmoid(d_g), glora_ref[...])

    ones_bd = ones_ref[...]
    kk = k * kk_ref[...]
    kk = kk / jnp.maximum(jnp.sqrt(_seg_sum(kk * kk, ones_bd)), 1e-12)
    k = k * (1.0 + (a - 1.0) * ka_ref[...])
    bonus = _seg_sum(r * k * rk_ref[...], ones_bd) * v

    cl = _dot_f32(tri_ref[...], log_w)
    p_end_rows = [cl[(c_i + 1) * CHUNK - 1:(c_i + 1) * CHUNK, :] for c_i in range(n_chunks)]
    cl_last = p_end_rows[0]
    chunk_of_row = _iota(cl.shape, 0) // CHUNK
    for c_i in range(1, n_chunks):
        cl_last = jnp.where(chunk_of_row == c_i, p_end_rows[c_i], cl_last)
    a_t = -kk * jnp.exp(cl - log_w)
    e_neg = jnp.exp(-cl)
    b_t = kk * a * e_neg
    k_t = k * e_neg
    r_t = r * jnp.exp(cl)
    e_end = jnp.exp(cl_last - cl)
    b_end = kk * a * e_end
    k_end = k * e_end
    p_end = jnp.exp(cl_last)

    n2 = 2 * CHUNK
    ri = _iota((n2, n2), 0)
    ci = _iota((n2, n2), 1)
    eye = ri == ci
    same_blk = (ri // SOLVE_BLOCK) == (ci // SOLVE_BLOCK)
    ri4 = _iota((2 * n2, 2 * n2), 0)
    ci4 = _iota((2 * n2, 2 * n2), 1) & (n2 - 1)
    causal4 = ((ri4 < n2) & (ri4 > ci4)) | ((ri4 >= n2) & ((ri4 - n2) >= ci4))
    zeros = jnp.zeros((n2, n2), F32)

    items = [(c_i, p) for c_i in range(n_chunks) for p in range(PAIRS)]

    def tile(t, item):
        c_i, p = item
        return t[c_i * CHUNK:(c_i + 1) * CHUNK, p * LANES:(p + 1) * LANES]

    sa = [_stack_heads(tile(a_t, it)) for it in items]
    sr = [_stack_heads(tile(r_t, it)) for it in items]
    sv = [_stack_heads(tile(v, it)) for it in items]
    big = [jnp.where(causal4,
                     _dot_nt(jnp.concatenate([sa[i], sr[i]], axis=0),
                             jnp.concatenate([_stack_heads(tile(b_t, it)),
                                              _stack_heads(tile(k_t, it))], axis=0)), 0.0)
           for i, it in enumerate(items)]
    a_ab = [b[0:n2, 0:n2] for b in big]
    a_ak = [b[0:n2, n2:2 * n2] for b in big]
    a_rbk = [b[n2:2 * n2, :] for b in big]

    d1 = [jnp.where(same_blk, a, 0.0) for a in a_ab]
    e1 = [a - d for a, d in zip(a_ab, d1)]
    d2 = [_dot(d, d) for d in d1]
    d4 = [_dot(d, d) for d in d2]
    d8 = [_dot(d, d) for d in d4]
    p12 = [_dot(_add_eye(x, eye), _add_eye(y, eye)) for x, y in zip(d1, d2)]
    p48 = [_dot(_add_eye(x, eye), _add_eye(y, eye)) for x, y in zip(d4, d8)]
    t_d = [_dot(x, y) for x, y in zip(p12, p48)]
    g1 = [_dot(t, e) for t, e in zip(t_d, e1)]
    g2 = [_dot(g, g) for g in g1]
    gx = [_dot(_add_eye(x, eye), _add_eye(y, eye)) for x, y in zip(g1, g2)]
    t_inv = [_dot(x, t) for x, t in zip(gx, t_d)]

    akv = [_dot(a, s) for a, s in zip(a_ak, sv)]
    wu = [_dot(t, jnp.concatenate([s, x], axis=1)) for t, s, x in zip(t_inv, sa, akv)]
    rhs = [jnp.concatenate([w, jnp.concatenate([zeros, s], axis=1)], axis=0)
           for w, s in zip(wu, sv)]
    out_c = [_dot(a, x) for a, x in zip(a_rbk, rhs)]
    end_t = [jnp.concatenate([_stack_heads(tile(b_end, it)).T,
                              _stack_heads(tile(k_end, it)).T], axis=1) for it in items]
    end_c = [_dot(e, x) for e, x in zip(end_t, rhs)]

    y_rows = []
    for c_i in range(n_chunks):
        y_cols = []
        for p in range(PAIRS):
            i = c_i * PAIRS + p
            psi = sr[i] + out_c[i][:, 0:n2]
            pe = p_end[c_i * CHUNK:c_i * CHUNK + 1, p * LANES:(p + 1) * LANES]
            phi = jnp.where(eye, pe, 0.0) + end_c[i][:, 0:n2]
            state = state_ref[p]
            both = _dot_x3(jnp.concatenate([psi, phi], axis=0), state)
            y = both[0:n2, :] + out_c[i][:, n2:2 * n2]
            state_ref[p] = both[n2:2 * n2, :] + end_c[i][:, n2:2 * n2]
            y_cols.append(y[0:CHUNK, :] + y[CHUNK:n2, :])
        y_rows.append(jnp.concatenate(y_cols, axis=1))
    y = jnp.concatenate(y_rows, axis=0)
    mean = _seg_sum(y, ones_bd) * (1.0 / HEAD)
    yc = y - mean
    var = _seg_sum(yc * yc, ones_bd) * (1.0 / HEAD)
    yn = yc * lax.rsqrt(var + LN_X_EPS) * lng_ref[...] + lnb_ref[...]
    o_ref[0] = ((yn + bonus) * g).astype(o_ref.dtype)


def _rwkv_call(z_rwkv, mu_shift, w0, a0, lora_cat, g_lora_up, k_k, k_a, r_k, lnx_g, lnx_b,
               ones_bd, tri):
    batch, seq, _ = z_rwkv.shape
    rows = RWKV_ROWS
    const = lambda shape: pl.BlockSpec(shape, lambda b, c: (0,) * len(shape))
    vec = const((1, WIDTH))
    return pl.pallas_call(
        _rwkv_kernel,
        grid=(batch, seq // rows),
        in_specs=[pl.BlockSpec((1, rows, RWKV_COLS), lambda b, c: (b, c, 0)),
                  const((1, RWKV_COLS)), vec, vec,
                  const((DECAY_LORA + AAA_LORA, 2 * WIDTH)), const((GATE_LORA, WIDTH)),
                  vec, vec, vec, vec, vec, const((WIDTH, WIDTH)), const((rows, rows))],
        out_specs=pl.BlockSpec((1, rows, WIDTH), lambda b, c: (b, c, 0)),
        out_shape=jax.ShapeDtypeStruct((batch, seq, WIDTH), BF16),
        scratch_shapes=[pltpu.VMEM((1, RWKV_COLS), F32),
                        pltpu.VMEM((PAIRS, LANES, LANES), F32)],
        compiler_params=pltpu.CompilerParams(
            dimension_semantics=("arbitrary", "arbitrary"), vmem_limit_bytes=VMEM_LIMIT),
        name="rwkv7",
    )(z_rwkv, mu_shift, w0, a0, lora_cat, g_lora_up, k_k, k_a, r_k, lnx_g, lnx_b, ones_bd, tri)


def _moba_kernel(q_ref, k_ref, v_ref, km_ref, o_ref, bias_ref, m_ref, l_ref, acc_ref):
    qi = pl.program_id(2)
    tq = q_ref.shape[1]
    n_blk = km_ref.shape[1]
    q = q_ref[0]
    first = _iota((tq, LANES), 1) < HEAD
    zero = jnp.zeros_like(q)
    qh = [jnp.where(first, q, zero), jnp.where(first, zero, q)]

    km = km_ref[0]
    blk = _iota((n_blk, tq), 0)
    valid = blk < qi
    for h in range(2):
        gate = lax.dot_general(km, qh[h].astype(F32), (((1,), (1,)), ((), ())),
                               precision=HIGHEST, preferred_element_type=F32)
        gate = jnp.where(valid, gate, NEG_INF)
        bias = jnp.full((n_blk, tq), NEG_INF, F32)
        for n in range(n_blk):
            g_n = gate[n:n + 1, :]
            ahead = valid & ((gate > g_n) | ((gate == g_n) & (blk < n)))
            rank = jnp.sum(ahead.astype(F32), axis=0, keepdims=True)
            chosen = (rank < MOBA_TOPK) & (qi > n)
            bias = jnp.where((blk == n) & chosen, 0.0, bias)
        bias_ref[h] = bias

    m_ref[...] = jnp.full(m_ref.shape, NEG_INF, F32)
    l_ref[...] = jnp.zeros(l_ref.shape, F32)
    acc_ref[...] = jnp.zeros(acc_ref.shape, F32)
    eye = (_iota((tq, tq), 0) == _iota((tq, tq), 1)).astype(BF16)
    tiles = [(h, r0) for h in range(2) for r0 in range(0, tq, MOBA_SUB)]
    q_tiles = [qh[h][r0:r0 + MOBA_SUB] for h, r0 in tiles]

    def update(scores, v_blk):
        n_keys = v_blk.shape[0]
        v_ext = jnp.concatenate([v_blk, jnp.ones((n_keys, LANES), BF16)], axis=1)
        refs = [(h, slice(r0, r0 + MOBA_SUB)) for h, r0 in tiles]
        m_old = [m_ref[h, rows, :] for h, rows in refs]
        l_old = [l_ref[h, rows, :] for h, rows in refs]
        acc_old = [acc_ref[h, rows, :] for h, rows in refs]
        m_new = [jnp.maximum(m, jnp.broadcast_to(jnp.max(s, axis=-1, keepdims=True),
                                                 (MOBA_SUB, LANES)))
                 for m, s in zip(m_old, scores)]
        alpha = [jnp.exp(m - n) for m, n in zip(m_old, m_new)]
        p = [jnp.exp(s - jnp.concatenate([n] * (n_keys // LANES), axis=1)).astype(BF16)
             for s, n in zip(scores, m_new)]
        pv = [jnp.dot(x, v_ext, preferred_element_type=F32) for x in p]
        for (h, rows), m, a, l, acc, y in zip(refs, m_new, alpha, l_old, acc_old, pv):
            m_ref[h, rows, :] = m
            l_ref[h, rows, :] = a * l + y[:, LANES:2 * LANES]
            acc_ref[h, rows, :] = a * acc + y[:, 0:LANES]

    def past_blocks(j, carry):
        start = pl.multiple_of(j * (2 * MOBA_BLOCK), 2 * MOBA_BLOCK)
        k_blk = k_ref[0, pl.ds(start, 2 * MOBA_BLOCK), :]
        v_blk = v_ref[0, pl.ds(start, 2 * MOBA_BLOCK), :]
        bias_rows = [jnp.concatenate(
            [jnp.broadcast_to(bias_ref[h, pl.ds(2 * j + i, 1), :], (MOBA_BLOCK, tq))
             for i in range(2)], axis=0).astype(BF16) for h in range(2)]
        scores = [_dot_nt(qt, k_blk) + _dot_nt(eye[r0:r0 + MOBA_SUB], bias_rows[h])
                  for qt, (h, r0) in zip(q_tiles, tiles)]
        update(scores, v_blk)
        return carry

    lax.fori_loop(0, (qi + 1) // 2, past_blocks, 0)

    start = pl.multiple_of(qi * MOBA_BLOCK, MOBA_BLOCK)
    k_blk = k_ref[0, pl.ds(start, MOBA_BLOCK), :]
    v_blk = v_ref[0, pl.ds(start, MOBA_BLOCK), :]
    row = _iota((MOBA_SUB, MOBA_BLOCK), 0)
    col = _iota((MOBA_SUB, MOBA_BLOCK), 1)
    update([jnp.where(col <= row + r0, _dot_nt(qt, k_blk), NEG_INF)
            for qt, (h, r0) in zip(q_tiles, tiles)], v_blk)

    o_ref[0] = jnp.where(first, acc_ref[0] / l_ref[0], acc_ref[1] / l_ref[1]).astype(o_ref.dtype)


def _moba_call(q, k, v, kmean):
    batch, seq, _ = q.shape
    tq = MOBA_BLOCK
    n_blk = seq // MOBA_BLOCK
    return pl.pallas_call(
        _moba_kernel,
        grid=(batch, PAIRS, seq // tq),
        in_specs=[pl.BlockSpec((1, tq, LANES), lambda b, p, i: (b, i, p)),
                  pl.BlockSpec((1, seq, LANES), lambda b, p, i: (b, 0, p)),
                  pl.BlockSpec((1, seq, LANES), lambda b, p, i: (b, 0, p)),
                  pl.BlockSpec((1, n_blk, LANES), lambda b, p, i: (b, 0, p))],
        out_specs=pl.BlockSpec((1, tq, LANES), lambda b, p, i: (b, i, p)),
        out_shape=jax.ShapeDtypeStruct((batch, seq, WIDTH), BF16),
        scratch_shapes=[pltpu.VMEM((2, n_blk, tq), F32),
                        pltpu.VMEM((2, tq, LANES), F32),
                        pltpu.VMEM((2, tq, LANES), F32),
                        pltpu.VMEM((2, tq, LANES), F32)],
        compiler_params=pltpu.CompilerParams(
            dimension_semantics=("arbitrary", "arbitrary", "arbitrary"),
            vmem_limit_bytes=VMEM_LIMIT),
        name="moba",
    )(q, k, v, kmean)


def _pack_bf16_pair(lo, hi):
    lo_bits = lax.bitcast_convert_type(lo.astype(BF16).astype(F32), jnp.uint32)
    hi_bits = lax.bitcast_convert_type(hi.astype(BF16).astype(F32), jnp.uint32)
    return (lo_bits >> 16) | (hi_bits & jnp.uint32(0xFFFF0000))


def _unpack_bf16_pair(u):
    lo = lax.bitcast_convert_type(u << 16, F32)
    hi = lax.bitcast_convert_type(u & jnp.uint32(0xFFFF0000), F32)
    return lo, hi


def _merge_kernel(x_ref, ya_ref, yb_ref, gate_ref, wa_ref, wb_ref, wo_ref, g1_ref, ln_ref,
                  scale_ref, shift_ref, wr_ref, br_ref, tri_ref, x1_ref, h2_ref, route_ref,
                  count_ref, carry_ref):
    @pl.when(pl.program_id(0) == 0)
    def _():
        carry_ref[...] = jnp.zeros_like(carry_ref)

    ya = jnp.dot(ya_ref[...], wa_ref[...], preferred_element_type=F32)
    yb = jnp.dot(yb_ref[...], wb_ref[...], preferred_element_type=F32)
    gates = gate_ref[...]
    merged = (gates[:, 0:D_MODEL].astype(F32) * ya + gates[:, D_MODEL:GATE_COLS].astype(F32) * yb)
    x1 = x_ref[...] + g1_ref[0] * jnp.dot(merged.astype(BF16), wo_ref[...],
                                          preferred_element_type=F32)
    x1_ref[...] = x1

    ms = jnp.mean(x1 * x1, axis=-1, keepdims=True)
    h2 = x1 * lax.rsqrt(ms + NORM_EPS) * ln_ref[...]
    h2 = h2 * (1.0 + scale_ref[0]) + shift_ref[0]
    half = D_MODEL // 2
    h2_ref[...] = _pack_bf16_pair(h2[:, 0:half], h2[:, half:D_MODEL])

    logits = _dot_x3(h2, wr_ref[...]) + br_ref[...]
    lane = _iota(logits.shape, 1)
    lane_f = lane.astype(F32)
    far = float(ROUTER_LANES)

    def top(vals):
        m = jnp.max(vals, axis=-1, keepdims=True)
        idx = jnp.min(jnp.where(vals == m, lane_f, far), axis=-1, keepdims=True)
        return m, idx

    grp = jnp.where(lane < N_GROUPS, logits, NEG_INF)
    g_max, g_idx = top(grp)
    p_group = 1.0 / jnp.sum(jnp.exp(grp - g_max), axis=-1, keepdims=True)

    e_lo = EXPERT_LANE0 + EXPERTS_PER_GROUP * g_idx
    in_grp = (lane_f >= e_lo) & (lane_f < e_lo + EXPERTS_PER_GROUP)
    el = jnp.where(in_grp, logits, NEG_INF)
    m1, i1 = top(el)
    m2, i2 = top(jnp.where(lane_f == i1, NEG_INF, el))
    ratio = jnp.exp(m2 - m1)
    w_first = p_group / (1.0 + ratio)
    w_second = w_first * ratio

    first = lane_f == i1
    second = lane_f == i2
    hits = (first | second).astype(BF16)
    before = carry_ref[...] + jnp.dot(tri_ref[...], hits, preferred_element_type=F32)
    rank1 = jnp.sum(jnp.where(first, before, 0.0), axis=-1, keepdims=True)
    rank2 = jnp.sum(jnp.where(second, before, 0.0), axis=-1, keepdims=True)
    carry = carry_ref[...] + jnp.sum(hits.astype(F32), axis=0, keepdims=True)
    carry_ref[...] = carry
    count_ref[...] = carry

    fields = (i1 - EXPERT_LANE0, i2 - EXPERT_LANE0, w_first, w_second, rank1, rank2)
    route = jnp.zeros(logits.shape, F32)
    for n, field in enumerate(fields):
        route = jnp.where(lane == n, field, route)
    route_ref[...] = route


R_EXPERT1, R_EXPERT2, R_WEIGHT1, R_WEIGHT2, R_RANK1, R_RANK2 = range(6)
MERGE_ROWS = 256


def _merge_call(x2d, ya, yb, gates, w_br_rwkv, w_br_moba, w_out, gate1, ln2_g, scale2, shift2,
                w_router, b_router, tri, seq):
    tokens = x2d.shape[0]
    tm = MERGE_ROWS
    per_seq = seq // tm
    row = lambda width: pl.BlockSpec((tm, width), lambda i: (i, 0))
    per_batch = pl.BlockSpec((1, 1, D_MODEL), lambda i: (i // per_seq, 0, 0))
    const = lambda shape: pl.BlockSpec(shape, lambda i: (0,) * len(shape))
    return pl.pallas_call(
        _merge_kernel,
        grid=(tokens // tm,),
        in_specs=[row(D_MODEL), row(WIDTH), row(WIDTH), row(GATE_COLS),
                  const((WIDTH, D_MODEL)), const((WIDTH, D_MODEL)), const((D_MODEL, D_MODEL)),
                  per_batch, const((1, D_MODEL)), per_batch, per_batch,
                  const((D_MODEL, ROUTER_LANES)), const((1, ROUTER_LANES)), const((tm, tm))],
        out_specs=[row(D_MODEL), row(D_MODEL // 2), row(ROUTER_LANES),
                   const((1, ROUTER_LANES))],
        out_shape=[jax.ShapeDtypeStruct((tokens, D_MODEL), F32),
                   jax.ShapeDtypeStruct((tokens, D_MODEL // 2), jnp.uint32),
                   jax.ShapeDtypeStruct((tokens, ROUTER_LANES), F32),
                   jax.ShapeDtypeStruct((1, ROUTER_LANES), F32)],
        scratch_shapes=[pltpu.VMEM((1, ROUTER_LANES), F32)],
        compiler_params=pltpu.CompilerParams(
            dimension_semantics=("arbitrary",), vmem_limit_bytes=VMEM_LIMIT),
        name="merge",
    )(x2d, ya, yb, gates, w_br_rwkv, w_br_moba, w_out, gate1, ln2_g, scale2, shift2,
      w_router, b_router, tri)


EXPERT_ROWS = 256
DISPATCH_TOKENS = 1024
COMBINE_TOKENS = 256


def _dest_kernel(route_ref, start_ref, o_ref):
    route = route_ref[...]
    lane = _iota(route.shape, 1)
    lane_f = lane.astype(F32)
    starts = start_ref[...]
    out = jnp.zeros(route.shape, F32)
    for n, (e_lane, r_lane) in enumerate(((R_EXPERT1, R_RANK1), (R_EXPERT2, R_RANK2))):
        e_col = route[:, e_lane:e_lane + 1] + EXPERT_LANE0
        first_row = jnp.sum(jnp.where(lane_f == e_col, starts, 0.0), axis=-1, keepdims=True)
        out = jnp.where(lane == n, first_row + route[:, r_lane:r_lane + 1], out)
    o_ref[...] = out


def _dest_call(route, start_row):
    tokens = route.shape[0]
    tm = DISPATCH_TOKENS
    return pl.pallas_call(
        _dest_kernel,
        grid=(tokens // tm,),
        in_specs=[pl.BlockSpec((tm, ROUTER_LANES), lambda i: (i, 0)),
                  pl.BlockSpec((1, ROUTER_LANES), lambda i: (0, 0))],
        out_specs=pl.BlockSpec((tm, ROUTER_LANES), lambda i: (i, 0)),
        out_shape=jax.ShapeDtypeStruct((tokens, ROUTER_LANES), F32),
        name="moe_dest",
    )(route, start_row)


def _dispatch_kernel(d1_ref, d2_ref, h_ref, xs_init_hbm, xs_hbm, sem):
    del xs_init_hbm
    base = pl.program_id(0) * DISPATCH_TOKENS

    def issue(t, carry):
        src = h_ref.at[pl.ds(t, 1)]
        pltpu.make_async_copy(src, xs_hbm.at[pl.ds(d1_ref[base + t], 1)], sem).start()
        pltpu.make_async_copy(src, xs_hbm.at[pl.ds(d2_ref[base + t], 1)], sem).start()
        return carry

    lax.fori_loop(0, DISPATCH_TOKENS, issue, 0, unroll=4)
    for _ in range(2):
        pltpu.make_async_copy(h_ref, xs_hbm.at[pl.ds(0, DISPATCH_TOKENS)], sem).wait()


def _dispatch_call(dest1, dest2, h2p, n_rows):
    tokens = h2p.shape[0]
    any_spec = pl.BlockSpec(memory_space=pl.ANY)
    return pl.pallas_call(
        _dispatch_kernel,
        grid_spec=pltpu.PrefetchScalarGridSpec(
            num_scalar_prefetch=2,
            grid=(tokens // DISPATCH_TOKENS,),
            in_specs=[pl.BlockSpec((DISPATCH_TOKENS, D_MODEL // 2), lambda i, d1, d2: (i, 0)),
                      any_spec],
            out_specs=any_spec,
            scratch_shapes=[pltpu.SemaphoreType.DMA(())]),
        out_shape=jax.ShapeDtypeStruct((n_rows, D_MODEL // 2), jnp.uint32),
        input_output_aliases={3: 0},
        compiler_params=pltpu.CompilerParams(dimension_semantics=("arbitrary",)),
        name="moe_dispatch",
    )(dest1, dest2, h2p, jnp.zeros((n_rows, D_MODEL // 2), jnp.uint32))


def _expert_kernel(te_ref, nu_ref, xs_ref, w1_ref, w3_ref, w2_ref, ys_ref):
    half = D_MODEL // 2

    @pl.when(pl.program_id(0) < nu_ref[0])
    def _():
        x_lo, x_hi = _unpack_bf16_pair(xs_ref[...])
        x_lo = x_lo.astype(BF16)
        x_hi = x_hi.astype(BF16)

        def proj(w_ref):
            return (jnp.dot(x_lo, w_ref[0, 0:half, :], preferred_element_type=F32)
                    + jnp.dot(x_hi, w_ref[0, half:D_MODEL, :], preferred_element_type=F32))

        a = proj(w1_ref)
        hid = (a * jax.nn.sigmoid(a) * proj(w3_ref)).astype(BF16)
        y = jnp.dot(hid, w2_ref[0], preferred_element_type=F32)
        ys_ref[...] = _pack_bf16_pair(y[:, 0:half], y[:, half:D_MODEL])

    @pl.when(pl.program_id(0) >= nu_ref[0])
    def _():
        ys_ref[...] = jnp.zeros_like(ys_ref)


def _expert_call(tile_expert, n_used, xs, w1, w3, w2):
    n_rows = xs.shape[0]
    half = D_MODEL // 2
    w_spec = lambda shape: pl.BlockSpec((1,) + shape, lambda j, te, nu: (te[j], 0, 0))
    return pl.pallas_call(
        _expert_kernel,
        grid_spec=pltpu.PrefetchScalarGridSpec(
            num_scalar_prefetch=2,
            grid=(n_rows // EXPERT_ROWS,),
            in_specs=[pl.BlockSpec((EXPERT_ROWS, half), lambda j, te, nu: (j, 0)),
                      w_spec((D_MODEL, D_EXPERT)), w_spec((D_MODEL, D_EXPERT)),
                      w_spec((D_EXPERT, D_MODEL))],
            out_specs=pl.BlockSpec((EXPERT_ROWS, half), lambda j, te, nu: (j, 0))),
        out_shape=jax.ShapeDtypeStruct((n_rows, half), jnp.uint32),
        compiler_params=pltpu.CompilerParams(
            dimension_semantics=("arbitrary",), vmem_limit_bytes=VMEM_LIMIT),
        name="moe_experts",
    )(tile_expert, n_used, xs, w1, w3, w2)


def _combine_kernel(d1_ref, d2_ref, ys_hbm, x1_ref, route_ref, g2_ref, o_ref, buf_ref, sem):
    i = pl.program_id(0)
    n_steps = pl.num_programs(0)
    tc = COMBINE_TOKENS
    half = D_MODEL // 2

    def issue(step, slot):
        base = step * tc

        def one(t, carry):
            tok = base + t
            pltpu.make_async_copy(ys_hbm.at[pl.ds(d1_ref[tok], 1)],
                                  buf_ref.at[slot, pl.ds(t, 1)], sem.at[slot]).start()
            pltpu.make_async_copy(ys_hbm.at[pl.ds(d2_ref[tok], 1)],
                                  buf_ref.at[slot, pl.ds(tc + t, 1)], sem.at[slot]).start()
            return carry

        lax.fori_loop(0, tc, one, 0)

    slot = i % 2

    @pl.when(i == 0)
    def _():
        issue(0, 0)

    @pl.when(i + 1 < n_steps)
    def _():
        issue(i + 1, 1 - slot)

    pltpu.make_async_copy(ys_hbm.at[pl.ds(0, 2 * tc)], buf_ref.at[slot], sem.at[slot]).wait()

    rows = buf_ref[slot]
    a_lo, a_hi = _unpack_bf16_pair(rows[0:tc])
    b_lo, b_hi = _unpack_bf16_pair(rows[tc:2 * tc])
    route = route_ref[...]
    w_a = route[:, R_WEIGHT1:R_WEIGHT1 + 1]
    w_b = route[:, R_WEIGHT2:R_WEIGHT2 + 1]
    g2 = g2_ref[0]
    o_ref[:, 0:half] = x1_ref[:, 0:half] + g2[:, 0:half] * (w_a * a_lo + w_b * b_lo)
    o_ref[:, half:D_MODEL] = (x1_ref[:, half:D_MODEL]
                              + g2[:, half:D_MODEL] * (w_a * a_hi + w_b * b_hi))


def _combine_call(dest1, dest2, ys, x1, route, gate2, seq):
    tokens = x1.shape[0]
    tc = COMBINE_TOKENS
    per_seq = seq // tc
    half = D_MODEL // 2
    return pl.pallas_call(
        _combine_kernel,
        grid_spec=pltpu.PrefetchScalarGridSpec(
            num_scalar_prefetch=2,
            grid=(tokens // tc,),
            in_specs=[pl.BlockSpec(memory_space=pl.ANY),
                      pl.BlockSpec((tc, D_MODEL), lambda i, d1, d2: (i, 0)),
                      pl.BlockSpec((tc, ROUTER_LANES), lambda i, d1, d2: (i, 0)),
                      pl.BlockSpec((1, 1, D_MODEL), lambda i, d1, d2: (i // per_seq, 0, 0))],
            out_specs=pl.BlockSpec((tc, D_MODEL), lambda i, d1, d2: (i, 0)),
            scratch_shapes=[pltpu.VMEM((2, 2 * tc, half), jnp.uint32),
                            pltpu.SemaphoreType.DMA((2,))]),
        out_shape=jax.ShapeDtypeStruct((tokens, D_MODEL), F32),
        compiler_params=pltpu.CompilerParams(
            dimension_semantics=("arbitrary",), vmem_limit_bytes=VMEM_LIMIT),
        name="moe_combine",
    )(dest1, dest2, ys, x1, route, gate2)


def _moe(h2p, route, counts, x1, gate2, w1, w3, w2, seq):
    tokens = h2p.shape[0]
    n_rows = 2 * tokens + N_EXPERTS * EXPERT_ROWS
    n_rows -= n_rows % EXPERT_ROWS
    n_tiles = n_rows // EXPERT_ROWS

    count = counts[0, EXPERT_LANE0:EXPERT_LANE0 + N_EXPERTS].astype(jnp.int32)
    padded = (count + EXPERT_ROWS - 1) // EXPERT_ROWS * EXPERT_ROWS
    ends = jnp.cumsum(padded)
    starts = ends - padded
    tile_start = jnp.arange(n_tiles, dtype=jnp.int32) * EXPERT_ROWS
    tile_expert = jnp.minimum(jnp.sum(ends[None, :] <= tile_start[:, None], axis=1),
                              N_EXPERTS - 1).astype(jnp.int32)
    n_used = (ends[-1:] // EXPERT_ROWS).astype(jnp.int32)

    start_row = jnp.zeros((1, ROUTER_LANES), F32)
    start_row = start_row.at[0, EXPERT_LANE0:EXPERT_LANE0 + N_EXPERTS].set(starts.astype(F32))
    dest = _dest_call(route, start_row)[:, 0:2].astype(jnp.int32)
    dest1, dest2 = dest[:, 0], dest[:, 1]

    xs = _dispatch_call(dest1, dest2, h2p, n_rows)
    ys = _expert_call(tile_expert, n_used, xs, w1, w3, w2)
    return _combine_call(dest1, dest2, ys, x1, route, gate2, seq)


def _rope_tables(seq):
    half = HEAD // 2
    inv_freq = ROPE_THETA ** (-jnp.arange(half, dtype=F32) / half)
    ang = jnp.arange(seq, dtype=F32)[:, None] * inv_freq[None, :]
    cos = jnp.cos(ang)
    sin = jnp.sin(ang)
    cos_head = jnp.concatenate([cos, cos], axis=1)
    sin_head = jnp.concatenate([-sin, sin], axis=1)
    return jnp.tile(cos_head, (1, RWKV_HEADS)), jnp.tile(sin_head, (1, RWKV_HEADS))


def _layer(x, c, w_ada, b_ada, ln1_g, ln2_g, w_in, mu_shift, w0, w_lora_up, a0, a_lora_up,
           g_lora_up, k_k, k_a, r_k, lnx_g, lnx_b, q_norm_g, k_norm_g, w_br_rwkv, w_br_moba,
           w_out, w_rg, b_rg, w_re, b_re, w1, w3, w2):
    batch, seq, _ = x.shape
    vec = lambda a: a.reshape(1, -1)

    mod = _mod_call(c, w_ada, b_ada)
    shift1, scale1, gate1, shift2, scale2, gate2 = (
        m.reshape(batch, 1, D_MODEL) for m in jnp.split(mod, 6, axis=-1))

    idx = np.arange(WIDTH)
    ones_bd = jnp.asarray(idx[:, None] // HEAD == idx[None, :] // HEAD, BF16)
    t_idx = np.arange(RWKV_ROWS)
    tri = jnp.asarray((t_idx[:, None] >= t_idx[None, :])
                      & (t_idx[:, None] // CHUNK == t_idx[None, :] // CHUNK), F32)
    cos, sin_signed = _rope_tables(seq)

    z_rwkv, q, k, v, gates, kmean = _inproj_call(
        x, shift1, scale1, vec(ln1_g), w_in.astype(BF16),
        vec(jnp.tile(q_norm_g, RWKV_HEADS)), vec(jnp.tile(k_norm_g, RWKV_HEADS)),
        cos, sin_signed, ones_bd)

    lora_cat = jnp.zeros((DECAY_LORA + AAA_LORA, 2 * WIDTH), F32)
    lora_cat = lora_cat.at[:DECAY_LORA, :WIDTH].set(w_lora_up).at[DECAY_LORA:, WIDTH:].set(a_lora_up)
    y_a = _rwkv_call(z_rwkv, vec(mu_shift), vec(w0), vec(a0), lora_cat, g_lora_up, vec(k_k),
                     vec(k_a), vec(r_k), vec(lnx_g), vec(lnx_b), ones_bd, tri)

    y_b = _moba_call(q, k, v, kmean.reshape(batch, seq // MOBA_BLOCK, WIDTH))

    tokens = batch * seq
    w_router = jnp.zeros((D_MODEL, ROUTER_LANES), F32)
    w_router = w_router.at[:, :N_GROUPS].set(w_rg).at[:, EXPERT_LANE0:EXPERT_LANE0 + N_EXPERTS].set(w_re)
    b_router = jnp.zeros((1, ROUTER_LANES), F32)
    b_router = b_router.at[0, :N_GROUPS].set(b_rg).at[0, EXPERT_LANE0:EXPERT_LANE0 + N_EXPERTS].set(b_re)
    m_idx = np.arange(MERGE_ROWS)
    tri_strict = jnp.asarray(m_idx[:, None] > m_idx[None, :], BF16)
    x1, h2p, route, counts = _merge_call(
        x.reshape(tokens, D_MODEL), y_a.reshape(tokens, WIDTH), y_b.reshape(tokens, WIDTH),
        gates.reshape(tokens, GATE_COLS), w_br_rwkv.astype(BF16), w_br_moba.astype(BF16),
        w_out.astype(BF16), gate1, vec(ln2_g), scale2, shift2, w_router, b_router, tri_strict,
        seq)

    out = _moe(h2p, route, counts, x1, gate2, w1.astype(BF16), w3.astype(BF16), w2.astype(BF16),
               seq)
    return out.reshape(batch, seq, D_MODEL)


def kernel(x, c, w_ada, b_ada, ln1_g, ln2_g, w_in, mu_shift, w0, w_lora_up, a0, a_lora_up,
           g_lora_up, k_k, k_a, r_k, lnx_g, lnx_b, q_norm_g, k_norm_g, w_br_rwkv, w_br_moba,
           w_out, w_rg, b_rg, w_re, b_re, w1, w3, w2):
    assert w_ada.shape[0] == 1, "single-layer problem"
    layer_params = (w_ada, b_ada, ln1_g, ln2_g, w_in, mu_shift, w0, w_lora_up, a0, a_lora_up,
                    g_lora_up, k_k, k_a, r_k, lnx_g, lnx_b, q_norm_g, k_norm_g, w_br_rwkv,
                    w_br_moba, w_out, w_rg, b_rg, w_re, b_re, w1, w3, w2)
    return _layer(x, c, *(p[0] for p in layer_params))
```

```python
import functools

import jax
import jax.numpy as jnp
import numpy as np
from jax import lax
from jax.experimental import pallas as pl
from jax.experimental.pallas import tpu as pltpu

F32 = jnp.float32
BF16 = jnp.bfloat16
HIGHEST = lax.Precision.HIGHEST

D_MODEL = 1024
RWKV_HEADS = 8
HEAD = 64
WIDTH = RWKV_HEADS * HEAD
DECAY_LORA = 64
AAA_LORA = 64
GATE_LORA = 128
RWKV_COLS = 3 * WIDTH + DECAY_LORA + AAA_LORA + GATE_LORA
ATT_COLS = 3 * WIDTH
GATE_COLS = 2 * D_MODEL
IN_COLS = RWKV_COLS + ATT_COLS + GATE_COLS
DECAY_SCALE = 0.606531
LN_X_EPS = 64e-5
MOBA_BLOCK = 256
MOBA_TOPK = 3
ROPE_THETA = 10000.0
N_GROUPS = 4
EXPERTS_PER_GROUP = 8
N_EXPERTS = N_GROUPS * EXPERTS_PER_GROUP
D_EXPERT = D_MODEL // 2
NORM_EPS = 1e-6
NEG_INF = -1e30

LANES = 128
MXU_DIM = 256
PAIRS = WIDTH // LANES
CHUNK = 64
SOLVE_BLOCK = 16
RWKV_ROWS = 2 * CHUNK
MOBA_SUB = 128
MOBA_PAIRS = 4
VMEM_LIMIT = 56 * 1024 * 1024

ROUTER_LANES = LANES
EXPERT_LANE0 = N_GROUPS


def _dot(a, b):
    return jnp.dot(a.astype(BF16), b.astype(BF16), preferred_element_type=F32)


def _dot_nt(a, b):
    return lax.dot_general(a.astype(BF16), b.astype(BF16), (((1,), (1,)), ((), ())),
                           preferred_element_type=F32)


def _dot_f32(a, b):
    return jnp.dot(a, b, precision=HIGHEST, preferred_element_type=F32)


def _split_bf16(a):
    hi = a.astype(BF16)
    return hi, (a - hi.astype(F32)).astype(BF16)


def _dot_x3(a, b):
    a_hi, a_lo = _split_bf16(a)
    b_hi, b_lo = _split_bf16(b)
    return (jnp.dot(a_hi, b_hi, preferred_element_type=F32)
            + jnp.dot(a_hi, b_lo, preferred_element_type=F32)
            + jnp.dot(a_lo, b_hi, preferred_element_type=F32))


def _dot_exact_lhs(a_bf16, b):
    hi, lo = _split_bf16(b)
    return (jnp.dot(a_bf16, hi, preferred_element_type=F32)
            + jnp.dot(a_bf16, lo, preferred_element_type=F32))


def _seg_sum(x, ones_bd):
    group = ones_bd.shape[0]
    hi, lo = _split_bf16(x)
    cols = []
    for c0 in range(0, x.shape[1], group):
        cols.append(jnp.dot(hi[:, c0:c0 + group], ones_bd, preferred_element_type=F32)
                    + jnp.dot(lo[:, c0:c0 + group], ones_bd, preferred_element_type=F32))
    return jnp.concatenate(cols, axis=1)


def _iota(shape, axis):
    return lax.broadcasted_iota(jnp.int32, shape, axis)


def _mod_kernel(c_ref, w_ref, b_ref, o_ref):
    c = c_ref[...]
    o_ref[...] = _dot_f32(c * jax.nn.sigmoid(c), w_ref[...]) + b_ref[...]


def _mod_call(c, w_ada, b_ada):
    batch = c.shape[0]
    n_out = w_ada.shape[1]
    tn = D_MODEL
    return pl.pallas_call(
        _mod_kernel,
        grid=(n_out // tn,),
        in_specs=[pl.BlockSpec((batch, D_MODEL), lambda j: (0, 0)),
                  pl.BlockSpec((D_MODEL, tn), lambda j: (0, j)),
                  pl.BlockSpec((1, tn), lambda j: (0, j))],
        out_specs=pl.BlockSpec((batch, tn), lambda j: (0, j)),
        out_shape=jax.ShapeDtypeStruct((batch, n_out), F32),
        name="adaln_mod",
    )(c, w_ada, b_ada.reshape(1, n_out))


def _swap_halves(x):
    first = (_iota(x.shape, 1) & (HEAD - 1)) < HEAD // 2
    up = pltpu.roll(x, LANES - HEAD // 2, axis=1)
    down = pltpu.roll(x, HEAD // 2, axis=1)
    return jnp.where(first, up, down)


def _head_norm_rope(x, gain, cos, sin_signed, ones_bd):
    ms = _seg_sum(x * x, ones_bd) * (1.0 / HEAD)
    y = x * lax.rsqrt(ms + NORM_EPS) * gain
    cols = []
    for p in range(PAIRS):
        sl = slice(p * LANES, (p + 1) * LANES)
        yb = y[:, sl]
        cols.append(yb * cos[:, sl] + _swap_halves(yb) * sin_signed[:, sl])
    return jnp.concatenate(cols, axis=1)


def _inproj_kernel(x_ref, shift_ref, scale_ref, g_ref, w_ref, qg_ref, kg_ref, cos_ref, sin_ref,
                   ones_ref, zr_ref, q_ref, k_ref, v_ref, gate_ref, kmean_ref):
    x = x_ref[0]
    ms = jnp.mean(x * x, axis=-1, keepdims=True)
    h = x * lax.rsqrt(ms + NORM_EPS) * g_ref[...]
    h = (h * (1.0 + scale_ref[0]) + shift_ref[0]).astype(BF16)

    zr_ref[0] = jnp.dot(h, w_ref[:, 0:RWKV_COLS], preferred_element_type=F32)

    za = jnp.dot(h, w_ref[:, RWKV_COLS:RWKV_COLS + ATT_COLS], preferred_element_type=F32)
    ones_bd = ones_ref[...]
    cos = cos_ref[...]
    sin = sin_ref[...]
    q = _head_norm_rope(za[:, 0:WIDTH], qg_ref[...], cos, sin, ones_bd)
    k = _head_norm_rope(za[:, WIDTH:2 * WIDTH], kg_ref[...], cos, sin, ones_bd)
    q_ref[0] = (q * (HEAD ** -0.5)).astype(BF16)
    k_ref[0] = k.astype(BF16)
    v_ref[0] = za[:, 2 * WIDTH:3 * WIDTH].astype(BF16)
    kmean_ref[0] = jnp.mean(k, axis=0, keepdims=True)

    zg = jnp.dot(h, w_ref[:, RWKV_COLS + ATT_COLS:IN_COLS], preferred_element_type=F32)
    gate_ref[0] = jax.nn.sigmoid(zg).astype(BF16)


def _inproj_call(x, shift1, scale1, ln1_g, w_in_bf16, q_gain, k_gain, cos, sin_signed, ones_bd):
    batch, seq, _ = x.shape
    tm = MOBA_BLOCK
    n_t = seq // tm
    row = lambda width: pl.BlockSpec((1, tm, width), lambda b, i: (b, i, 0))
    per_batch = pl.BlockSpec((1, 1, D_MODEL), lambda b, i: (b, 0, 0))
    const = lambda shape: pl.BlockSpec(shape, lambda b, i: (0,) * len(shape))
    return pl.pallas_call(
        _inproj_kernel,
        grid=(batch, n_t),
        in_specs=[row(D_MODEL), per_batch, per_batch, const((1, D_MODEL)),
                  const((D_MODEL, IN_COLS)), const((1, WIDTH)), const((1, WIDTH)),
                  pl.BlockSpec((tm, WIDTH), lambda b, i: (i, 0)),
                  pl.BlockSpec((tm, WIDTH), lambda b, i: (i, 0)),
                  const((MXU_DIM, MXU_DIM))],
        out_specs=[row(RWKV_COLS), row(WIDTH), row(WIDTH), row(WIDTH), row(GATE_COLS),
                   pl.BlockSpec((1, 1, WIDTH), lambda b, i: (b * n_t + i, 0, 0))],
        out_shape=[jax.ShapeDtypeStruct((batch, seq, RWKV_COLS), F32),
                   jax.ShapeDtypeStruct((batch, seq, WIDTH), BF16),
                   jax.ShapeDtypeStruct((batch, seq, WIDTH), BF16),
                   jax.ShapeDtypeStruct((batch, seq, WIDTH), BF16),
                   jax.ShapeDtypeStruct((batch, seq, GATE_COLS), BF16),
                   jax.ShapeDtypeStruct((batch * n_t, 1, WIDTH), F32)],
        compiler_params=pltpu.CompilerParams(
            dimension_semantics=("arbitrary", "arbitrary"), vmem_limit_bytes=VMEM_LIMIT),
        name="inproj",
    )(x, shift1, scale1, ln1_g, w_in_bf16, q_gain, k_gain, cos, sin_signed, ones_bd)


def _stack_heads(x):
    first = _iota(x.shape, 1) < HEAD
    return jnp.concatenate([jnp.where(first, x, 0.0), jnp.where(first, 0.0, x)], axis=0)


def _add_eye(x, eye):
    return jnp.where(eye, x + 1.0, x)


def _rwkv_kernel(z_ref, mu_ref, w0_ref, a0_ref, lora_ref, glora_ref, kk_ref, ka_ref, rk_ref,
                 lng_ref, lnb_ref, ones_ref, tri_ref, o_ref, prev_ref, state_ref):
    c = pl.program_id(1)

    @pl.when(c == 0)
    def _():
        prev_ref[...] = jnp.zeros_like(prev_ref)
        state_ref[...] = jnp.zeros_like(state_ref)

    z = z_ref[0]
    rows = z.shape[0]
    n_chunks = rows // CHUNK
    row = _iota(z.shape, 0)
    z_prev = jnp.where(row == 0, prev_ref[...], pltpu.roll(z, 1, axis=0))
    prev_ref[...] = z[rows - 1:rows, :]
    zs = z + (z_prev - z) * mu_ref[...]

    r = zs[:, 0:WIDTH]
    k = zs[:, WIDTH:2 * WIDTH]
    v = zs[:, 2 * WIDTH:3 * WIDTH]
    lo = 3 * WIDTH
    d_wa = zs[:, lo:lo + DECAY_LORA + AAA_LORA]
    d_g = zs[:, lo + DECAY_LORA + AAA_LORA:RWKV_COLS]
    is_decay = _iota(d_wa.shape, 1) < DECAY_LORA
    pre = _dot_x3(jnp.where(is_decay, jnp.tanh(d_wa), d_wa), lora_ref[...])
    log_w = -DECAY_SCALE * jax.nn.sigmoid(w0_ref[...] + pre[:, 0:WIDTH])
    a = jax.nn.sigmoid(a0_ref[...] + pre[:, WIDTH:2 * WIDTH])
    g = _dot_x3(jax.nn.sigmoid(d_g), glora_ref[...])

    ones_bd = ones_ref[...]
    kk = k * kk_ref[...]
    kk = kk / jnp.maximum(jnp.sqrt(_seg_sum(kk * kk, ones_bd)), 1e-12)
    k = k * (1.0 + (a - 1.0) * ka_ref[...])
    bonus = _seg_sum(r * k * rk_ref[...], ones_bd) * v

    cl = _dot_exact_lhs(tri_ref[...], log_w)
    p_end_rows = [cl[(c_i + 1) * CHUNK - 1:(c_i + 1) * CHUNK, :] for c_i in range(n_chunks)]
    cl_last = p_end_rows[0]
    chunk_of_row = _iota(cl.shape, 0) // CHUNK
    for c_i in range(1, n_chunks):
        cl_last = jnp.where(chunk_of_row == c_i, p_end_rows[c_i], cl_last)
    a_t = -kk * jnp.exp(cl - log_w)
    e_neg = jnp.exp(-cl)
    b_t = kk * a * e_neg
    k_t = k * e_neg
    r_t = r * jnp.exp(cl)
    e_end = jnp.exp(cl_last - cl)
    b_end = kk * a * e_end
    k_end = k * e_end
    p_end = jnp.exp(cl_last)

    n2 = 2 * CHUNK
    ri = _iota((n2, n2), 0)
    ci = _iota((n2, n2), 1)
    eye = ri == ci
    same_blk = (ri // SOLVE_BLOCK) == (ci // SOLVE_BLOCK)
    ri4 = _iota((2 * n2, 2 * n2), 0)
    ci4 = _iota((2 * n2, 2 * n2), 1) & (n2 - 1)
    causal4 = ((ri4 < n2) & (ri4 > ci4)) | ((ri4 >= n2) & ((ri4 - n2) >= ci4))
    zeros = jnp.zeros((n2, n2), F32)

    items = [(c_i, p) for c_i in range(n_chunks) for p in range(PAIRS)]

    def tile(t, item):
        c_i, p = item
        return t[c_i * CHUNK:(c_i + 1) * CHUNK, p * LANES:(p + 1) * LANES]

    sa = [_stack_heads(tile(a_t, it)) for it in items]
    sr = [_stack_heads(tile(r_t, it)) for it in items]
    sv = [_stack_heads(tile(v, it)) for it in items]
    big = [jnp.where(causal4,
                     _dot_nt(jnp.concatenate([sa[i], sr[i]], axis=0),
                             jnp.concatenate([_stack_heads(tile(b_t, it)),
                                              _stack_heads(tile(k_t, it))], axis=0)), 0.0)
           for i, it in enumerate(items)]
    a_ab = [b[0:n2, 0:n2] for b in big]
    a_ak = [b[0:n2, n2:2 * n2] for b in big]
    a_rbk = [b[n2:2 * n2, :] for b in big]

    def pair_dot(xs, ys, prod=_dot):
        out = []
        for i in range(0, len(xs), 2):
            wide = prod(jnp.concatenate([xs[i], xs[i + 1]], axis=1),
                        jnp.concatenate([jnp.concatenate([ys[i], zeros], axis=1),
                                         jnp.concatenate([zeros, ys[i + 1]], axis=1)], axis=0))
            out += [wide[:, 0:n2], wide[:, n2:2 * n2]]
        return out

    def plus_eye(xs):
        return [_add_eye(x, eye) for x in xs]

    d1 = [jnp.where(same_blk, a, 0.0) for a in a_ab]
    e1 = [a - d for a, d in zip(a_ab, d1)]
    d2 = pair_dot(d1, d1)
    d4 = pair_dot(d2, d2)
    d8 = pair_dot(d4, d4)
    p12 = pair_dot(plus_eye(d1), plus_eye(d2))
    p48 = pair_dot(plus_eye(d4), plus_eye(d8))
    t_d = pair_dot(p12, p48)
    g1 = pair_dot(t_d, e1)
    g2 = pair_dot(g1, g1)
    gx = pair_dot(plus_eye(g1), plus_eye(g2))
    t_inv = pair_dot(gx, t_d)

    akv = pair_dot(a_ak, sv)
    wu = [_dot(t, jnp.concatenate([s, x], axis=1)) for t, s, x in zip(t_inv, sa, akv)]
    rhs = [jnp.concatenate([w, jnp.concatenate([zeros, s], axis=1)], axis=0)
           for w, s in zip(wu, sv)]
    out_c = [_dot(a, x) for a, x in zip(a_rbk, rhs)]
    end_t = [jnp.concatenate([_stack_heads(tile(b_end, it)).T,
                              _stack_heads(tile(k_end, it)).T], axis=1) for it in items]
    end_c = [_dot(e, x) for e, x in zip(end_t, rhs)]

    y_rows = []
    for c_i in range(n_chunks):
        psi_phi = []
        for p in range(PAIRS):
            i = c_i * PAIRS + p
            psi = sr[i] + out_c[i][:, 0:n2]
            pe = p_end[c_i * CHUNK:c_i * CHUNK + 1, p * LANES:(p + 1) * LANES]
            phi = jnp.where(eye, pe, 0.0) + end_c[i][:, 0:n2]
            psi_phi.append(jnp.concatenate([psi, phi], axis=0))
        both = pair_dot(psi_phi, [state_ref[p] for p in range(PAIRS)], prod=_dot_x3)
        y_cols = []
        for p in range(PAIRS):
            i = c_i * PAIRS + p
            y = both[p][0:n2, :] + out_c[i][:, n2:2 * n2]
            state_ref[p] = both[p][n2:2 * n2, :] + end_c[i][:, n2:2 * n2]
            y_cols.append(y[0:CHUNK, :] + y[CHUNK:n2, :])
        y_rows.append(jnp.concatenate(y_cols, axis=1))
    y = jnp.concatenate(y_rows, axis=0)
    mean = _seg_sum(y, ones_bd) * (1.0 / HEAD)
    yc = y - mean
    var = _seg_sum(yc * yc, ones_bd) * (1.0 / HEAD)
    yn = yc * lax.rsqrt(var + LN_X_EPS) * lng_ref[...] + lnb_ref[...]
    o_ref[0] = ((yn + bonus) * g).astype(o_ref.dtype)


def _rwkv_call(z_rwkv, mu_shift, w0, a0, lora_cat, g_lora_up, k_k, k_a, r_k, lnx_g, lnx_b,
               ones_bd, tri):
    batch, seq, _ = z_rwkv.shape
    rows = RWKV_ROWS
    const = lambda shape: pl.BlockSpec(shape, lambda b, c: (0,) * len(shape))
    vec = const((1, WIDTH))
    return pl.pallas_call(
        _rwkv_kernel,
        grid=(batch, seq // rows),
        in_specs=[pl.BlockSpec((1, rows, RWKV_COLS), lambda b, c: (b, c, 0)),
                  const((1, RWKV_COLS)), vec, vec,
                  const((DECAY_LORA + AAA_LORA, 2 * WIDTH)), const((GATE_LORA, WIDTH)),
                  vec, vec, vec, vec, vec, const((MXU_DIM, MXU_DIM)), const((rows, rows))],
        out_specs=pl.BlockSpec((1, rows, WIDTH), lambda b, c: (b, c, 0)),
        out_shape=jax.ShapeDtypeStruct((batch, seq, WIDTH), BF16),
        scratch_shapes=[pltpu.VMEM((1, RWKV_COLS), F32),
                        pltpu.VMEM((PAIRS, LANES, LANES), F32)],
        compiler_params=pltpu.CompilerParams(
            dimension_semantics=("arbitrary", "arbitrary"), vmem_limit_bytes=VMEM_LIMIT),
        name="rwkv7",
    )(z_rwkv, mu_shift, w0, a0, lora_cat, g_lora_up, k_k, k_a, r_k, lnx_g, lnx_b, ones_bd, tri)


def _moba_kernel(q_ref, k_ref, v_ref, km_ref, o_ref, bias_ref, m_ref, l_ref, acc_ref):
    qi = pl.program_id(2)
    tq = q_ref.shape[1]
    n_blk = km_ref.shape[1]
    n_pairs = q_ref.shape[2] // LANES
    n_heads = 2 * n_pairs
    first = _iota((tq, LANES), 1) < HEAD
    pair_lanes = [slice(p * LANES, (p + 1) * LANES) for p in range(n_pairs)]
    qh = []
    for lanes in pair_lanes:
        q = q_ref[0, :, lanes]
        zero = jnp.zeros_like(q)
        qh += [jnp.where(first, q, zero), jnp.where(first, zero, q)]

    m_ref[...] = jnp.full(m_ref.shape, NEG_INF, F32)
    l_ref[...] = jnp.zeros(l_ref.shape, F32)
    acc_ref[...] = jnp.zeros(acc_ref.shape, F32)
    tiles = [(h, r0) for h in range(n_heads) for r0 in range(0, tq, MOBA_SUB)]
    q_tiles = [qh[h][r0:r0 + MOBA_SUB] for h, r0 in tiles]

    def update(scores, v_blks):
        n_keys = v_blks[0].shape[0]
        v_ext = [jnp.concatenate([v, jnp.ones((n_keys, LANES), BF16)], axis=1) for v in v_blks]
        refs = [(h, slice(r0, r0 + MOBA_SUB)) for h, r0 in tiles]
        m_old = [m_ref[h, rows, :] for h, rows in refs]
        l_old = [l_ref[h, rows, :] for h, rows in refs]
        acc_old = [acc_ref[h, rows, :] for h, rows in refs]
        m_new = [jnp.maximum(m, jnp.broadcast_to(jnp.max(s, axis=-1, keepdims=True),
                                                 (MOBA_SUB, LANES)))
                 for m, s in zip(m_old, scores)]
        alpha = [jnp.exp(m - n) for m, n in zip(m_old, m_new)]
        p = [jnp.exp(s - jnp.concatenate([n] * (n_keys // LANES), axis=1)).astype(BF16)
             for s, n in zip(scores, m_new)]
        pv = [jnp.dot(x, v_ext[h // 2], preferred_element_type=F32)
              for x, (h, _) in zip(p, tiles)]
        for (h, rows), m, a, l, acc, y in zip(refs, m_new, alpha, l_old, acc_old, pv):
            m_ref[h, rows, :] = m
            l_ref[h, rows, :] = a * l + y[:, LANES:2 * LANES]
            acc_ref[h, rows, :] = a * acc + y[:, 0:LANES]

    start = pl.multiple_of(qi * MOBA_BLOCK, MOBA_BLOCK)
    k_own = [k_ref[0, pl.ds(start, MOBA_BLOCK), lanes] for lanes in pair_lanes]
    v_own = [v_ref[0, pl.ds(start, MOBA_BLOCK), lanes] for lanes in pair_lanes]
    row = _iota((MOBA_SUB, MOBA_BLOCK), 0)
    col = _iota((MOBA_SUB, MOBA_BLOCK), 1)
    update([jnp.where(col <= row + r0, _dot_nt(qt, k_own[h // 2]), NEG_INF)
            for qt, (h, r0) in zip(q_tiles, tiles)], v_own)

    blk = _iota((n_blk, tq), 0)
    valid = blk < qi
    for h in range(n_heads):
        km = km_ref[0, :, pair_lanes[h // 2]]
        gate = lax.dot_general(km, qh[h].astype(F32), (((1,), (1,)), ((), ())),
                               precision=HIGHEST, preferred_element_type=F32)
        gate = jnp.where(valid, gate, NEG_INF)
        bias = jnp.full((n_blk, tq), NEG_INF, F32)
        for n in range(n_blk):
            g_n = gate[n:n + 1, :]
            ahead = valid & ((gate > g_n) | ((gate == g_n) & (blk < n)))
            rank = jnp.sum(ahead.astype(F32), axis=0, keepdims=True)
            chosen = (rank < MOBA_TOPK) & (qi > n)
            bias = jnp.where((blk == n) & chosen, 0.0, bias)
        bias_ref[h] = bias

    eye = (_iota((tq, tq), 0) == _iota((tq, tq), 1)).astype(BF16)

    def past_blocks(j, carry):
        start = pl.multiple_of(j * (2 * MOBA_BLOCK), 2 * MOBA_BLOCK)
        k_blks = [k_ref[0, pl.ds(start, 2 * MOBA_BLOCK), lanes] for lanes in pair_lanes]
        v_blks = [v_ref[0, pl.ds(start, 2 * MOBA_BLOCK), lanes] for lanes in pair_lanes]
        bias_rows = [jnp.concatenate(
            [jnp.broadcast_to(bias_ref[h, pl.ds(2 * j + i, 1), :], (MOBA_BLOCK, tq))
             for i in range(2)], axis=0).astype(BF16) for h in range(n_heads)]
        scores = [_dot_nt(qt, k_blks[h // 2]) + _dot_nt(eye[r0:r0 + MOBA_SUB], bias_rows[h])
                  for qt, (h, r0) in zip(q_tiles, tiles)]
        update(scores, v_blks)
        return carry

    lax.fori_loop(0, (qi + 1) // 2, past_blocks, 0)

    for p, lanes in enumerate(pair_lanes):
        o_ref[0, :, lanes] = jnp.where(first, acc_ref[2 * p] / l_ref[2 * p],
                                       acc_ref[2 * p + 1] / l_ref[2 * p + 1]).astype(o_ref.dtype)


def _moba_call(q, k, v, kmean):
    batch, seq, _ = q.shape
    tq = MOBA_BLOCK
    n_blk = seq // MOBA_BLOCK
    width = MOBA_PAIRS * LANES
    n_heads = 2 * MOBA_PAIRS
    return pl.pallas_call(
        _moba_kernel,
        grid=(batch, PAIRS // MOBA_PAIRS, seq // tq),
        in_specs=[pl.BlockSpec((1, tq, width), lambda b, p, i: (b, i, p)),
                  pl.BlockSpec((1, seq, width), lambda b, p, i: (b, 0, p)),
                  pl.BlockSpec((1, seq, width), lambda b, p, i: (b, 0, p)),
                  pl.BlockSpec((1, n_blk, width), lambda b, p, i: (b, 0, p))],
        out_specs=pl.BlockSpec((1, tq, width), lambda b, p, i: (b, i, p)),
        out_shape=jax.ShapeDtypeStruct((batch, seq, WIDTH), BF16),
        scratch_shapes=[pltpu.VMEM((n_heads, n_blk, tq), F32),
                        pltpu.VMEM((n_heads, tq, LANES), F32),
                        pltpu.VMEM((n_heads, tq, LANES), F32),
                        pltpu.VMEM((n_heads, tq, LANES), F32)],
        compiler_params=pltpu.CompilerParams(
            dimension_semantics=("arbitrary", "arbitrary", "arbitrary"),
            vmem_limit_bytes=VMEM_LIMIT),
        name="moba",
    )(q, k, v, kmean)


def _pack_bf16_pair(lo, hi):
    lo_bits = lax.bitcast_convert_type(lo.astype(BF16).astype(F32), jnp.uint32)
    hi_bits = lax.bitcast_convert_type(hi.astype(BF16).astype(F32), jnp.uint32)
    return (lo_bits >> 16) | (hi_bits & jnp.uint32(0xFFFF0000))


def _unpack_bf16_pair(u):
    lo = lax.bitcast_convert_type(u << 16, F32)
    hi = lax.bitcast_convert_type(u & jnp.uint32(0xFFFF0000), F32)
    return lo, hi


def _merge_kernel(x_ref, ya_ref, yb_ref, gate_ref, wa_ref, wb_ref, wo_ref, g1_ref, ln_ref,
                  scale_ref, shift_ref, wr_ref, br_ref, tri_ref, x1_ref, h2_ref, route_ref,
                  count_ref, carry_ref):
    @pl.when(pl.program_id(0) == 0)
    def _():
        carry_ref[...] = jnp.zeros_like(carry_ref)

    ya = jnp.dot(ya_ref[...], wa_ref[...], preferred_element_type=F32)
    yb = jnp.dot(yb_ref[...], wb_ref[...], preferred_element_type=F32)
    gates = gate_ref[...]
    merged = (gates[:, 0:D_MODEL].astype(F32) * ya + gates[:, D_MODEL:GATE_COLS].astype(F32) * yb)
    x1 = x_ref[...] + g1_ref[0] * jnp.dot(merged.astype(BF16), wo_ref[...],
                                          preferred_element_type=F32)
    x1_ref[...] = x1

    ms = jnp.mean(x1 * x1, axis=-1, keepdims=True)
    h2 = x1 * lax.rsqrt(ms + NORM_EPS) * ln_ref[...]
    h2 = h2 * (1.0 + scale_ref[0]) + shift_ref[0]
    half = D_MODEL // 2
    h2_ref[...] = _pack_bf16_pair(h2[:, 0:half], h2[:, half:D_MODEL])

    logits = _dot_x3(h2, wr_ref[...]) + br_ref[...]
    lane = _iota(logits.shape, 1)
    lane_f = lane.astype(F32)
    far = float(ROUTER_LANES)

    def top(vals):
        m = jnp.max(vals, axis=-1, keepdims=True)
        idx = jnp.min(jnp.where(vals == m, lane_f, far), axis=-1, keepdims=True)
        return m, idx

    grp = jnp.where(lane < N_GROUPS, logits, NEG_INF)
    g_max, g_idx = top(grp)
    p_group = 1.0 / jnp.sum(jnp.exp(grp - g_max), axis=-1, keepdims=True)

    e_lo = EXPERT_LANE0 + EXPERTS_PER_GROUP * g_idx
    in_grp = (lane_f >= e_lo) & (lane_f < e_lo + EXPERTS_PER_GROUP)
    el = jnp.where(in_grp, logits, NEG_INF)
    m1, i1 = top(el)
    m2, i2 = top(jnp.where(lane_f == i1, NEG_INF, el))
    ratio = jnp.exp(m2 - m1)
    w_first = p_group / (1.0 + ratio)
    w_second = w_first * ratio

    first = lane_f == i1
    second = lane_f == i2
    hits = (first | second).astype(BF16)
    before = carry_ref[...] + jnp.dot(tri_ref[...], hits, preferred_element_type=F32)
    rank1 = jnp.sum(jnp.where(first, before, 0.0), axis=-1, keepdims=True)
    rank2 = jnp.sum(jnp.where(second, before, 0.0), axis=-1, keepdims=True)
    carry = carry_ref[...] + jnp.sum(hits.astype(F32), axis=0, keepdims=True)
    carry_ref[...] = carry
    count_ref[...] = carry

    fields = (i1 - EXPERT_LANE0, i2 - EXPERT_LANE0, w_first, w_second, rank1, rank2)
    route = jnp.zeros(logits.shape, F32)
    for n, field in enumerate(fields):
        route = jnp.where(lane == n, field, route)
    route_ref[...] = route


R_EXPERT1, R_EXPERT2, R_WEIGHT1, R_WEIGHT2, R_RANK1, R_RANK2 = range(6)
MERGE_ROWS = 256


def _merge_call(x2d, ya, yb, gates, w_br_rwkv, w_br_moba, w_out, gate1, ln2_g, scale2, shift2,
                w_router, b_router, tri, seq):
    tokens = x2d.shape[0]
    tm = MERGE_ROWS
    per_seq = seq // tm
    row = lambda width: pl.BlockSpec((tm, width), lambda i: (i, 0))
    per_batch = pl.BlockSpec((1, 1, D_MODEL), lambda i: (i // per_seq, 0, 0))
    const = lambda shape: pl.BlockSpec(shape, lambda i: (0,) * len(shape))
    return pl.pallas_call(
        _merge_kernel,
        grid=(tokens // tm,),
        in_specs=[row(D_MODEL), row(WIDTH), row(WIDTH), row(GATE_COLS),
                  const((WIDTH, D_MODEL)), const((WIDTH, D_MODEL)), const((D_MODEL, D_MODEL)),
                  per_batch, const((1, D_MODEL)), per_batch, per_batch,
                  const((D_MODEL, ROUTER_LANES)), const((1, ROUTER_LANES)), const((tm, tm))],
        out_specs=[row(D_MODEL), row(D_MODEL // 2), row(ROUTER_LANES),
                   const((1, ROUTER_LANES))],
        out_shape=[jax.ShapeDtypeStruct((tokens, D_MODEL), F32),
                   jax.ShapeDtypeStruct((tokens, D_MODEL // 2), jnp.uint32),
                   jax.ShapeDtypeStruct((tokens, ROUTER_LANES), F32),
                   jax.ShapeDtypeStruct((1, ROUTER_LANES), F32)],
        scratch_shapes=[pltpu.VMEM((1, ROUTER_LANES), F32)],
        compiler_params=pltpu.CompilerParams(
            dimension_semantics=("arbitrary",), vmem_limit_bytes=VMEM_LIMIT),
        name="merge",
    )(x2d, ya, yb, gates, w_br_rwkv, w_br_moba, w_out, gate1, ln2_g, scale2, shift2,
      w_router, b_router, tri)


EXPERT_ROWS = 256
DISPATCH_TOKENS = 1024
COMBINE_TOKENS = 256


def _dest_kernel(route_ref, start_ref, o_ref):
    route = route_ref[...]
    lane = _iota(route.shape, 1)
    lane_f = lane.astype(F32)
    starts = start_ref[...]
    out = jnp.zeros(route.shape, F32)
    for n, (e_lane, r_lane) in enumerate(((R_EXPERT1, R_RANK1), (R_EXPERT2, R_RANK2))):
        e_col = route[:, e_lane:e_lane + 1] + EXPERT_LANE0
        first_row = jnp.sum(jnp.where(lane_f == e_col, starts, 0.0), axis=-1, keepdims=True)
        out = jnp.where(lane == n, first_row + route[:, r_lane:r_lane + 1], out)
    o_ref[...] = out


def _dest_call(route, start_row):
    tokens = route.shape[0]
    tm = DISPATCH_TOKENS
    return pl.pallas_call(
        _dest_kernel,
        grid=(tokens // tm,),
        in_specs=[pl.BlockSpec((tm, ROUTER_LANES), lambda i: (i, 0)),
                  pl.BlockSpec((1, ROUTER_LANES), lambda i: (0, 0))],
        out_specs=pl.BlockSpec((tm, ROUTER_LANES), lambda i: (i, 0)),
        out_shape=jax.ShapeDtypeStruct((tokens, ROUTER_LANES), F32),
        name="moe_dest",
    )(route, start_row)


def _dispatch_kernel(d1_ref, d2_ref, h_ref, xs_init_hbm, xs_hbm, sem):
    del xs_init_hbm
    base = pl.program_id(0) * DISPATCH_TOKENS

    def issue(t, carry):
        src = h_ref.at[pl.ds(t, 1)]
        pltpu.make_async_copy(src, xs_hbm.at[pl.ds(d1_ref[base + t], 1)], sem).start()
        pltpu.make_async_copy(src, xs_hbm.at[pl.ds(d2_ref[base + t], 1)], sem).start()
        return carry

    lax.fori_loop(0, DISPATCH_TOKENS, issue, 0, unroll=4)
    for _ in range(2):
        pltpu.make_async_copy(h_ref, xs_hbm.at[pl.ds(0, DISPATCH_TOKENS)], sem).wait()


def _dispatch_call(dest1, dest2, h2p, n_rows):
    tokens = h2p.shape[0]
    any_spec = pl.BlockSpec(memory_space=pl.ANY)
    return pl.pallas_call(
        _dispatch_kernel,
        grid_spec=pltpu.PrefetchScalarGridSpec(
            num_scalar_prefetch=2,
            grid=(tokens // DISPATCH_TOKENS,),
            in_specs=[pl.BlockSpec((DISPATCH_TOKENS, D_MODEL // 2), lambda i, d1, d2: (i, 0)),
                      any_spec],
            out_specs=any_spec,
            scratch_shapes=[pltpu.SemaphoreType.DMA(())]),
        out_shape=jax.ShapeDtypeStruct((n_rows, D_MODEL // 2), jnp.uint32),
        input_output_aliases={3: 0},
        compiler_params=pltpu.CompilerParams(dimension_semantics=("arbitrary",)),
        name="moe_dispatch",
    )(dest1, dest2, h2p, jnp.zeros((n_rows, D_MODEL // 2), jnp.uint32))


def _expert_kernel(te_ref, nu_ref, xs_ref, w1_ref, w3_ref, w2_ref, ys_ref):
    half = D_MODEL // 2

    @pl.when(pl.program_id(0) < nu_ref[0])
    def _():
        x_lo, x_hi = _unpack_bf16_pair(xs_ref[...])
        x_lo = x_lo.astype(BF16)
        x_hi = x_hi.astype(BF16)

        def proj(w_ref):
            return (jnp.dot(x_lo, w_ref[0, 0:half, :], preferred_element_type=F32)
                    + jnp.dot(x_hi, w_ref[0, half:D_MODEL, :], preferred_element_type=F32))

        a = proj(w1_ref)
        hid = (a * jax.nn.sigmoid(a) * proj(w3_ref)).astype(BF16)
        y = jnp.dot(hid, w2_ref[0], preferred_element_type=F32)
        ys_ref[...] = _pack_bf16_pair(y[:, 0:half], y[:, half:D_MODEL])

    @pl.when(pl.program_id(0) >= nu_ref[0])
    def _():
        ys_ref[...] = jnp.zeros_like(ys_ref)


def _expert_call(tile_expert, n_used, xs, w1, w3, w2):
    n_rows = xs.shape[0]
    half = D_MODEL // 2
    w_spec = lambda shape: pl.BlockSpec((1,) + shape, lambda j, te, nu: (te[j], 0, 0))
    return pl.pallas_call(
        _expert_kernel,
        grid_spec=pltpu.PrefetchScalarGridSpec(
            num_scalar_prefetch=2,
            grid=(n_rows // EXPERT_ROWS,),
            in_specs=[pl.BlockSpec((EXPERT_ROWS, half), lambda j, te, nu: (j, 0)),
                      w_spec((D_MODEL, D_EXPERT)), w_spec((D_MODEL, D_EXPERT)),
                      w_spec((D_EXPERT, D_MODEL))],
            out_specs=pl.BlockSpec((EXPERT_ROWS, half), lambda j, te, nu: (j, 0))),
        out_shape=jax.ShapeDtypeStruct((n_rows, half), jnp.uint32),
        compiler_params=pltpu.CompilerParams(
            dimension_semantics=("arbitrary",), vmem_limit_bytes=VMEM_LIMIT),
        name="moe_experts",
    )(tile_expert, n_used, xs, w1, w3, w2)


def _combine_kernel(d1_ref, d2_ref, ys_hbm, x1_ref, route_ref, g2_ref, o_ref, buf_ref, sem):
    i = pl.program_id(0)
    n_steps = pl.num_programs(0)
    tc = COMBINE_TOKENS
    half = D_MODEL // 2

    def issue(step, slot):
        base = step * tc

        def one(t, carry):
            tok = base + t
            pltpu.make_async_copy(ys_hbm.at[pl.ds(d1_ref[tok], 1)],
                                  buf_ref.at[slot, pl.ds(t, 1)], sem.at[slot]).start()
            pltpu.make_async_copy(ys_hbm.at[pl.ds(d2_ref[tok], 1)],
                                  buf_ref.at[slot, pl.ds(tc + t, 1)], sem.at[slot]).start()
            return carry

        lax.fori_loop(0, tc, one, 0)

    slot = i % 2

    @pl.when(i == 0)
    def _():
        issue(0, 0)

    @pl.when(i + 1 < n_steps)
    def _():
        issue(i + 1, 1 - slot)

    pltpu.make_async_copy(ys_hbm.at[pl.ds(0, 2 * tc)], buf_ref.at[slot], sem.at[slot]).wait()

    rows = buf_ref[slot]
    a_lo, a_hi = _unpack_bf16_pair(rows[0:tc])
    b_lo, b_hi = _unpack_bf16_pair(rows[tc:2 * tc])
    route = route_ref[...]
    w_a = route[:, R_WEIGHT1:R_WEIGHT1 + 1]
    w_b = route[:, R_WEIGHT2:R_WEIGHT2 + 1]
    g2 = g2_ref[0]
    o_ref[:, 0:half] = x1_ref[:, 0:half] + g2[:, 0:half] * (w_a * a_lo + w_b * b_lo)
    o_ref[:, half:D_MODEL] = (x1_ref[:, half:D_MODEL]
                              + g2[:, half:D_MODEL] * (w_a * a_hi + w_b * b_hi))


def _combine_call(dest1, dest2, ys, x1, route, gate2, seq):
    tokens = x1.shape[0]
    tc = COMBINE_TOKENS
    per_seq = seq // tc
    half = D_MODEL // 2
    return pl.pallas_call(
        _combine_kernel,
        grid_spec=pltpu.PrefetchScalarGridSpec(
            num_scalar_prefetch=2,
            grid=(tokens // tc,),
            in_specs=[pl.BlockSpec(memory_space=pl.ANY),
                      pl.BlockSpec((tc, D_MODEL), lambda i, d1, d2: (i, 0)),
                      pl.BlockSpec((tc, ROUTER_LANES), lambda i, d1, d2: (i, 0)),
                      pl.BlockSpec((1, 1, D_MODEL), lambda i, d1, d2: (i // per_seq, 0, 0))],
            out_specs=pl.BlockSpec((tc, D_MODEL), lambda i, d1, d2: (i, 0)),
            scratch_shapes=[pltpu.VMEM((2, 2 * tc, half), jnp.uint32),
                            pltpu.SemaphoreType.DMA((2,))]),
        out_shape=jax.ShapeDtypeStruct((tokens, D_MODEL), F32),
        compiler_params=pltpu.CompilerParams(
            dimension_semantics=("arbitrary",), vmem_limit_bytes=VMEM_LIMIT),
        name="moe_combine",
    )(dest1, dest2, ys, x1, route, gate2)


def _moe(h2p, route, counts, x1, gate2, w1, w3, w2, seq):
    tokens = h2p.shape[0]
    n_rows = 2 * tokens + N_EXPERTS * EXPERT_ROWS
    n_rows -= n_rows % EXPERT_ROWS
    n_tiles = n_rows // EXPERT_ROWS

    count = counts[0, EXPERT_LANE0:EXPERT_LANE0 + N_EXPERTS].astype(jnp.int32)
    padded = (count + EXPERT_ROWS - 1) // EXPERT_ROWS * EXPERT_ROWS
    ends = jnp.cumsum(padded)
    starts = ends - padded
    tile_start = jnp.arange(n_tiles, dtype=jnp.int32) * EXPERT_ROWS
    tile_expert = jnp.minimum(jnp.sum(ends[None, :] <= tile_start[:, None], axis=1),
                              N_EXPERTS - 1).astype(jnp.int32)
    n_used = (ends[-1:] // EXPERT_ROWS).astype(jnp.int32)

    start_row = jnp.zeros((1, ROUTER_LANES), F32)
    start_row = start_row.at[0, EXPERT_LANE0:EXPERT_LANE0 + N_EXPERTS].set(starts.astype(F32))
    dest = _dest_call(route, start_row)[:, 0:2].astype(jnp.int32)
    dest1, dest2 = dest[:, 0], dest[:, 1]

    xs = _dispatch_call(dest1, dest2, h2p, n_rows)
    ys = _expert_call(tile_expert, n_used, xs, w1, w3, w2)
    return _combine_call(dest1, dest2, ys, x1, route, gate2, seq)


def _rope_tables(seq):
    half = HEAD // 2
    inv_freq = ROPE_THETA ** (-jnp.arange(half, dtype=F32) / half)
    ang = jnp.arange(seq, dtype=F32)[:, None] * inv_freq[None, :]
    cos = jnp.cos(ang)
    sin = jnp.sin(ang)
    cos_head = jnp.concatenate([cos, cos], axis=1)
    sin_head = jnp.concatenate([-sin, sin], axis=1)
    return jnp.tile(cos_head, (1, RWKV_HEADS)), jnp.tile(sin_head, (1, RWKV_HEADS))


def _layer(x, c, w_ada, b_ada, ln1_g, ln2_g, w_in, mu_shift, w0, w_lora_up, a0, a_lora_up,
           g_lora_up, k_k, k_a, r_k, lnx_g, lnx_b, q_norm_g, k_norm_g, w_br_rwkv, w_br_moba,
           w_out, w_rg, b_rg, w_re, b_re, w1, w3, w2):
    batch, seq, _ = x.shape
    vec = lambda a: a.reshape(1, -1)

    mod = _mod_call(c, w_ada, b_ada)
    shift1, scale1, gate1, shift2, scale2, gate2 = (
        m.reshape(batch, 1, D_MODEL) for m in jnp.split(mod, 6, axis=-1))

    idx = np.arange(MXU_DIM)
    ones_bd = jnp.asarray(idx[:, None] // HEAD == idx[None, :] // HEAD, BF16)
    t_idx = np.arange(RWKV_ROWS)
    tri = jnp.asarray((t_idx[:, None] >= t_idx[None, :])
                      & (t_idx[:, None] // CHUNK == t_idx[None, :] // CHUNK), BF16)
    cos, sin_signed = _rope_tables(seq)

    z_rwkv, q, k, v, gates, kmean = _inproj_call(
        x, shift1, scale1, vec(ln1_g), w_in.astype(BF16),
        vec(jnp.tile(q_norm_g, RWKV_HEADS)), vec(jnp.tile(k_norm_g, RWKV_HEADS)),
        cos, sin_signed, ones_bd)

    lora_cat = jnp.zeros((DECAY_LORA + AAA_LORA, 2 * WIDTH), F32)
    lora_cat = lora_cat.at[:DECAY_LORA, :WIDTH].set(w_lora_up).at[DECAY_LORA:, WIDTH:].set(a_lora_up)
    y_a = _rwkv_call(z_rwkv, vec(mu_shift), vec(w0), vec(a0), lora_cat, g_lora_up, vec(k_k),
                     vec(k_a), vec(r_k), vec(lnx_g), vec(lnx_b), ones_bd, tri)

    y_b = _moba_call(q, k, v, kmean.reshape(batch, seq // MOBA_BLOCK, WIDTH))

    tokens = batch * seq
    w_router = jnp.zeros((D_MODEL, ROUTER_LANES), F32)
    w_router = w_router.at[:, :N_GROUPS].set(w_rg).at[:, EXPERT_LANE0:EXPERT_LANE0 + N_EXPERTS].set(w_re)
    b_router = jnp.zeros((1, ROUTER_LANES), F32)
    b_router = b_router.at[0, :N_GROUPS].set(b_rg).at[0, EXPERT_LANE0:EXPERT_LANE0 + N_EXPERTS].set(b_re)
    m_idx = np.arange(MERGE_ROWS)
    tri_strict = jnp.asarray(m_idx[:, None] > m_idx[None, :], BF16)
    x1, h2p, route, counts = _merge_call(
        x.reshape(tokens, D_MODEL), y_a.reshape(tokens, WIDTH), y_b.reshape(tokens, WIDTH),
        gates.reshape(tokens, GATE_COLS), w_br_rwkv.astype(BF16), w_br_moba.astype(BF16),
        w_out.astype(BF16), gate1, vec(ln2_g), scale2, shift2, w_router, b_router, tri_strict,
        seq)

    out = _moe(h2p, route, counts, x1, gate2, w1.astype(BF16), w3.astype(BF16), w2.astype(BF16),
               seq)
    return out.reshape(batch, seq, D_MODEL)


def kernel(x, c, w_ada, b_ada, ln1_g, ln2_g, w_in, mu_shift, w0, w_lora_up, a0, a_lora_up,
           g_lora_up, k_k, k_a, r_k, lnx_g, lnx_b, q_norm_g, k_norm_g, w_br_rwkv, w_br_moba,
           w_out, w_rg, b_rg, w_re, b_re, w1, w3, w2):
    assert w_ada.shape[0] == 1, "single-layer problem"
    layer_params = (w_ada, b_ada, ln1_g, ln2_g, w_in, mu_shift, w0, w_lora_up, a0, a_lora_up,
                    g_lora_up, k_k, k_a, r_k, lnx_g, lnx_b, q_norm_g, k_norm_g, w_br_rwkv,
                    w_br_moba, w_out, w_rg, b_rg, w_re, b_re, w1, w3, w2)
    return _layer(x, c, *(p[0] for p in layer_params))
```

```python
import functools

import jax
import jax.numpy as jnp
import numpy as np
from jax import lax
from jax.experimental import pallas as pl
from jax.experimental.pallas import tpu as pltpu

F32 = jnp.float32
BF16 = jnp.bfloat16
HIGHEST = lax.Precision.HIGHEST

D_MODEL = 1024
RWKV_HEADS = 8
HEAD = 64
WIDTH = RWKV_HEADS * HEAD
DECAY_LORA = 64
AAA_LORA = 64
GATE_LORA = 128
RWKV_COLS = 3 * WIDTH + DECAY_LORA + AAA_LORA + GATE_LORA
ATT_COLS = 3 * WIDTH
GATE_COLS = 2 * D_MODEL
IN_COLS = RWKV_COLS + ATT_COLS + GATE_COLS
DECAY_SCALE = 0.606531
LN_X_EPS = 64e-5
MOBA_BLOCK = 256
MOBA_TOPK = 3
ROPE_THETA = 10000.0
N_GROUPS = 4
EXPERTS_PER_GROUP = 8
N_EXPERTS = N_GROUPS * EXPERTS_PER_GROUP
D_EXPERT = D_MODEL // 2
NORM_EPS = 1e-6
NEG_INF = -1e30

LANES = 128
MXU_DIM = 256
PAIRS = WIDTH // LANES
CHUNK = 64
SOLVE_BLOCK = 16
RWKV_ROWS = 2 * CHUNK
MOBA_SUB = 128
MOBA_PAIRS = 4
VMEM_LIMIT = 56 * 1024 * 1024

ROUTER_LANES = LANES
EXPERT_LANE0 = N_GROUPS


def _dot(a, b):
    return jnp.dot(a.astype(BF16), b.astype(BF16), preferred_element_type=F32)


def _dot_nt(a, b):
    return lax.dot_general(a.astype(BF16), b.astype(BF16), (((1,), (1,)), ((), ())),
                           preferred_element_type=F32)


def _dot_f32(a, b):
    return jnp.dot(a, b, precision=HIGHEST, preferred_element_type=F32)


def _split_bf16(a):
    hi = a.astype(BF16)
    return hi, (a - hi.astype(F32)).astype(BF16)


def _dot_x3(a, b):
    a_hi, a_lo = _split_bf16(a)
    b_hi, b_lo = _split_bf16(b)
    return (jnp.dot(a_hi, b_hi, preferred_element_type=F32)
            + jnp.dot(a_hi, b_lo, preferred_element_type=F32)
            + jnp.dot(a_lo, b_hi, preferred_element_type=F32))


def _dot_exact_lhs(a_bf16, b):
    hi, lo = _split_bf16(b)
    return (jnp.dot(a_bf16, hi, preferred_element_type=F32)
            + jnp.dot(a_bf16, lo, preferred_element_type=F32))


def _seg_sum(x, ones_bd):
    group = ones_bd.shape[0]
    hi, lo = _split_bf16(x)
    cols = []
    for c0 in range(0, x.shape[1], group):
        cols.append(jnp.dot(hi[:, c0:c0 + group], ones_bd, preferred_element_type=F32)
                    + jnp.dot(lo[:, c0:c0 + group], ones_bd, preferred_element_type=F32))
    return jnp.concatenate(cols, axis=1)


def _iota(shape, axis):
    return lax.broadcasted_iota(jnp.int32, shape, axis)


def _mod_kernel(c_ref, w_ref, b_ref, o_ref):
    c = c_ref[...]
    o_ref[...] = _dot_f32(c * jax.nn.sigmoid(c), w_ref[...]) + b_ref[...]


def _mod_call(c, w_ada, b_ada):
    batch = c.shape[0]
    n_out = w_ada.shape[1]
    tn = D_MODEL
    return pl.pallas_call(
        _mod_kernel,
        grid=(n_out // tn,),
        in_specs=[pl.BlockSpec((batch, D_MODEL), lambda j: (0, 0)),
                  pl.BlockSpec((D_MODEL, tn), lambda j: (0, j)),
                  pl.BlockSpec((1, tn), lambda j: (0, j))],
        out_specs=pl.BlockSpec((batch, tn), lambda j: (0, j)),
        out_shape=jax.ShapeDtypeStruct((batch, n_out), F32),
        name="adaln_mod",
    )(c, w_ada, b_ada.reshape(1, n_out))


def _swap_halves(x):
    first = (_iota(x.shape, 1) & (HEAD - 1)) < HEAD // 2
    up = pltpu.roll(x, LANES - HEAD // 2, axis=1)
    down = pltpu.roll(x, HEAD // 2, axis=1)
    return jnp.where(first, up, down)


def _head_norm_rope(x, gain, cos, sin_signed, ones_bd):
    ms = _seg_sum(x * x, ones_bd) * (1.0 / HEAD)
    y = x * lax.rsqrt(ms + NORM_EPS) * gain
    cols = []
    for p in range(PAIRS):
        sl = slice(p * LANES, (p + 1) * LANES)
        yb = y[:, sl]
        cols.append(yb * cos[:, sl] + _swap_halves(yb) * sin_signed[:, sl])
    return jnp.concatenate(cols, axis=1)


SEL_GROUP = 8


def _block_bias(q, km_ref, blk_idx):
    q_hi, q_lo = _split_bf16(q)
    km_hi, km_lo = _split_bf16(km_ref[...])
    nt = lambda a, b: lax.dot_general(a, b, (((1,), (1,)), ((), ())), preferred_element_type=F32)
    gate = nt(q_hi, km_hi) + nt(q_hi, km_lo) + nt(q_lo, km_hi)
    lane = _iota(gate.shape, 1)
    blk = lane & (SEL_GROUP - 1)
    valid = (blk < blk_idx) & (lane < RWKV_HEADS * SEL_GROUP)
    g = jnp.where(valid, gate, NEG_INF)
    rank = jnp.zeros(gate.shape, F32)
    for shift in range(1, SEL_GROUP):
        wrapped = blk + shift >= SEL_GROUP
        partner = jnp.where(wrapped, pltpu.roll(g, SEL_GROUP - shift, axis=1),
                            pltpu.roll(g, LANES - shift, axis=1))
        ahead = (partner > g) | ((partner == g) & wrapped)
        rank = rank + ahead.astype(F32)
    return jnp.where(valid & (rank < MOBA_TOPK), 0.0, NEG_INF)


def _inproj_kernel(x_ref, shift_ref, scale_ref, g_ref, w_ref, qg_ref, kg_ref, cos_ref, sin_ref,
                   ones_ref, zr_ref, q_ref, k_ref, v_ref, gate_ref, bias_ref, km_ref):
    i = pl.program_id(1)

    @pl.when(i == 0)
    def _():
        km_ref[...] = jnp.zeros_like(km_ref)

    x = x_ref[0]
    ms = jnp.mean(x * x, axis=-1, keepdims=True)
    h = x * lax.rsqrt(ms + NORM_EPS) * g_ref[...]
    h = (h * (1.0 + scale_ref[0]) + shift_ref[0]).astype(BF16)

    zr_ref[0] = jnp.dot(h, w_ref[:, 0:RWKV_COLS], preferred_element_type=F32)

    za = jnp.dot(h, w_ref[:, RWKV_COLS:RWKV_COLS + ATT_COLS], preferred_element_type=F32)
    ones_bd = ones_ref[...]
    cos = cos_ref[...]
    sin = sin_ref[...]
    q = _head_norm_rope(za[:, 0:WIDTH], qg_ref[...], cos, sin, ones_bd)
    k = _head_norm_rope(za[:, WIDTH:2 * WIDTH], kg_ref[...], cos, sin, ones_bd)
    q_ref[0] = (q * (HEAD ** -0.5)).astype(BF16)
    k_ref[0] = k.astype(BF16)
    v_ref[0] = za[:, 2 * WIDTH:3 * WIDTH].astype(BF16)

    bias_ref[0] = _block_bias(q, km_ref, i).astype(BF16)
    k_mean = jnp.mean(k, axis=0, keepdims=True)
    head_of_lane = _iota(k_mean.shape, 1) // HEAD
    for head in range(RWKV_HEADS):
        km_ref[pl.ds(head * SEL_GROUP + i, 1), :] = jnp.where(head_of_lane == head, k_mean, 0.0)

    zg = jnp.dot(h, w_ref[:, RWKV_COLS + ATT_COLS:IN_COLS], preferred_element_type=F32)
    gate_ref[0] = jax.nn.sigmoid(zg).astype(BF16)


def _inproj_call(x, shift1, scale1, ln1_g, w_in_bf16, q_gain, k_gain, cos, sin_signed, ones_bd):
    batch, seq, _ = x.shape
    tm = MOBA_BLOCK
    n_t = seq // tm
    row = lambda width: pl.BlockSpec((1, tm, width), lambda b, i: (b, i, 0))
    per_batch = pl.BlockSpec((1, 1, D_MODEL), lambda b, i: (b, 0, 0))
    const = lambda shape: pl.BlockSpec(shape, lambda b, i: (0,) * len(shape))
    return pl.pallas_call(
        _inproj_kernel,
        grid=(batch, n_t),
        in_specs=[row(D_MODEL), per_batch, per_batch, const((1, D_MODEL)),
                  const((D_MODEL, IN_COLS)), const((1, WIDTH)), const((1, WIDTH)),
                  pl.BlockSpec((tm, WIDTH), lambda b, i: (i, 0)),
                  pl.BlockSpec((tm, WIDTH), lambda b, i: (i, 0)),
                  const((MXU_DIM, MXU_DIM))],
        out_specs=[row(RWKV_COLS), row(WIDTH), row(WIDTH), row(WIDTH), row(GATE_COLS),
                   row(LANES)],
        out_shape=[jax.ShapeDtypeStruct((batch, seq, RWKV_COLS), F32),
                   jax.ShapeDtypeStruct((batch, seq, WIDTH), BF16),
                   jax.ShapeDtypeStruct((batch, seq, WIDTH), BF16),
                   jax.ShapeDtypeStruct((batch, seq, WIDTH), BF16),
                   jax.ShapeDtypeStruct((batch, seq, GATE_COLS), BF16),
                   jax.ShapeDtypeStruct((batch, seq, LANES), BF16)],
        scratch_shapes=[pltpu.VMEM((LANES, WIDTH), F32)],
        compiler_params=pltpu.CompilerParams(
            dimension_semantics=("arbitrary", "arbitrary"), vmem_limit_bytes=VMEM_LIMIT),
        name="inproj",
    )(x, shift1, scale1, ln1_g, w_in_bf16, q_gain, k_gain, cos, sin_signed, ones_bd)


def _stack_heads(x):
    first = _iota(x.shape, 1) < HEAD
    return jnp.concatenate([jnp.where(first, x, 0.0), jnp.where(first, 0.0, x)], axis=0)


def _add_eye(x, eye):
    return jnp.where(eye, x + 1.0, x)


def _rwkv_kernel(z_ref, mu_ref, w0_ref, a0_ref, lora_ref, glora_ref, kk_ref, ka_ref, rk_ref,
                 lng_ref, lnb_ref, ones_ref, tri_ref, o_ref, prev_ref, state_ref):
    c = pl.program_id(1)

    @pl.when(c == 0)
    def _():
        prev_ref[...] = jnp.zeros_like(prev_ref)
        state_ref[...] = jnp.zeros_like(state_ref)

    z = z_ref[0]
    rows = z.shape[0]
    n_chunks = rows // CHUNK
    row = _iota(z.shape, 0)
    z_prev = jnp.where(row == 0, prev_ref[...], pltpu.roll(z, 1, axis=0))
    prev_ref[...] = z[rows - 1:rows, :]
    zs = z + (z_prev - z) * mu_ref[...]

    r = zs[:, 0:WIDTH]
    k = zs[:, WIDTH:2 * WIDTH]
    v = zs[:, 2 * WIDTH:3 * WIDTH]
    lo = 3 * WIDTH
    d_wa = zs[:, lo:lo + DECAY_LORA + AAA_LORA]
    d_g = zs[:, lo + DECAY_LORA + AAA_LORA:RWKV_COLS]
    is_decay = _iota(d_wa.shape, 1) < DECAY_LORA
    pre = _dot_x3(jnp.where(is_decay, jnp.tanh(d_wa), d_wa), lora_ref[...])
    log_w = -DECAY_SCALE * jax.nn.sigmoid(w0_ref[...] + pre[:, 0:WIDTH])
    a = jax.nn.sigmoid(a0_ref[...] + pre[:, WIDTH:2 * WIDTH])
    g = _dot_x3(jax.nn.sigmoid(d_g), glora_ref[...])

    ones_bd = ones_ref[...]
    kk = k * kk_ref[...]
    kk = kk / jnp.maximum(jnp.sqrt(_seg_sum(kk * kk, ones_bd)), 1e-12)
    k = k * (1.0 + (a - 1.0) * ka_ref[...])
    bonus = _seg_sum(r * k * rk_ref[...], ones_bd) * v

    cl = _dot_exact_lhs(tri_ref[...], log_w)
    p_end_rows = [cl[(c_i + 1) * CHUNK - 1:(c_i + 1) * CHUNK, :] for c_i in range(n_chunks)]
    cl_last = p_end_rows[0]
    chunk_of_row = _iota(cl.shape, 0) // CHUNK
    for c_i in range(1, n_chunks):
        cl_last = jnp.where(chunk_of_row == c_i, p_end_rows[c_i], cl_last)
    a_t = -kk * jnp.exp(cl - log_w)
    e_neg = jnp.exp(-cl)
    b_t = kk * a * e_neg
    k_t = k * e_neg
    r_t = r * jnp.exp(cl)
    e_end = jnp.exp(cl_last - cl)
    b_end = kk * a * e_end
    k_end = k * e_end
    p_end = jnp.exp(cl_last)

    n2 = 2 * CHUNK
    ri = _iota((n2, n2), 0)
    ci = _iota((n2, n2), 1)
    eye = ri == ci
    same_blk = (ri // SOLVE_BLOCK) == (ci // SOLVE_BLOCK)
    ri4 = _iota((2 * n2, 2 * n2), 0)
    ci4 = _iota((2 * n2, 2 * n2), 1) & (n2 - 1)
    causal4 = ((ri4 < n2) & (ri4 > ci4)) | ((ri4 >= n2) & ((ri4 - n2) >= ci4))
    zeros = jnp.zeros((n2, n2), BF16)

    items = [(c_i, p) for c_i in range(n_chunks) for p in range(PAIRS)]

    def tile(t, item):
        c_i, p = item
        return t[c_i * CHUNK:(c_i + 1) * CHUNK, p * LANES:(p + 1) * LANES]

    def bf16_all(xs):
        return [x.astype(BF16) for x in xs]

    sa = [_stack_heads(tile(a_t, it)).astype(BF16) for it in items]
    sr = [_stack_heads(tile(r_t, it)) for it in items]
    sv = [_stack_heads(tile(v, it)).astype(BF16) for it in items]
    big = [jnp.where(causal4,
                     _dot_nt(jnp.concatenate([sa[i], sr[i].astype(BF16)], axis=0),
                             jnp.concatenate([_stack_heads(tile(b_t, it)).astype(BF16),
                                              _stack_heads(tile(k_t, it)).astype(BF16)],
                                             axis=0)), 0.0)
           for i, it in enumerate(items)]
    a_ab = [b[0:n2, 0:n2] for b in big]
    a_ak = bf16_all(b[0:n2, n2:2 * n2] for b in big)
    a_rbk = bf16_all(b[n2:2 * n2, :] for b in big)

    def pair_dot(xs, ys, prod=_dot):
        out = []
        for i in range(0, len(xs), 2):
            z = jnp.zeros_like(ys[i])
            wide = prod(jnp.concatenate([xs[i], xs[i + 1]], axis=1),
                        jnp.concatenate([jnp.concatenate([ys[i], z], axis=1),
                                         jnp.concatenate([z, ys[i + 1]], axis=1)], axis=0))
            out += [wide[:, 0:n2], wide[:, n2:2 * n2]]
        return out

    def plus_eye(xs):
        return [_add_eye(x, eye).astype(BF16) for x in xs]

    d1_f = [jnp.where(same_blk, a, 0.0) for a in a_ab]
    e1 = bf16_all(jnp.where(same_blk, 0.0, a) for a in a_ab)
    d1 = bf16_all(d1_f)
    d2_f = pair_dot(d1, d1)
    d2 = bf16_all(d2_f)
    d4_f = pair_dot(d2, d2)
    d4 = bf16_all(d4_f)
    d8_f = pair_dot(d4, d4)
    p12 = bf16_all(pair_dot(plus_eye(d1_f), plus_eye(d2_f)))
    p48 = bf16_all(pair_dot(plus_eye(d4_f), plus_eye(d8_f)))
    t_d = bf16_all(pair_dot(p12, p48))
    g1_f = pair_dot(t_d, e1)
    g1 = bf16_all(g1_f)
    g2_f = pair_dot(g1, g1)
    gx = bf16_all(pair_dot(plus_eye(g1_f), plus_eye(g2_f)))
    t_inv = bf16_all(pair_dot(gx, t_d))

    akv = bf16_all(pair_dot(a_ak, sv))
    wu = bf16_all(_dot(t, jnp.concatenate([s, x], axis=1)) for t, s, x in zip(t_inv, sa, akv))
    rhs = [jnp.concatenate([w, jnp.concatenate([zeros, s], axis=1)], axis=0)
           for w, s in zip(wu, sv)]
    out_c = [_dot(a, x) for a, x in zip(a_rbk, rhs)]
    end_t = [jnp.concatenate([_stack_heads(tile(b_end, it)).T,
                              _stack_heads(tile(k_end, it)).T], axis=1).astype(BF16)
             for it in items]
    end_c = [_dot(e, x) for e, x in zip(end_t, rhs)]

    y_rows = []
    for c_i in range(n_chunks):
        psi_phi = []
        for p in range(PAIRS):
            i = c_i * PAIRS + p
            psi = sr[i] + out_c[i][:, 0:n2]
            pe = p_end[c_i * CHUNK:c_i * CHUNK + 1, p * LANES:(p + 1) * LANES]
            phi = jnp.where(eye, pe, 0.0) + end_c[i][:, 0:n2]
            psi_phi.append(jnp.concatenate([psi, phi], axis=0))
        both = pair_dot(psi_phi, [state_ref[p] for p in range(PAIRS)], prod=_dot_x3)
        y_cols = []
        for p in range(PAIRS):
            i = c_i * PAIRS + p
            y = both[p][0:n2, :] + out_c[i][:, n2:2 * n2]
            state_ref[p] = both[p][n2:2 * n2, :] + end_c[i][:, n2:2 * n2]
            y_cols.append(y[0:CHUNK, :] + y[CHUNK:n2, :])
        y_rows.append(jnp.concatenate(y_cols, axis=1))
    y = jnp.concatenate(y_rows, axis=0)
    mean = _seg_sum(y, ones_bd) * (1.0 / HEAD)
    yc = y - mean
    var = _seg_sum(yc * yc, ones_bd) * (1.0 / HEAD)
    yn = yc * lax.rsqrt(var + LN_X_EPS) * lng_ref[...] + lnb_ref[...]
    o_ref[0] = ((yn + bonus) * g).astype(o_ref.dtype)


def _rwkv_call(z_rwkv, mu_shift, w0, a0, lora_cat, g_lora_up, k_k, k_a, r_k, lnx_g, lnx_b,
               ones_bd, tri):
    batch, seq, _ = z_rwkv.shape
    rows = RWKV_ROWS
    const = lambda shape: pl.BlockSpec(shape, lambda b, c: (0,) * len(shape))
    vec = const((1, WIDTH))
    return pl.pallas_call(
        _rwkv_kernel,
        grid=(batch, seq // rows),
        in_specs=[pl.BlockSpec((1, rows, RWKV_COLS), lambda b, c: (b, c, 0)),
                  const((1, RWKV_COLS)), vec, vec,
                  const((DECAY_LORA + AAA_LORA, 2 * WIDTH)), const((GATE_LORA, WIDTH)),
                  vec, vec, vec, vec, vec, const((MXU_DIM, MXU_DIM)), const((rows, rows))],
        out_specs=pl.BlockSpec((1, rows, WIDTH), lambda b, c: (b, c, 0)),
        out_shape=jax.ShapeDtypeStruct((batch, seq, WIDTH), BF16),
        scratch_shapes=[pltpu.VMEM((1, RWKV_COLS), F32),
                        pltpu.VMEM((PAIRS, LANES, LANES), F32)],
        compiler_params=pltpu.CompilerParams(
            dimension_semantics=("arbitrary", "arbitrary"), vmem_limit_bytes=VMEM_LIMIT),
        name="rwkv7",
    )(z_rwkv, mu_shift, w0, a0, lora_cat, g_lora_up, k_k, k_a, r_k, lnx_g, lnx_b, ones_bd, tri)


def _moba_kernel(q_ref, k_ref, v_ref, sel_ref, o_ref, m_ref, l_ref, acc_ref):
    qi = pl.program_id(2)
    tq = q_ref.shape[1]
    n_pairs = q_ref.shape[2] // LANES
    n_heads = 2 * n_pairs
    first = _iota((tq, LANES), 1) < HEAD
    pair_lanes = [slice(p * LANES, (p + 1) * LANES) for p in range(n_pairs)]
    qh = []
    for lanes in pair_lanes:
        q = q_ref[0, :, lanes]
        zero = jnp.zeros_like(q)
        qh += [jnp.where(first, q, zero), jnp.where(first, zero, q)]

    m_ref[...] = jnp.full(m_ref.shape, NEG_INF, F32)
    l_ref[...] = jnp.zeros(l_ref.shape, F32)
    acc_ref[...] = jnp.zeros(acc_ref.shape, F32)
    tiles = [(h, r0) for h in range(n_heads) for r0 in range(0, tq, MOBA_SUB)]
    q_tiles = [qh[h][r0:r0 + MOBA_SUB] for h, r0 in tiles]

    def update(scores, v_blks):
        n_keys = v_blks[0].shape[0]
        v_ext = [jnp.concatenate([v, jnp.ones((n_keys, LANES), BF16)], axis=1) for v in v_blks]
        refs = [(h, slice(r0, r0 + MOBA_SUB)) for h, r0 in tiles]
        m_old = [m_ref[h, rows, :] for h, rows in refs]
        l_old = [l_ref[h, rows, :] for h, rows in refs]
        acc_old = [acc_ref[h, rows, :] for h, rows in refs]
        m_new = [jnp.maximum(m, jnp.broadcast_to(jnp.max(s, axis=-1, keepdims=True),
                                                 (MOBA_SUB, LANES)))
                 for m, s in zip(m_old, scores)]
        alpha = [jnp.exp(m - n) for m, n in zip(m_old, m_new)]
        p = [jnp.exp(s - jnp.concatenate([n] * (n_keys // LANES), axis=1)).astype(BF16)
             for s, n in zip(scores, m_new)]
        pv = [jnp.dot(x, v_ext[h // 2], preferred_element_type=F32)
              for x, (h, _) in zip(p, tiles)]
        for (h, rows), m, a, l, acc, y in zip(refs, m_new, alpha, l_old, acc_old, pv):
            m_ref[h, rows, :] = m
            l_ref[h, rows, :] = a * l + y[:, LANES:2 * LANES]
            acc_ref[h, rows, :] = a * acc + y[:, 0:LANES]

    start = pl.multiple_of(qi * MOBA_BLOCK, MOBA_BLOCK)
    k_own = [k_ref[0, pl.ds(start, MOBA_BLOCK), lanes] for lanes in pair_lanes]
    v_own = [v_ref[0, pl.ds(start, MOBA_BLOCK), lanes] for lanes in pair_lanes]
    row = _iota((MOBA_SUB, MOBA_BLOCK), 0)
    col = _iota((MOBA_SUB, MOBA_BLOCK), 1)
    update([jnp.where(col <= row + r0, _dot_nt(qt, k_own[h // 2]), NEG_INF)
            for qt, (h, r0) in zip(q_tiles, tiles)], v_own)

    sel = sel_ref[0]
    sel_head = _iota(sel.shape, 1) // SEL_GROUP
    head0 = pl.program_id(1) * n_heads
    sel_h = [jnp.where(sel_head == head0 + h, sel, jnp.zeros_like(sel)) for h in range(n_heads)]
    q_ext = [jnp.concatenate([qt, sel_h[h][r0:r0 + MOBA_SUB]], axis=1)
             for qt, (h, r0) in zip(q_tiles, tiles)]
    key_blk = _iota((2 * MOBA_BLOCK, LANES), 0) // MOBA_BLOCK
    key_lane = _iota((2 * MOBA_BLOCK, LANES), 1)
    in_table = key_lane < RWKV_HEADS * SEL_GROUP

    def past_blocks(j, carry):
        start = pl.multiple_of(j * (2 * MOBA_BLOCK), 2 * MOBA_BLOCK)
        block_onehot = (in_table & ((key_lane & (SEL_GROUP - 1)) == 2 * j + key_blk)).astype(BF16)
        k_ext = [jnp.concatenate([k_ref[0, pl.ds(start, 2 * MOBA_BLOCK), lanes], block_onehot],
                                 axis=1) for lanes in pair_lanes]
        v_blks = [v_ref[0, pl.ds(start, 2 * MOBA_BLOCK), lanes] for lanes in pair_lanes]
        update([_dot_nt(qe, k_ext[h // 2]) for qe, (h, _) in zip(q_ext, tiles)], v_blks)
        return carry

    lax.fori_loop(0, (qi + 1) // 2, past_blocks, 0)

    for p, lanes in enumerate(pair_lanes):
        o_ref[0, :, lanes] = jnp.where(first, acc_ref[2 * p] / l_ref[2 * p],
                                       acc_ref[2 * p + 1] / l_ref[2 * p + 1]).astype(o_ref.dtype)


def _moba_call(q, k, v, sel_bias):
    batch, seq, _ = q.shape
    tq = MOBA_BLOCK
    width = MOBA_PAIRS * LANES
    n_heads = 2 * MOBA_PAIRS
    return pl.pallas_call(
        _moba_kernel,
        grid=(batch, PAIRS // MOBA_PAIRS, seq // tq),
        in_specs=[pl.BlockSpec((1, tq, width), lambda b, p, i: (b, i, p)),
                  pl.BlockSpec((1, seq, width), lambda b, p, i: (b, 0, p)),
                  pl.BlockSpec((1, seq, width), lambda b, p, i: (b, 0, p)),
                  pl.BlockSpec((1, tq, LANES), lambda b, p, i: (b, i, 0))],
        out_specs=pl.BlockSpec((1, tq, width), lambda b, p, i: (b, i, p)),
        out_shape=jax.ShapeDtypeStruct((batch, seq, WIDTH), BF16),
        scratch_shapes=[pltpu.VMEM((n_heads, tq, LANES), F32),
                        pltpu.VMEM((n_heads, tq, LANES), F32),
                        pltpu.VMEM((n_heads, tq, LANES), F32)],
        compiler_params=pltpu.CompilerParams(
            dimension_semantics=("arbitrary", "arbitrary", "arbitrary"),
            vmem_limit_bytes=VMEM_LIMIT),
        name="moba",
    )(q, k, v, sel_bias)


def _pack_bf16_pair(lo, hi):
    lo_bits = lax.bitcast_convert_type(lo.astype(BF16).astype(F32), jnp.uint32)
    hi_bits = lax.bitcast_convert_type(hi.astype(BF16).astype(F32), jnp.uint32)
    return (lo_bits >> 16) | (hi_bits & jnp.uint32(0xFFFF0000))


def _unpack_bf16_pair(u):
    lo = lax.bitcast_convert_type(u << 16, F32)
    hi = lax.bitcast_convert_type(u & jnp.uint32(0xFFFF0000), F32)
    return lo, hi


def _merge_kernel(x_ref, ya_ref, yb_ref, gate_ref, wa_ref, wb_ref, wo_ref, g1_ref, ln_ref,
                  scale_ref, shift_ref, wr_ref, br_ref, tri_ref, x1_ref, h2_ref, route_ref,
                  count_ref, carry_ref):
    @pl.when(pl.program_id(0) == 0)
    def _():
        carry_ref[...] = jnp.zeros_like(carry_ref)

    ya = jnp.dot(ya_ref[...], wa_ref[...], preferred_element_type=F32)
    yb = jnp.dot(yb_ref[...], wb_ref[...], preferred_element_type=F32)
    gates = gate_ref[...]
    merged = (gates[:, 0:D_MODEL].astype(F32) * ya + gates[:, D_MODEL:GATE_COLS].astype(F32) * yb)
    x1 = x_ref[...] + g1_ref[0] * jnp.dot(merged.astype(BF16), wo_ref[...],
                                          preferred_element_type=F32)
    x1_ref[...] = x1

    ms = jnp.mean(x1 * x1, axis=-1, keepdims=True)
    h2 = x1 * lax.rsqrt(ms + NORM_EPS) * ln_ref[...]
    h2 = h2 * (1.0 + scale_ref[0]) + shift_ref[0]
    half = D_MODEL // 2
    h2_ref[...] = _pack_bf16_pair(h2[:, 0:half], h2[:, half:D_MODEL])

    logits = _dot_x3(h2, wr_ref[...]) + br_ref[...]
    lane = _iota(logits.shape, 1)
    lane_f = lane.astype(F32)
    far = float(ROUTER_LANES)

    def top(vals):
        m = jnp.max(vals, axis=-1, keepdims=True)
        idx = jnp.min(jnp.where(vals == m, lane_f, far), axis=-1, keepdims=True)
        return m, idx

    grp = jnp.where(lane < N_GROUPS, logits, NEG_INF)
    g_max, g_idx = top(grp)
    p_group = 1.0 / jnp.sum(jnp.exp(grp - g_max), axis=-1, keepdims=True)

    e_lo = EXPERT_LANE0 + EXPERTS_PER_GROUP * g_idx
    in_grp = (lane_f >= e_lo) & (lane_f < e_lo + EXPERTS_PER_GROUP)
    el = jnp.where(in_grp, logits, NEG_INF)
    m1, i1 = top(el)
    m2, i2 = top(jnp.where(lane_f == i1, NEG_INF, el))
    ratio = jnp.exp(m2 - m1)
    w_first = p_group / (1.0 + ratio)
    w_second = w_first * ratio

    first = lane_f == i1
    second = lane_f == i2
    hits = (first | second).astype(BF16)
    before = carry_ref[...] + jnp.dot(tri_ref[...], hits, preferred_element_type=F32)
    rank1 = jnp.sum(jnp.where(first, before, 0.0), axis=-1, keepdims=True)
    rank2 = jnp.sum(jnp.where(second, before, 0.0), axis=-1, keepdims=True)
    carry = carry_ref[...] + jnp.sum(hits.astype(F32), axis=0, keepdims=True)
    carry_ref[...] = carry
    count_ref[...] = carry

    fields = (i1 - EXPERT_LANE0, i2 - EXPERT_LANE0, w_first, w_second, rank1, rank2)
    route = jnp.zeros(logits.shape, F32)
    for n, field in enumerate(fields):
        route = jnp.where(lane == n, field, route)
    route_ref[...] = route


R_EXPERT1, R_EXPERT2, R_WEIGHT1, R_WEIGHT2, R_RANK1, R_RANK2 = range(6)
MERGE_ROWS = 256


def _merge_call(x2d, ya, yb, gates, w_br_rwkv, w_br_moba, w_out, gate1, ln2_g, scale2, shift2,
                w_router, b_router, tri, seq):
    tokens = x2d.shape[0]
    tm = MERGE_ROWS
    per_seq = seq // tm
    row = lambda width: pl.BlockSpec((tm, width), lambda i: (i, 0))
    per_batch = pl.BlockSpec((1, 1, D_MODEL), lambda i: (i // per_seq, 0, 0))
    const = lambda shape: pl.BlockSpec(shape, lambda i: (0,) * len(shape))
    return pl.pallas_call(
        _merge_kernel,
        grid=(tokens // tm,),
        in_specs=[row(D_MODEL), row(WIDTH), row(WIDTH), row(GATE_COLS),
                  const((WIDTH, D_MODEL)), const((WIDTH, D_MODEL)), const((D_MODEL, D_MODEL)),
                  per_batch, const((1, D_MODEL)), per_batch, per_batch,
                  const((D_MODEL, ROUTER_LANES)), const((1, ROUTER_LANES)), const((tm, tm))],
        out_specs=[row(D_MODEL), row(D_MODEL // 2), row(ROUTER_LANES),
                   const((1, ROUTER_LANES))],
        out_shape=[jax.ShapeDtypeStruct((tokens, D_MODEL), F32),
                   jax.ShapeDtypeStruct((tokens, D_MODEL // 2), jnp.uint32),
                   jax.ShapeDtypeStruct((tokens, ROUTER_LANES), F32),
                   jax.ShapeDtypeStruct((1, ROUTER_LANES), F32)],
        scratch_shapes=[pltpu.VMEM((1, ROUTER_LANES), F32)],
        compiler_params=pltpu.CompilerParams(
            dimension_semantics=("arbitrary",), vmem_limit_bytes=VMEM_LIMIT),
        name="merge",
    )(x2d, ya, yb, gates, w_br_rwkv, w_br_moba, w_out, gate1, ln2_g, scale2, shift2,
      w_router, b_router, tri)


EXPERT_ROWS = 256
DISPATCH_TOKENS = 1024
COMBINE_TOKENS = 256


def _dest_kernel(route_ref, start_ref, o_ref):
    route = route_ref[...]
    lane = _iota(route.shape, 1)
    lane_f = lane.astype(F32)
    starts = start_ref[...]
    out = jnp.zeros(route.shape, F32)
    for n, (e_lane, r_lane) in enumerate(((R_EXPERT1, R_RANK1), (R_EXPERT2, R_RANK2))):
        e_col = route[:, e_lane:e_lane + 1] + EXPERT_LANE0
        first_row = jnp.sum(jnp.where(lane_f == e_col, starts, 0.0), axis=-1, keepdims=True)
        out = jnp.where(lane == n, first_row + route[:, r_lane:r_lane + 1], out)
    o_ref[...] = out


def _dest_call(route, start_row):
    tokens = route.shape[0]
    tm = DISPATCH_TOKENS
    return pl.pallas_call(
        _dest_kernel,
        grid=(tokens // tm,),
        in_specs=[pl.BlockSpec((tm, ROUTER_LANES), lambda i: (i, 0)),
                  pl.BlockSpec((1, ROUTER_LANES), lambda i: (0, 0))],
        out_specs=pl.BlockSpec((tm, ROUTER_LANES), lambda i: (i, 0)),
        out_shape=jax.ShapeDtypeStruct((tokens, ROUTER_LANES), F32),
        name="moe_dest",
    )(route, start_row)


def _dispatch_kernel(d1_ref, d2_ref, h_ref, xs_init_hbm, xs_hbm, sem):
    del xs_init_hbm
    base = pl.program_id(0) * DISPATCH_TOKENS

    def issue(t, carry):
        src = h_ref.at[pl.ds(t, 1)]
        pltpu.make_async_copy(src, xs_hbm.at[pl.ds(d1_ref[base + t], 1)], sem).start()
        pltpu.make_async_copy(src, xs_hbm.at[pl.ds(d2_ref[base + t], 1)], sem).start()
        return carry

    lax.fori_loop(0, DISPATCH_TOKENS, issue, 0, unroll=8)
    for _ in range(2):
        pltpu.make_async_copy(h_ref, xs_hbm.at[pl.ds(0, DISPATCH_TOKENS)], sem).wait()


def _dispatch_call(dest1, dest2, h2p, n_rows):
    tokens = h2p.shape[0]
    any_spec = pl.BlockSpec(memory_space=pl.ANY)
    return pl.pallas_call(
        _dispatch_kernel,
        grid_spec=pltpu.PrefetchScalarGridSpec(
            num_scalar_prefetch=2,
            grid=(tokens // DISPATCH_TOKENS,),
            in_specs=[pl.BlockSpec((DISPATCH_TOKENS, D_MODEL // 2), lambda i, d1, d2: (i, 0)),
                      any_spec],
            out_specs=any_spec,
            scratch_shapes=[pltpu.SemaphoreType.DMA(())]),
        out_shape=jax.ShapeDtypeStruct((n_rows, D_MODEL // 2), jnp.uint32),
        input_output_aliases={3: 0},
        compiler_params=pltpu.CompilerParams(dimension_semantics=("arbitrary",)),
        name="moe_dispatch",
    )(dest1, dest2, h2p, jnp.zeros((n_rows, D_MODEL // 2), jnp.uint32))


def _expert_kernel(te_ref, nu_ref, xs_ref, w1_ref, w3_ref, w2_ref, ys_ref):
    half = D_MODEL // 2

    @pl.when(pl.program_id(0) < nu_ref[0])
    def _():
        x_lo, x_hi = _unpack_bf16_pair(xs_ref[...])
        x_lo = x_lo.astype(BF16)
        x_hi = x_hi.astype(BF16)

        def proj(w_ref):
            return (jnp.dot(x_lo, w_ref[0, 0:half, :], preferred_element_type=F32)
                    + jnp.dot(x_hi, w_ref[0, half:D_MODEL, :], preferred_element_type=F32))

        a = proj(w1_ref)
        hid = (a * jax.nn.sigmoid(a) * proj(w3_ref)).astype(BF16)
        y = jnp.dot(hid, w2_ref[0], preferred_element_type=F32)
        ys_ref[...] = _pack_bf16_pair(y[:, 0:half], y[:, half:D_MODEL])

    @pl.when(pl.program_id(0) >= nu_ref[0])
    def _():
        ys_ref[...] = jnp.zeros_like(ys_ref)


def _expert_call(tile_expert, n_used, xs, w1, w3, w2):
    n_rows = xs.shape[0]
    half = D_MODEL // 2
    w_spec = lambda shape: pl.BlockSpec((1,) + shape, lambda j, te, nu: (te[j], 0, 0))
    return pl.pallas_call(
        _expert_kernel,
        grid_spec=pltpu.PrefetchScalarGridSpec(
            num_scalar_prefetch=2,
            grid=(n_rows // EXPERT_ROWS,),
            in_specs=[pl.BlockSpec((EXPERT_ROWS, half), lambda j, te, nu: (j, 0)),
                      w_spec((D_MODEL, D_EXPERT)), w_spec((D_MODEL, D_EXPERT)),
                      w_spec((D_EXPERT, D_MODEL))],
            out_specs=pl.BlockSpec((EXPERT_ROWS, half), lambda j, te, nu: (j, 0))),
        out_shape=jax.ShapeDtypeStruct((n_rows, half), jnp.uint32),
        compiler_params=pltpu.CompilerParams(
            dimension_semantics=("arbitrary",), vmem_limit_bytes=VMEM_LIMIT),
        name="moe_experts",
    )(tile_expert, n_used, xs, w1, w3, w2)


def _combine_kernel(d1_ref, d2_ref, ys_hbm, x1_ref, route_ref, g2_ref, o_ref, buf_ref, sem):
    i = pl.program_id(0)
    n_steps = pl.num_programs(0)
    tc = COMBINE_TOKENS
    half = D_MODEL // 2

    def issue(step, slot):
        base = step * tc

        def one(t, carry):
            tok = base + t
            pltpu.make_async_copy(ys_hbm.at[pl.ds(d1_ref[tok], 1)],
                                  buf_ref.at[slot, pl.ds(t, 1)], sem.at[slot]).start()
            pltpu.make_async_copy(ys_hbm.at[pl.ds(d2_ref[tok], 1)],
                                  buf_ref.at[slot, pl.ds(tc + t, 1)], sem.at[slot]).start()
            return carry

        lax.fori_loop(0, tc, one, 0, unroll=8)

    slot = i % 2

    @pl.when(i == 0)
    def _():
        issue(0, 0)

    @pl.when(i + 1 < n_steps)
    def _():
        issue(i + 1, 1 - slot)

    pltpu.make_async_copy(ys_hbm.at[pl.ds(0, 2 * tc)], buf_ref.at[slot], sem.at[slot]).wait()

    rows = buf_ref[slot]
    a_lo, a_hi = _unpack_bf16_pair(rows[0:tc])
    b_lo, b_hi = _unpack_bf16_pair(rows[tc:2 * tc])
    route = route_ref[...]
    w_a = route[:, R_WEIGHT1:R_WEIGHT1 + 1]
    w_b = route[:, R_WEIGHT2:R_WEIGHT2 + 1]
    g2 = g2_ref[0]
    o_ref[:, 0:half] = x1_ref[:, 0:half] + g2[:, 0:half] * (w_a * a_lo + w_b * b_lo)
    o_ref[:, half:D_MODEL] = (x1_ref[:, half:D_MODEL]
                              + g2[:, half:D_MODEL] * (w_a * a_hi + w_b * b_hi))


def _combine_call(dest1, dest2, ys, x1, route, gate2, seq):
    tokens = x1.shape[0]
    tc = COMBINE_TOKENS
    per_seq = seq // tc
    half = D_MODEL // 2
    return pl.pallas_call(
        _combine_kernel,
        grid_spec=pltpu.PrefetchScalarGridSpec(
            num_scalar_prefetch=2,
            grid=(tokens // tc,),
            in_specs=[pl.BlockSpec(memory_space=pl.ANY),
                      pl.BlockSpec((tc, D_MODEL), lambda i, d1, d2: (i, 0)),
                      pl.BlockSpec((tc, ROUTER_LANES), lambda i, d1, d2: (i, 0)),
                      pl.BlockSpec((1, 1, D_MODEL), lambda i, d1, d2: (i // per_seq, 0, 0))],
            out_specs=pl.BlockSpec((tc, D_MODEL), lambda i, d1, d2: (i, 0)),
            scratch_shapes=[pltpu.VMEM((2, 2 * tc, half), jnp.uint32),
                            pltpu.SemaphoreType.DMA((2,))]),
        out_shape=jax.ShapeDtypeStruct((tokens, D_MODEL), F32),
        compiler_params=pltpu.CompilerParams(
            dimension_semantics=("arbitrary",), vmem_limit_bytes=VMEM_LIMIT),
        name="moe_combine",
    )(dest1, dest2, ys, x1, route, gate2)


def _moe(h2p, route, counts, x1, gate2, w1, w3, w2, seq):
    tokens = h2p.shape[0]
    n_rows = 2 * tokens + N_EXPERTS * EXPERT_ROWS
    n_rows -= n_rows % EXPERT_ROWS
    n_tiles = n_rows // EXPERT_ROWS

    count = counts[0, EXPERT_LANE0:EXPERT_LANE0 + N_EXPERTS].astype(jnp.int32)
    padded = (count + EXPERT_ROWS - 1) // EXPERT_ROWS * EXPERT_ROWS
    ends = jnp.cumsum(padded)
    starts = ends - padded
    tile_start = jnp.arange(n_tiles, dtype=jnp.int32) * EXPERT_ROWS
    tile_expert = jnp.minimum(jnp.sum(ends[None, :] <= tile_start[:, None], axis=1),
                              N_EXPERTS - 1).astype(jnp.int32)
    n_used = (ends[-1:] // EXPERT_ROWS).astype(jnp.int32)

    start_row = jnp.zeros((1, ROUTER_LANES), F32)
    start_row = start_row.at[0, EXPERT_LANE0:EXPERT_LANE0 + N_EXPERTS].set(starts.astype(F32))
    dest = _dest_call(route, start_row)[:, 0:2].astype(jnp.int32)
    dest1, dest2 = dest[:, 0], dest[:, 1]

    xs = _dispatch_call(dest1, dest2, h2p, n_rows)
    ys = _expert_call(tile_expert, n_used, xs, w1, w3, w2)
    return _combine_call(dest1, dest2, ys, x1, route, gate2, seq)


def _rope_tables(seq):
    half = HEAD // 2
    inv_freq = ROPE_THETA ** (-jnp.arange(half, dtype=F32) / half)
    ang = jnp.arange(seq, dtype=F32)[:, None] * inv_freq[None, :]
    cos = jnp.cos(ang)
    sin = jnp.sin(ang)
    cos_head = jnp.concatenate([cos, cos], axis=1)
    sin_head = jnp.concatenate([-sin, sin], axis=1)
    return jnp.tile(cos_head, (1, RWKV_HEADS)), jnp.tile(sin_head, (1, RWKV_HEADS))


def _layer(x, c, w_ada, b_ada, ln1_g, ln2_g, w_in, mu_shift, w0, w_lora_up, a0, a_lora_up,
           g_lora_up, k_k, k_a, r_k, lnx_g, lnx_b, q_norm_g, k_norm_g, w_br_rwkv, w_br_moba,
           w_out, w_rg, b_rg, w_re, b_re, w1, w3, w2):
    batch, seq, _ = x.shape
    vec = lambda a: a.reshape(1, -1)

    mod = _mod_call(c, w_ada, b_ada)
    shift1, scale1, gate1, shift2, scale2, gate2 = (
        m.reshape(batch, 1, D_MODEL) for m in jnp.split(mod, 6, axis=-1))

    idx = np.arange(MXU_DIM)
    ones_bd = jnp.asarray(idx[:, None] // HEAD == idx[None, :] // HEAD, BF16)
    t_idx = np.arange(RWKV_ROWS)
    tri = jnp.asarray((t_idx[:, None] >= t_idx[None, :])
                      & (t_idx[:, None] // CHUNK == t_idx[None, :] // CHUNK), BF16)
    cos, sin_signed = _rope_tables(seq)

    z_rwkv, q, k, v, gates, sel_bias = _inproj_call(
        x, shift1, scale1, vec(ln1_g), w_in.astype(BF16),
        vec(jnp.tile(q_norm_g, RWKV_HEADS)), vec(jnp.tile(k_norm_g, RWKV_HEADS)),
        cos, sin_signed, ones_bd)

    lora_cat = jnp.zeros((DECAY_LORA + AAA_LORA, 2 * WIDTH), F32)
    lora_cat = lora_cat.at[:DECAY_LORA, :WIDTH].set(w_lora_up).at[DECAY_LORA:, WIDTH:].set(a_lora_up)
    y_a = _rwkv_call(z_rwkv, vec(mu_shift), vec(w0), vec(a0), lora_cat, g_lora_up, vec(k_k),
                     vec(k_a), vec(r_k), vec(lnx_g), vec(lnx_b), ones_bd, tri)

    y_b = _moba_call(q, k, v, sel_bias)

    tokens = batch * seq
    w_router = jnp.zeros((D_MODEL, ROUTER_LANES), F32)
    w_router = w_router.at[:, :N_GROUPS].set(w_rg).at[:, EXPERT_LANE0:EXPERT_LANE0 + N_EXPERTS].set(w_re)
    b_router = jnp.zeros((1, ROUTER_LANES), F32)
    b_router = b_router.at[0, :N_GROUPS].set(b_rg).at[0, EXPERT_LANE0:EXPERT_LANE0 + N_EXPERTS].set(b_re)
    m_idx = np.arange(MERGE_ROWS)
    tri_strict = jnp.asarray(m_idx[:, None] > m_idx[None, :], BF16)
    x1, h2p, route, counts = _merge_call(
        x.reshape(tokens, D_MODEL), y_a.reshape(tokens, WIDTH), y_b.reshape(tokens, WIDTH),
        gates.reshape(tokens, GATE_COLS), w_br_rwkv.astype(BF16), w_br_moba.astype(BF16),
        w_out.astype(BF16), gate1, vec(ln2_g), scale2, shift2, w_router, b_router, tri_strict,
        seq)

    out = _moe(h2p, route, counts, x1, gate2, w1.astype(BF16), w3.astype(BF16), w2.astype(BF16),
               seq)
    return out.reshape(batch, seq, D_MODEL)


def kernel(x, c, w_ada, b_ada, ln1_g, ln2_g, w_in, mu_shift, w0, w_lora_up, a0, a_lora_up,
           g_lora_up, k_k, k_a, r_k, lnx_g, lnx_b, q_norm_g, k_norm_g, w_br_rwkv, w_br_moba,
           w_out, w_rg, b_rg, w_re, b_re, w1, w3, w2):
    assert w_ada.shape[0] == 1, "single-layer problem"
    layer_params = (w_ada, b_ada, ln1_g, ln2_g, w_in, mu_shift, w0, w_lora_up, a0, a_lora_up,
                    g_lora_up, k_k, k_a, r_k, lnx_g, lnx_b, q_norm_g, k_norm_g, w_br_rwkv,
                    w_br_moba, w_out, w_rg, b_rg, w_re, b_re, w1, w3, w2)
    return _layer(x, c, *(p[0] for p in layer_params))
```

```python
import functools

import jax
import jax.numpy as jnp
import numpy as np
from jax import lax
from jax.experimental import pallas as pl
from jax.experimental.pallas import tpu as pltpu

F32 = jnp.float32
BF16 = jnp.bfloat16
HIGHEST = lax.Precision.HIGHEST

D_MODEL = 1024
RWKV_HEADS = 8
HEAD = 64
WIDTH = RWKV_HEADS * HEAD
DECAY_LORA = 64
AAA_LORA = 64
GATE_LORA = 128
RWKV_COLS = 3 * WIDTH + DECAY_LORA + AAA_LORA + GATE_LORA
ATT_COLS = 3 * WIDTH
GATE_COLS = 2 * D_MODEL
IN_COLS = RWKV_COLS + ATT_COLS + GATE_COLS
DECAY_SCALE = 0.606531
LN_X_EPS = 64e-5
MOBA_BLOCK = 256
MOBA_TOPK = 3
ROPE_THETA = 10000.0
N_GROUPS = 4
EXPERTS_PER_GROUP = 8
N_EXPERTS = N_GROUPS * EXPERTS_PER_GROUP
D_EXPERT = D_MODEL // 2
NORM_EPS = 1e-6
NEG_INF = -1e30

LANES = 128
MXU_DIM = 256
PAIRS = WIDTH // LANES
CHUNK = 64
SOLVE_BLOCK = 16
RWKV_ROWS = 4 * CHUNK
MOBA_SUB = 128
MOBA_PAIRS = 4
VMEM_LIMIT = 56 * 1024 * 1024

ROUTER_LANES = LANES
EXPERT_LANE0 = N_GROUPS


def _dot(a, b):
    return jnp.dot(a.astype(BF16), b.astype(BF16), preferred_element_type=F32)


def _dot_nt(a, b):
    return lax.dot_general(a.astype(BF16), b.astype(BF16), (((1,), (1,)), ((), ())),
                           preferred_element_type=F32)


def _dot_f32(a, b):
    return jnp.dot(a, b, precision=HIGHEST, preferred_element_type=F32)


def _split_bf16(a):
    hi = a.astype(BF16)
    return hi, (a - hi.astype(F32)).astype(BF16)


def _dot_x3(a, b):
    a_hi, a_lo = _split_bf16(a)
    b_hi, b_lo = _split_bf16(b)
    return (jnp.dot(a_hi, b_hi, preferred_element_type=F32)
            + jnp.dot(a_hi, b_lo, preferred_element_type=F32)
            + jnp.dot(a_lo, b_hi, preferred_element_type=F32))


def _dot_wide_rhs(a, b):
    a = a.astype(BF16)
    hi, lo = _split_bf16(b)
    return (jnp.dot(a, hi, preferred_element_type=F32)
            + jnp.dot(a, lo, preferred_element_type=F32))


def _seg_sum(x, ones_bd):
    group = ones_bd.shape[0]
    hi, lo = _split_bf16(x)
    cols = []
    for c0 in range(0, x.shape[1], group):
        cols.append(jnp.dot(hi[:, c0:c0 + group], ones_bd, preferred_element_type=F32)
                    + jnp.dot(lo[:, c0:c0 + group], ones_bd, preferred_element_type=F32))
    return jnp.concatenate(cols, axis=1)


def _iota(shape, axis):
    return lax.broadcasted_iota(jnp.int32, shape, axis)


def _mod_kernel(c_ref, w_ref, b_ref, o_ref):
    c = c_ref[...]
    o_ref[...] = _dot_f32(c * jax.nn.sigmoid(c), w_ref[...]) + b_ref[...]


def _mod_call(c, w_ada, b_ada):
    batch = c.shape[0]
    n_out = w_ada.shape[1]
    tn = D_MODEL
    return pl.pallas_call(
        _mod_kernel,
        grid=(n_out // tn,),
        in_specs=[pl.BlockSpec((batch, D_MODEL), lambda j: (0, 0)),
                  pl.BlockSpec((D_MODEL, tn), lambda j: (0, j)),
                  pl.BlockSpec((1, tn), lambda j: (0, j))],
        out_specs=pl.BlockSpec((batch, tn), lambda j: (0, j)),
        out_shape=jax.ShapeDtypeStruct((batch, n_out), F32),
        name="adaln_mod",
    )(c, w_ada, b_ada.reshape(1, n_out))


def _swap_halves(x):
    first = (_iota(x.shape, 1) & (HEAD - 1)) < HEAD // 2
    up = pltpu.roll(x, LANES - HEAD // 2, axis=1)
    down = pltpu.roll(x, HEAD // 2, axis=1)
    return jnp.where(first, up, down)


def _head_norm_rope(x, gain, cos, sin_signed, ones_bd):
    ms = _seg_sum(x * x, ones_bd) * (1.0 / HEAD)
    y = x * lax.rsqrt(ms + NORM_EPS) * gain
    cols = []
    for p in range(PAIRS):
        sl = slice(p * LANES, (p + 1) * LANES)
        yb = y[:, sl]
        cols.append(yb * cos[:, sl] + _swap_halves(yb) * sin_signed[:, sl])
    return jnp.concatenate(cols, axis=1)


SEL_GROUP = 8


def _block_bias(q, km_ref, blk_idx):
    q_hi, q_lo = _split_bf16(q)
    km_hi, km_lo = _split_bf16(km_ref[...])
    nt = lambda a, b: lax.dot_general(a, b, (((1,), (1,)), ((), ())), preferred_element_type=F32)
    gate = nt(q_hi, km_hi) + nt(q_hi, km_lo) + nt(q_lo, km_hi)
    lane = _iota(gate.shape, 1)
    blk = lane & (SEL_GROUP - 1)
    valid = (blk < blk_idx) & (lane < RWKV_HEADS * SEL_GROUP)
    g = jnp.where(valid, gate, NEG_INF)
    rank = jnp.zeros(gate.shape, F32)
    for shift in range(1, SEL_GROUP):
        wrapped = blk + shift >= SEL_GROUP
        partner = jnp.where(wrapped, pltpu.roll(g, SEL_GROUP - shift, axis=1),
                            pltpu.roll(g, LANES - shift, axis=1))
        ahead = (partner > g) | ((partner == g) & wrapped)
        rank = rank + ahead.astype(F32)
    return jnp.where(valid & (rank < MOBA_TOPK), 0.0, NEG_INF)


def _inproj_kernel(x_ref, shift_ref, scale_ref, g_ref, w_ref, qg_ref, kg_ref, cos_ref, sin_ref,
                   ones_ref, zr_ref, q_ref, k_ref, v_ref, gate_ref, bias_ref, km_ref):
    i = pl.program_id(1)

    @pl.when(i == 0)
    def _():
        km_ref[...] = jnp.zeros_like(km_ref)

    x = x_ref[0]
    ms = jnp.mean(x * x, axis=-1, keepdims=True)
    h = x * lax.rsqrt(ms + NORM_EPS) * g_ref[...]
    h = (h * (1.0 + scale_ref[0]) + shift_ref[0]).astype(BF16)

    zr_ref[0] = jnp.dot(h, w_ref[:, 0:RWKV_COLS], preferred_element_type=F32)

    za = jnp.dot(h, w_ref[:, RWKV_COLS:RWKV_COLS + ATT_COLS], preferred_element_type=F32)
    ones_bd = ones_ref[...]
    cos = cos_ref[...]
    sin = sin_ref[...]
    q = _head_norm_rope(za[:, 0:WIDTH], qg_ref[...], cos, sin, ones_bd)
    k = _head_norm_rope(za[:, WIDTH:2 * WIDTH], kg_ref[...], cos, sin, ones_bd)
    q_ref[0] = (q * (HEAD ** -0.5)).astype(BF16)
    k_ref[0] = k.astype(BF16)
    v_ref[0] = za[:, 2 * WIDTH:3 * WIDTH].astype(BF16)

    bias_ref[0] = _block_bias(q, km_ref, i).astype(BF16)
    k_mean = jnp.mean(k, axis=0, keepdims=True)
    head_of_lane = _iota(k_mean.shape, 1) // HEAD
    for head in range(RWKV_HEADS):
        km_ref[pl.ds(head * SEL_GROUP + i, 1), :] = jnp.where(head_of_lane == head, k_mean, 0.0)

    zg = jnp.dot(h, w_ref[:, RWKV_COLS + ATT_COLS:IN_COLS], preferred_element_type=F32)
    gate_ref[0] = jax.nn.sigmoid(zg).astype(BF16)


def _inproj_call(x, shift1, scale1, ln1_g, w_in_bf16, q_gain, k_gain, cos, sin_signed, ones_bd):
    batch, seq, _ = x.shape
    tm = MOBA_BLOCK
    n_t = seq // tm
    row = lambda width: pl.BlockSpec((1, tm, width), lambda b, i: (b, i, 0))
    per_batch = pl.BlockSpec((1, 1, D_MODEL), lambda b, i: (b, 0, 0))
    const = lambda shape: pl.BlockSpec(shape, lambda b, i: (0,) * len(shape))
    return pl.pallas_call(
        _inproj_kernel,
        grid=(batch, n_t),
        in_specs=[row(D_MODEL), per_batch, per_batch, const((1, D_MODEL)),
                  const((D_MODEL, IN_COLS)), const((1, WIDTH)), const((1, WIDTH)),
                  pl.BlockSpec((tm, WIDTH), lambda b, i: (i, 0)),
                  pl.BlockSpec((tm, WIDTH), lambda b, i: (i, 0)),
                  const((MXU_DIM, MXU_DIM))],
        out_specs=[row(RWKV_COLS), row(WIDTH), row(WIDTH), row(WIDTH), row(GATE_COLS),
                   row(LANES)],
        out_shape=[jax.ShapeDtypeStruct((batch, seq, RWKV_COLS), F32),
                   jax.ShapeDtypeStruct((batch, seq, WIDTH), BF16),
                   jax.ShapeDtypeStruct((batch, seq, WIDTH), BF16),
                   jax.ShapeDtypeStruct((batch, seq, WIDTH), BF16),
                   jax.ShapeDtypeStruct((batch, seq, GATE_COLS), BF16),
                   jax.ShapeDtypeStruct((batch, seq, LANES), BF16)],
        scratch_shapes=[pltpu.VMEM((LANES, WIDTH), F32)],
        compiler_params=pltpu.CompilerParams(
            dimension_semantics=("arbitrary", "arbitrary"), vmem_limit_bytes=VMEM_LIMIT),
        name="inproj",
    )(x, shift1, scale1, ln1_g, w_in_bf16, q_gain, k_gain, cos, sin_signed, ones_bd)


def _stack_heads(x):
    first = _iota(x.shape, 1) < HEAD
    return jnp.concatenate([jnp.where(first, x, 0.0), jnp.where(first, 0.0, x)], axis=0)


def _add_eye(x, eye):
    return jnp.where(eye, x + 1.0, x)


def _rwkv_kernel(z_ref, mu_ref, w0_ref, a0_ref, lora_ref, glora_ref, kk_ref, ka_ref, rk_ref,
                 lng_ref, lnb_ref, ones_ref, tri_ref, o_ref, prev_ref, state_ref):
    c = pl.program_id(1)

    @pl.when(c == 0)
    def _():
        prev_ref[...] = jnp.zeros_like(prev_ref)
        state_ref[...] = jnp.zeros_like(state_ref)

    z = z_ref[0]
    rows = z.shape[0]
    n_chunks = rows // CHUNK
    row = _iota(z.shape, 0)
    z_prev = jnp.where(row == 0, prev_ref[...], pltpu.roll(z, 1, axis=0))
    prev_ref[...] = z[rows - 1:rows, :]
    zs = z + (z_prev - z) * mu_ref[...]

    r = zs[:, 0:WIDTH]
    k = zs[:, WIDTH:2 * WIDTH]
    v = zs[:, 2 * WIDTH:3 * WIDTH]
    lo = 3 * WIDTH
    d_wa = zs[:, lo:lo + DECAY_LORA + AAA_LORA]
    d_g = zs[:, lo + DECAY_LORA + AAA_LORA:RWKV_COLS]
    is_decay = _iota(d_wa.shape, 1) < DECAY_LORA
    pre = _dot_x3(jnp.where(is_decay, jnp.tanh(d_wa), d_wa), lora_ref[...])
    log_w = -DECAY_SCALE * jax.nn.sigmoid(w0_ref[...] + pre[:, 0:WIDTH])
    a = jax.nn.sigmoid(a0_ref[...] + pre[:, WIDTH:2 * WIDTH])
    g = _dot(jax.nn.sigmoid(d_g), glora_ref[...])

    ones_bd = ones_ref[...]
    kk = k * kk_ref[...]
    kk = kk / jnp.maximum(jnp.sqrt(_seg_sum(kk * kk, ones_bd)), 1e-12)
    k = k * (1.0 + (a - 1.0) * ka_ref[...])
    bonus = _seg_sum(r * k * rk_ref[...], ones_bd) * v

    cl = _dot_wide_rhs(tri_ref[...], log_w)
    p_end_rows = [cl[(c_i + 1) * CHUNK - 1:(c_i + 1) * CHUNK, :] for c_i in range(n_chunks)]
    cl_last = p_end_rows[0]
    chunk_of_row = _iota(cl.shape, 0) // CHUNK
    for c_i in range(1, n_chunks):
        cl_last = jnp.where(chunk_of_row == c_i, p_end_rows[c_i], cl_last)
    a_t = -kk * jnp.exp(cl - log_w)
    e_neg = jnp.exp(-cl)
    b_t = kk * a * e_neg
    k_t = k * e_neg
    r_t = r * jnp.exp(cl)
    e_end = jnp.exp(cl_last - cl)
    b_end = kk * a * e_end
    k_end = k * e_end
    p_end = jnp.exp(cl_last)

    n2 = 2 * CHUNK
    ri = _iota((n2, n2), 0)
    ci = _iota((n2, n2), 1)
    eye = ri == ci
    same_blk = (ri // SOLVE_BLOCK) == (ci // SOLVE_BLOCK)
    ri4 = _iota((2 * n2, 2 * n2), 0)
    ci4 = _iota((2 * n2, 2 * n2), 1) & (n2 - 1)
    causal4 = ((ri4 < n2) & (ri4 > ci4)) | ((ri4 >= n2) & ((ri4 - n2) >= ci4))
    zeros = jnp.zeros((n2, n2), BF16)

    items = [(c_i, p) for c_i in range(n_chunks) for p in range(PAIRS)]

    def tile(t, item):
        c_i, p = item
        return t[c_i * CHUNK:(c_i + 1) * CHUNK, p * LANES:(p + 1) * LANES]

    def bf16_all(xs):
        return [x.astype(BF16) for x in xs]

    sa = [_stack_heads(tile(a_t, it)).astype(BF16) for it in items]
    sr = [_stack_heads(tile(r_t, it)) for it in items]
    sv = [_stack_heads(tile(v, it)).astype(BF16) for it in items]
    big = [jnp.where(causal4,
                     _dot_nt(jnp.concatenate([sa[i], sr[i].astype(BF16)], axis=0),
                             jnp.concatenate([_stack_heads(tile(b_t, it)).astype(BF16),
                                              _stack_heads(tile(k_t, it)).astype(BF16)],
                                             axis=0)), 0.0)
           for i, it in enumerate(items)]
    a_ab = [b[0:n2, 0:n2] for b in big]
    a_ak = bf16_all(b[0:n2, n2:2 * n2] for b in big)
    a_rbk = bf16_all(b[n2:2 * n2, :] for b in big)

    def pair_dot(xs, ys, prod=_dot):
        out = []
        for i in range(0, len(xs), 2):
            z = jnp.zeros_like(ys[i])
            wide = prod(jnp.concatenate([xs[i], xs[i + 1]], axis=1),
                        jnp.concatenate([jnp.concatenate([ys[i], z], axis=1),
                                         jnp.concatenate([z, ys[i + 1]], axis=1)], axis=0))
            out += [wide[:, 0:n2], wide[:, n2:2 * n2]]
        return out

    def plus_eye(xs):
        return [_add_eye(x, eye).astype(BF16) for x in xs]

    d1_f = [jnp.where(same_blk, a, 0.0) for a in a_ab]
    e1 = bf16_all(jnp.where(same_blk, 0.0, a) for a in a_ab)
    d1 = bf16_all(d1_f)
    d2_f = pair_dot(d1, d1)
    d2 = bf16_all(d2_f)
    d4_f = pair_dot(d2, d2)
    d4 = bf16_all(d4_f)
    d8_f = pair_dot(d4, d4)
    p12 = bf16_all(pair_dot(plus_eye(d1_f), plus_eye(d2_f)))
    p48 = bf16_all(pair_dot(plus_eye(d4_f), plus_eye(d8_f)))
    t_d = bf16_all(pair_dot(p12, p48))
    g1_f = pair_dot(t_d, e1)
    g1 = bf16_all(g1_f)
    g2_f = pair_dot(g1, g1)
    gx = bf16_all(pair_dot(plus_eye(g1_f), plus_eye(g2_f)))
    t_inv = bf16_all(pair_dot(gx, t_d))

    akv = bf16_all(pair_dot(a_ak, sv))
    wu = bf16_all(_dot(t, jnp.concatenate([s, x], axis=1)) for t, s, x in zip(t_inv, sa, akv))
    rhs = [jnp.concatenate([w, jnp.concatenate([zeros, s], axis=1)], axis=0)
           for w, s in zip(wu, sv)]
    out_c = [_dot(a, x) for a, x in zip(a_rbk, rhs)]
    end_t = [jnp.concatenate([_stack_heads(tile(b_end, it)).T,
                              _stack_heads(tile(k_end, it)).T], axis=1).astype(BF16)
             for it in items]
    end_c = [_dot(e, x) for e, x in zip(end_t, rhs)]

    y_rows = []
    for c_i in range(n_chunks):
        psi_phi = []
        for p in range(PAIRS):
            i = c_i * PAIRS + p
            psi = sr[i] + out_c[i][:, 0:n2]
            pe = p_end[c_i * CHUNK:c_i * CHUNK + 1, p * LANES:(p + 1) * LANES]
            phi = jnp.where(eye, pe, 0.0) + end_c[i][:, 0:n2]
            psi_phi.append(jnp.concatenate([psi, phi], axis=0))
        both = pair_dot(psi_phi, [state_ref[p] for p in range(PAIRS)], prod=_dot_wide_rhs)
        y_cols = []
        for p in range(PAIRS):
            i = c_i * PAIRS + p
            y = both[p][0:n2, :] + out_c[i][:, n2:2 * n2]
            state_ref[p] = both[p][n2:2 * n2, :] + end_c[i][:, n2:2 * n2]
            y_cols.append(y[0:CHUNK, :] + y[CHUNK:n2, :])
        y_rows.append(jnp.concatenate(y_cols, axis=1))
    y = jnp.concatenate(y_rows, axis=0)
    mean = _seg_sum(y, ones_bd) * (1.0 / HEAD)
    yc = y - mean
    var = _seg_sum(yc * yc, ones_bd) * (1.0 / HEAD)
    yn = yc * lax.rsqrt(var + LN_X_EPS) * lng_ref[...] + lnb_ref[...]
    o_ref[0] = ((yn + bonus) * g).astype(o_ref.dtype)


def _rwkv_call(z_rwkv, mu_shift, w0, a0, lora_cat, g_lora_up, k_k, k_a, r_k, lnx_g, lnx_b,
               ones_bd, tri):
    batch, seq, _ = z_rwkv.shape
    rows = RWKV_ROWS
    const = lambda shape: pl.BlockSpec(shape, lambda b, c: (0,) * len(shape))
    vec = const((1, WIDTH))
    return pl.pallas_call(
        _rwkv_kernel,
        grid=(batch, seq // rows),
        in_specs=[pl.BlockSpec((1, rows, RWKV_COLS), lambda b, c: (b, c, 0)),
                  const((1, RWKV_COLS)), vec, vec,
                  const((DECAY_LORA + AAA_LORA, 2 * WIDTH)), const((GATE_LORA, WIDTH)),
                  vec, vec, vec, vec, vec, const((MXU_DIM, MXU_DIM)), const((rows, rows))],
        out_specs=pl.BlockSpec((1, rows, WIDTH), lambda b, c: (b, c, 0)),
        out_shape=jax.ShapeDtypeStruct((batch, seq, WIDTH), BF16),
        scratch_shapes=[pltpu.VMEM((1, RWKV_COLS), F32),
                        pltpu.VMEM((PAIRS, LANES, LANES), F32)],
        compiler_params=pltpu.CompilerParams(
            dimension_semantics=("arbitrary", "arbitrary"), vmem_limit_bytes=VMEM_LIMIT),
        name="rwkv7",
    )(z_rwkv, mu_shift, w0, a0, lora_cat, g_lora_up, k_k, k_a, r_k, lnx_g, lnx_b, ones_bd, tri)


def _moba_kernel(q_ref, k_ref, v_ref, sel_ref, o_ref, m_ref, l_ref, acc_ref):
    qi = pl.program_id(2)
    tq = q_ref.shape[1]
    n_pairs = q_ref.shape[2] // LANES
    n_heads = 2 * n_pairs
    first = _iota((tq, LANES), 1) < HEAD
    pair_lanes = [slice(p * LANES, (p + 1) * LANES) for p in range(n_pairs)]
    qh = []
    for lanes in pair_lanes:
        q = q_ref[0, :, lanes]
        zero = jnp.zeros_like(q)
        qh += [jnp.where(first, q, zero), jnp.where(first, zero, q)]

    m_ref[...] = jnp.full(m_ref.shape, NEG_INF, F32)
    l_ref[...] = jnp.zeros(l_ref.shape, F32)
    acc_ref[...] = jnp.zeros(acc_ref.shape, F32)
    tiles = [(h, r0) for h in range(n_heads) for r0 in range(0, tq, MOBA_SUB)]
    q_tiles = [qh[h][r0:r0 + MOBA_SUB] for h, r0 in tiles]

    def update(scores, v_blks):
        n_keys = v_blks[0].shape[0]
        v_ext = [jnp.concatenate([v, jnp.ones((n_keys, LANES), BF16)], axis=1) for v in v_blks]
        refs = [(h, slice(r0, r0 + MOBA_SUB)) for h, r0 in tiles]
        m_old = [m_ref[h, rows, :] for h, rows in refs]
        l_old = [l_ref[h, rows, :] for h, rows in refs]
        acc_old = [acc_ref[h, rows, :] for h, rows in refs]
        m_new = [jnp.maximum(m, jnp.broadcast_to(jnp.max(s, axis=-1, keepdims=True),
                                                 (MOBA_SUB, LANES)))
                 for m, s in zip(m_old, scores)]
        alpha = [jnp.exp(m - n) for m, n in zip(m_old, m_new)]
        p = [jnp.exp(s - jnp.concatenate([n] * (n_keys // LANES), axis=1)).astype(BF16)
             for s, n in zip(scores, m_new)]
        pv = [jnp.dot(x, v_ext[h // 2], preferred_element_type=F32)
              for x, (h, _) in zip(p, tiles)]
        for (h, rows), m, a, l, acc, y in zip(refs, m_new, alpha, l_old, acc_old, pv):
            m_ref[h, rows, :] = m
            l_ref[h, rows, :] = a * l + y[:, LANES:2 * LANES]
            acc_ref[h, rows, :] = a * acc + y[:, 0:LANES]

    start = pl.multiple_of(qi * MOBA_BLOCK, MOBA_BLOCK)
    k_own = [k_ref[0, pl.ds(start, MOBA_BLOCK), lanes] for lanes in pair_lanes]
    v_own = [v_ref[0, pl.ds(start, MOBA_BLOCK), lanes] for lanes in pair_lanes]
    row = _iota((MOBA_SUB, MOBA_BLOCK), 0)
    col = _iota((MOBA_SUB, MOBA_BLOCK), 1)
    update([jnp.where(col <= row + r0, _dot_nt(qt, k_own[h // 2]), NEG_INF)
            for qt, (h, r0) in zip(q_tiles, tiles)], v_own)

    sel = sel_ref[0]
    sel_head = _iota(sel.shape, 1) // SEL_GROUP
    head0 = pl.program_id(1) * n_heads
    sel_h = [jnp.where(sel_head == head0 + h, sel, jnp.zeros_like(sel)) for h in range(n_heads)]
    q_ext = [jnp.concatenate([qt, sel_h[h][r0:r0 + MOBA_SUB]], axis=1)
             for qt, (h, r0) in zip(q_tiles, tiles)]
    key_blk = _iota((2 * MOBA_BLOCK, LANES), 0) // MOBA_BLOCK
    key_lane = _iota((2 * MOBA_BLOCK, LANES), 1)
    in_table = key_lane < RWKV_HEADS * SEL_GROUP

    def past_blocks(j, carry):
        start = pl.multiple_of(j * (2 * MOBA_BLOCK), 2 * MOBA_BLOCK)
        block_onehot = (in_table & ((key_lane & (SEL_GROUP - 1)) == 2 * j + key_blk)).astype(BF16)
        k_ext = [jnp.concatenate([k_ref[0, pl.ds(start, 2 * MOBA_BLOCK), lanes], block_onehot],
                                 axis=1) for lanes in pair_lanes]
        v_blks = [v_ref[0, pl.ds(start, 2 * MOBA_BLOCK), lanes] for lanes in pair_lanes]
        update([_dot_nt(qe, k_ext[h // 2]) for qe, (h, _) in zip(q_ext, tiles)], v_blks)
        return carry

    lax.fori_loop(0, (qi + 1) // 2, past_blocks, 0)

    for p, lanes in enumerate(pair_lanes):
        o_ref[0, :, lanes] = jnp.where(first, acc_ref[2 * p] / l_ref[2 * p],
                                       acc_ref[2 * p + 1] / l_ref[2 * p + 1]).astype(o_ref.dtype)


def _moba_call(q, k, v, sel_bias):
    batch, seq, _ = q.shape
    tq = MOBA_BLOCK
    width = MOBA_PAIRS * LANES
    n_heads = 2 * MOBA_PAIRS
    return pl.pallas_call(
        _moba_kernel,
        grid=(batch, PAIRS // MOBA_PAIRS, seq // tq),
        in_specs=[pl.BlockSpec((1, tq, width), lambda b, p, i: (b, i, p)),
                  pl.BlockSpec((1, seq, width), lambda b, p, i: (b, 0, p)),
                  pl.BlockSpec((1, seq, width), lambda b, p, i: (b, 0, p)),
                  pl.BlockSpec((1, tq, LANES), lambda b, p, i: (b, i, 0))],
        out_specs=pl.BlockSpec((1, tq, width), lambda b, p, i: (b, i, p)),
        out_shape=jax.ShapeDtypeStruct((batch, seq, WIDTH), BF16),
        scratch_shapes=[pltpu.VMEM((n_heads, tq, LANES), F32),
                        pltpu.VMEM((n_heads, tq, LANES), F32),
                        pltpu.VMEM((n_heads, tq, LANES), F32)],
        compiler_params=pltpu.CompilerParams(
            dimension_semantics=("arbitrary", "arbitrary", "arbitrary"),
            vmem_limit_bytes=VMEM_LIMIT),
        name="moba",
    )(q, k, v, sel_bias)


def _pack_bf16_pair(lo, hi):
    lo_bits = lax.bitcast_convert_type(lo.astype(BF16).astype(F32), jnp.uint32)
    hi_bits = lax.bitcast_convert_type(hi.astype(BF16).astype(F32), jnp.uint32)
    return (lo_bits >> 16) | (hi_bits & jnp.uint32(0xFFFF0000))


def _unpack_bf16_pair(u):
    lo = lax.bitcast_convert_type(u << 16, F32)
    hi = lax.bitcast_convert_type(u & jnp.uint32(0xFFFF0000), F32)
    return lo, hi


def _merge_kernel(x_ref, ya_ref, yb_ref, gate_ref, wa_ref, wb_ref, wo_ref, g1_ref, ln_ref,
                  scale_ref, shift_ref, wr_ref, br_ref, tri_ref, x1_ref, h2_ref, route_ref,
                  count_ref, carry_ref):
    @pl.when(pl.program_id(0) == 0)
    def _():
        carry_ref[...] = jnp.zeros_like(carry_ref)

    ya = jnp.dot(ya_ref[...], wa_ref[...], preferred_element_type=F32)
    yb = jnp.dot(yb_ref[...], wb_ref[...], preferred_element_type=F32)
    gates = gate_ref[...]
    merged = (gates[:, 0:D_MODEL].astype(F32) * ya + gates[:, D_MODEL:GATE_COLS].astype(F32) * yb)
    x1 = x_ref[...] + g1_ref[0] * jnp.dot(merged.astype(BF16), wo_ref[...],
                                          preferred_element_type=F32)
    x1_ref[...] = x1

    ms = jnp.mean(x1 * x1, axis=-1, keepdims=True)
    h2 = x1 * lax.rsqrt(ms + NORM_EPS) * ln_ref[...]
    h2 = h2 * (1.0 + scale_ref[0]) + shift_ref[0]
    half = D_MODEL // 2
    h2_ref[...] = _pack_bf16_pair(h2[:, 0:half], h2[:, half:D_MODEL])

    logits = _dot_x3(h2, wr_ref[...]) + br_ref[...]
    lane = _iota(logits.shape, 1)
    lane_f = lane.astype(F32)
    far = float(ROUTER_LANES)

    def top(vals):
        m = jnp.max(vals, axis=-1, keepdims=True)
        idx = jnp.min(jnp.where(vals == m, lane_f, far), axis=-1, keepdims=True)
        return m, idx

    grp = jnp.where(lane < N_GROUPS, logits, NEG_INF)
    g_max, g_idx = top(grp)
    p_group = 1.0 / jnp.sum(jnp.exp(grp - g_max), axis=-1, keepdims=True)

    e_lo = EXPERT_LANE0 + EXPERTS_PER_GROUP * g_idx
    in_grp = (lane_f >= e_lo) & (lane_f < e_lo + EXPERTS_PER_GROUP)
    el = jnp.where(in_grp, logits, NEG_INF)
    m1, i1 = top(el)
    m2, i2 = top(jnp.where(lane_f == i1, NEG_INF, el))
    ratio = jnp.exp(m2 - m1)
    w_first = p_group / (1.0 + ratio)
    w_second = w_first * ratio

    first = lane_f == i1
    second = lane_f == i2
    hits = (first | second).astype(BF16)
    before = carry_ref[...] + jnp.dot(tri_ref[...], hits, preferred_element_type=F32)
    rank1 = jnp.sum(jnp.where(first, before, 0.0), axis=-1, keepdims=True)
    rank2 = jnp.sum(jnp.where(second, before, 0.0), axis=-1, keepdims=True)
    carry = carry_ref[...] + jnp.sum(hits.astype(F32), axis=0, keepdims=True)
    carry_ref[...] = carry
    count_ref[...] = carry

    fields = (i1 - EXPERT_LANE0, i2 - EXPERT_LANE0, w_first, w_second, rank1, rank2)
    route = jnp.zeros(logits.shape, F32)
    for n, field in enumerate(fields):
        route = jnp.where(lane == n, field, route)
    route_ref[...] = route


R_EXPERT1, R_EXPERT2, R_WEIGHT1, R_WEIGHT2, R_RANK1, R_RANK2 = range(6)
MERGE_ROWS = 512


def _merge_call(x2d, ya, yb, gates, w_br_rwkv, w_br_moba, w_out, gate1, ln2_g, scale2, shift2,
                w_router, b_router, tri, seq):
    tokens = x2d.shape[0]
    tm = MERGE_ROWS
    per_seq = seq // tm
    row = lambda width: pl.BlockSpec((tm, width), lambda i: (i, 0))
    per_batch = pl.BlockSpec((1, 1, D_MODEL), lambda i: (i // per_seq, 0, 0))
    const = lambda shape: pl.BlockSpec(shape, lambda i: (0,) * len(shape))
    return pl.pallas_call(
        _merge_kernel,
        grid=(tokens // tm,),
        in_specs=[row(D_MODEL), row(WIDTH), row(WIDTH), row(GATE_COLS),
                  const((WIDTH, D_MODEL)), const((WIDTH, D_MODEL)), const((D_MODEL, D_MODEL)),
                  per_batch, const((1, D_MODEL)), per_batch, per_batch,
                  const((D_MODEL, ROUTER_LANES)), const((1, ROUTER_LANES)), const((tm, tm))],
        out_specs=[row(D_MODEL), row(D_MODEL // 2), row(ROUTER_LANES),
                   const((1, ROUTER_LANES))],
        out_shape=[jax.ShapeDtypeStruct((tokens, D_MODEL), F32),
                   jax.ShapeDtypeStruct((tokens, D_MODEL // 2), jnp.uint32),
                   jax.ShapeDtypeStruct((tokens, ROUTER_LANES), F32),
                   jax.ShapeDtypeStruct((1, ROUTER_LANES), F32)],
        scratch_shapes=[pltpu.VMEM((1, ROUTER_LANES), F32)],
        compiler_params=pltpu.CompilerParams(
            dimension_semantics=("arbitrary",), vmem_limit_bytes=VMEM_LIMIT),
        name="merge",
    )(x2d, ya, yb, gates, w_br_rwkv, w_br_moba, w_out, gate1, ln2_g, scale2, shift2,
      w_router, b_router, tri)


EXPERT_ROWS = 512
DISPATCH_TOKENS = 1024
COMBINE_TOKENS = 256


def _dest_kernel(route_ref, start_ref, o_ref):
    route = route_ref[...]
    lane = _iota(route.shape, 1)
    lane_f = lane.astype(F32)
    starts = start_ref[...]
    out = jnp.zeros(route.shape, F32)
    for n, (e_lane, r_lane) in enumerate(((R_EXPERT1, R_RANK1), (R_EXPERT2, R_RANK2))):
        e_col = route[:, e_lane:e_lane + 1] + EXPERT_LANE0
        first_row = jnp.sum(jnp.where(lane_f == e_col, starts, 0.0), axis=-1, keepdims=True)
        out = jnp.where(lane == n, first_row + route[:, r_lane:r_lane + 1], out)
    o_ref[...] = out


def _dest_call(route, start_row):
    tokens = route.shape[0]
    tm = DISPATCH_TOKENS
    return pl.pallas_call(
        _dest_kernel,
        grid=(tokens // tm,),
        in_specs=[pl.BlockSpec((tm, ROUTER_LANES), lambda i: (i, 0)),
                  pl.BlockSpec((1, ROUTER_LANES), lambda i: (0, 0))],
        out_specs=pl.BlockSpec((tm, ROUTER_LANES), lambda i: (i, 0)),
        out_shape=jax.ShapeDtypeStruct((tokens, ROUTER_LANES), F32),
        name="moe_dest",
    )(route, start_row)


def _dispatch_kernel(d1_ref, d2_ref, h_ref, xs_init_hbm, xs_hbm, sem):
    del xs_init_hbm
    base = pl.program_id(0) * DISPATCH_TOKENS

    def issue(t, carry):
        src = h_ref.at[pl.ds(t, 1)]
        pltpu.make_async_copy(src, xs_hbm.at[pl.ds(d1_ref[base + t], 1)], sem).start()
        pltpu.make_async_copy(src, xs_hbm.at[pl.ds(d2_ref[base + t], 1)], sem).start()
        return carry

    lax.fori_loop(0, DISPATCH_TOKENS, issue, 0, unroll=8)
    for _ in range(2):
        pltpu.make_async_copy(h_ref, xs_hbm.at[pl.ds(0, DISPATCH_TOKENS)], sem).wait()


def _dispatch_call(dest1, dest2, h2p, n_rows):
    tokens = h2p.shape[0]
    any_spec = pl.BlockSpec(memory_space=pl.ANY)
    return pl.pallas_call(
        _dispatch_kernel,
        grid_spec=pltpu.PrefetchScalarGridSpec(
            num_scalar_prefetch=2,
            grid=(tokens // DISPATCH_TOKENS,),
            in_specs=[pl.BlockSpec((DISPATCH_TOKENS, D_MODEL // 2), lambda i, d1, d2: (i, 0)),
                      any_spec],
            out_specs=any_spec,
            scratch_shapes=[pltpu.SemaphoreType.DMA(())]),
        out_shape=jax.ShapeDtypeStruct((n_rows, D_MODEL // 2), jnp.uint32),
        input_output_aliases={3: 0},
        compiler_params=pltpu.CompilerParams(dimension_semantics=("arbitrary",)),
        name="moe_dispatch",
    )(dest1, dest2, h2p, jnp.zeros((n_rows, D_MODEL // 2), jnp.uint32))


def _expert_kernel(te_ref, nu_ref, xs_ref, w1_ref, w3_ref, w2_ref, ys_ref):
    half = D_MODEL // 2

    @pl.when(pl.program_id(0) < nu_ref[0])
    def _():
        x_lo, x_hi = _unpack_bf16_pair(xs_ref[...])
        x_lo = x_lo.astype(BF16)
        x_hi = x_hi.astype(BF16)

        def proj(w_ref):
            return (jnp.dot(x_lo, w_ref[0, 0:half, :], preferred_element_type=F32)
                    + jnp.dot(x_hi, w_ref[0, half:D_MODEL, :], preferred_element_type=F32))

        a = proj(w1_ref)
        hid = (a * jax.nn.sigmoid(a) * proj(w3_ref)).astype(BF16)
        y = jnp.dot(hid, w2_ref[0], preferred_element_type=F32)
        ys_ref[...] = _pack_bf16_pair(y[:, 0:half], y[:, half:D_MODEL])

    @pl.when(pl.program_id(0) >= nu_ref[0])
    def _():
        ys_ref[...] = jnp.zeros_like(ys_ref)


def _expert_call(tile_expert, n_used, xs, w1, w3, w2):
    n_rows = xs.shape[0]
    half = D_MODEL // 2
    w_spec = lambda shape: pl.BlockSpec((1,) + shape, lambda j, te, nu: (te[j], 0, 0))
    return pl.pallas_call(
        _expert_kernel,
        grid_spec=pltpu.PrefetchScalarGridSpec(
            num_scalar_prefetch=2,
            grid=(n_rows // EXPERT_ROWS,),
            in_specs=[pl.BlockSpec((EXPERT_ROWS, half), lambda j, te, nu: (j, 0)),
                      w_spec((D_MODEL, D_EXPERT)), w_spec((D_MODEL, D_EXPERT)),
                      w_spec((D_EXPERT, D_MODEL))],
            out_specs=pl.BlockSpec((EXPERT_ROWS, half), lambda j, te, nu: (j, 0))),
        out_shape=jax.ShapeDtypeStruct((n_rows, half), jnp.uint32),
        compiler_params=pltpu.CompilerParams(
            dimension_semantics=("arbitrary",), vmem_limit_bytes=VMEM_LIMIT),
        name="moe_experts",
    )(tile_expert, n_used, xs, w1, w3, w2)


def _combine_kernel(d1_ref, d2_ref, ys_hbm, x1_ref, route_ref, g2_ref, o_ref, buf_ref, sem):
    i = pl.program_id(0)
    n_steps = pl.num_programs(0)
    tc = COMBINE_TOKENS
    half = D_MODEL // 2

    def issue(step, slot):
        base = step * tc

        def one(t, carry):
            tok = base + t
            pltpu.make_async_copy(ys_hbm.at[pl.ds(d1_ref[tok], 1)],
                                  buf_ref.at[slot, pl.ds(t, 1)], sem.at[slot]).start()
            pltpu.make_async_copy(ys_hbm.at[pl.ds(d2_ref[tok], 1)],
                                  buf_ref.at[slot, pl.ds(tc + t, 1)], sem.at[slot]).start()
            return carry

        lax.fori_loop(0, tc, one, 0, unroll=8)

    slot = i % 2

    @pl.when(i == 0)
    def _():
        issue(0, 0)

    @pl.when(i + 1 < n_steps)
    def _():
        issue(i + 1, 1 - slot)

    pltpu.make_async_copy(ys_hbm.at[pl.ds(0, 2 * tc)], buf_ref.at[slot], sem.at[slot]).wait()

    rows = buf_ref[slot]
    a_lo, a_hi = _unpack_bf16_pair(rows[0:tc])
    b_lo, b_hi = _unpack_bf16_pair(rows[tc:2 * tc])
    route = route_ref[...]
    w_a = route[:, R_WEIGHT1:R_WEIGHT1 + 1]
    w_b = route[:, R_WEIGHT2:R_WEIGHT2 + 1]
    g2 = g2_ref[0]
    o_ref[:, 0:half] = x1_ref[:, 0:half] + g2[:, 0:half] * (w_a * a_lo + w_b * b_lo)
    o_ref[:, half:D_MODEL] = (x1_ref[:, half:D_MODEL]
                              + g2[:, half:D_MODEL] * (w_a * a_hi + w_b * b_hi))


def _combine_call(dest1, dest2, ys, x1, route, gate2, seq):
    tokens = x1.shape[0]
    tc = COMBINE_TOKENS
    per_seq = seq // tc
    half = D_MODEL // 2
    return pl.pallas_call(
        _combine_kernel,
        grid_spec=pltpu.PrefetchScalarGridSpec(
            num_scalar_prefetch=2,
            grid=(tokens // tc,),
            in_specs=[pl.BlockSpec(memory_space=pl.ANY),
                      pl.BlockSpec((tc, D_MODEL), lambda i, d1, d2: (i, 0)),
                      pl.BlockSpec((tc, ROUTER_LANES), lambda i, d1, d2: (i, 0)),
                      pl.BlockSpec((1, 1, D_MODEL), lambda i, d1, d2: (i // per_seq, 0, 0))],
            out_specs=pl.BlockSpec((tc, D_MODEL), lambda i, d1, d2: (i, 0)),
            scratch_shapes=[pltpu.VMEM((2, 2 * tc, half), jnp.uint32),
                            pltpu.SemaphoreType.DMA((2,))]),
        out_shape=jax.ShapeDtypeStruct((tokens, D_MODEL), F32),
        compiler_params=pltpu.CompilerParams(
            dimension_semantics=("arbitrary",), vmem_limit_bytes=VMEM_LIMIT),
        name="moe_combine",
    )(dest1, dest2, ys, x1, route, gate2)


def _moe(h2p, route, counts, x1, gate2, w1, w3, w2, seq):
    tokens = h2p.shape[0]
    n_rows = 2 * tokens + N_EXPERTS * EXPERT_ROWS
    n_rows -= n_rows % EXPERT_ROWS
    n_tiles = n_rows // EXPERT_ROWS

    count = counts[0, EXPERT_LANE0:EXPERT_LANE0 + N_EXPERTS].astype(jnp.int32)
    padded = (count + EXPERT_ROWS - 1) // EXPERT_ROWS * EXPERT_ROWS
    ends = jnp.cumsum(padded)
    starts = ends - padded
    tile_start = jnp.arange(n_tiles, dtype=jnp.int32) * EXPERT_ROWS
    tile_expert = jnp.minimum(jnp.sum(ends[None, :] <= tile_start[:, None], axis=1),
                              N_EXPERTS - 1).astype(jnp.int32)
    n_used = (ends[-1:] // EXPERT_ROWS).astype(jnp.int32)

    start_row = jnp.zeros((1, ROUTER_LANES), F32)
    start_row = start_row.at[0, EXPERT_LANE0:EXPERT_LANE0 + N_EXPERTS].set(starts.astype(F32))
    dest = _dest_call(route, start_row)[:, 0:2].astype(jnp.int32)
    dest1, dest2 = dest[:, 0], dest[:, 1]

    xs = _dispatch_call(dest1, dest2, h2p, n_rows)
    ys = _expert_call(tile_expert, n_used, xs, w1, w3, w2)
    return _combine_call(dest1, dest2, ys, x1, route, gate2, seq)


def _rope_tables(seq):
    half = HEAD // 2
    inv_freq = ROPE_THETA ** (-jnp.arange(half, dtype=F32) / half)
    ang = jnp.arange(seq, dtype=F32)[:, None] * inv_freq[None, :]
    cos = jnp.cos(ang)
    sin = jnp.sin(ang)
    cos_head = jnp.concatenate([cos, cos], axis=1)
    sin_head = jnp.concatenate([-sin, sin], axis=1)
    return jnp.tile(cos_head, (1, RWKV_HEADS)), jnp.tile(sin_head, (1, RWKV_HEADS))


def _layer(x, c, w_ada, b_ada, ln1_g, ln2_g, w_in, mu_shift, w0, w_lora_up, a0, a_lora_up,
           g_lora_up, k_k, k_a, r_k, lnx_g, lnx_b, q_norm_g, k_norm_g, w_br_rwkv, w_br_moba,
           w_out, w_rg, b_rg, w_re, b_re, w1, w3, w2):
    batch, seq, _ = x.shape
    vec = lambda a: a.reshape(1, -1)

    mod = _mod_call(c, w_ada, b_ada)
    shift1, scale1, gate1, shift2, scale2, gate2 = (
        m.reshape(batch, 1, D_MODEL) for m in jnp.split(mod, 6, axis=-1))

    idx = np.arange(MXU_DIM)
    ones_bd = jnp.asarray(idx[:, None] // HEAD == idx[None, :] // HEAD, BF16)
    t_idx = np.arange(RWKV_ROWS)
    tri = jnp.asarray((t_idx[:, None] >= t_idx[None, :])
                      & (t_idx[:, None] // CHUNK == t_idx[None, :] // CHUNK), BF16)
    cos, sin_signed = _rope_tables(seq)

    z_rwkv, q, k, v, gates, sel_bias = _inproj_call(
        x, shift1, scale1, vec(ln1_g), w_in.astype(BF16),
        vec(jnp.tile(q_norm_g, RWKV_HEADS)), vec(jnp.tile(k_norm_g, RWKV_HEADS)),
        cos, sin_signed, ones_bd)

    lora_cat = jnp.zeros((DECAY_LORA + AAA_LORA, 2 * WIDTH), F32)
    lora_cat = lora_cat.at[:DECAY_LORA, :WIDTH].set(w_lora_up).at[DECAY_LORA:, WIDTH:].set(a_lora_up)
    y_a = _rwkv_call(z_rwkv, vec(mu_shift), vec(w0), vec(a0), lora_cat, g_lora_up, vec(k_k),
                     vec(k_a), vec(r_k), vec(lnx_g), vec(lnx_b), ones_bd, tri)

    y_b = _moba_call(q, k, v, sel_bias)

    tokens = batch * seq
    w_router = jnp.zeros((D_MODEL, ROUTER_LANES), F32)
    w_router = w_router.at[:, :N_GROUPS].set(w_rg).at[:, EXPERT_LANE0:EXPERT_LANE0 + N_EXPERTS].set(w_re)
    b_router = jnp.zeros((1, ROUTER_LANES), F32)
    b_router = b_router.at[0, :N_GROUPS].set(b_rg).at[0, EXPERT_LANE0:EXPERT_LANE0 + N_EXPERTS].set(b_re)
    m_idx = np.arange(MERGE_ROWS)
    tri_strict = jnp.asarray(m_idx[:, None] > m_idx[None, :], BF16)
    x1, h2p, route, counts = _merge_call(
        x.reshape(tokens, D_MODEL), y_a.reshape(tokens, WIDTH), y_b.reshape(tokens, WIDTH),
        gates.reshape(tokens, GATE_COLS), w_br_rwkv.astype(BF16), w_br_moba.astype(BF16),
        w_out.astype(BF16), gate1, vec(ln2_g), scale2, shift2, w_router, b_router, tri_strict,
        seq)

    out = _moe(h2p, route, counts, x1, gate2, w1.astype(BF16), w3.astype(BF16), w2.astype(BF16),
               seq)
    return out.reshape(batch, seq, D_MODEL)


def kernel(x, c, w_ada, b_ada, ln1_g, ln2_g, w_in, mu_shift, w0, w_lora_up, a0, a_lora_up,
           g_lora_up, k_k, k_a, r_k, lnx_g, lnx_b, q_norm_g, k_norm_g, w_br_rwkv, w_br_moba,
           w_out, w_rg, b_rg, w_re, b_re, w1, w3, w2):
    assert w_ada.shape[0] == 1, "single-layer problem"
    layer_params = (w_ada, b_ada, ln1_g, ln2_g, w_in, mu_shift, w0, w_lora_up, a0, a_lora_up,
                    g_lora_up, k_k, k_a, r_k, lnx_g, lnx_b, q_norm_g, k_norm_g, w_br_rwkv,
                    w_br_moba, w_out, w_rg, b_rg, w_re, b_re, w1, w3, w2)
    return _layer(x, c, *(p[0] for p in layer_params))
```

```python
import functools

import jax
import jax.numpy as jnp
import numpy as np
from jax import lax
from jax.experimental import pallas as pl
from jax.experimental.pallas import tpu as pltpu

F32 = jnp.float32
BF16 = jnp.bfloat16
HIGHEST = lax.Precision.HIGHEST

D_MODEL = 1024
RWKV_HEADS = 8
HEAD = 64
WIDTH = RWKV_HEADS * HEAD
DECAY_LORA = 64
AAA_LORA = 64
GATE_LORA = 128
RWKV_COLS = 3 * WIDTH + DECAY_LORA + AAA_LORA + GATE_LORA
ATT_COLS = 3 * WIDTH
GATE_COLS = 2 * D_MODEL
IN_COLS = RWKV_COLS + ATT_COLS + GATE_COLS
DECAY_SCALE = 0.606531
LN_X_EPS = 64e-5
MOBA_BLOCK = 256
MOBA_TOPK = 3
ROPE_THETA = 10000.0
N_GROUPS = 4
EXPERTS_PER_GROUP = 8
N_EXPERTS = N_GROUPS * EXPERTS_PER_GROUP
D_EXPERT = D_MODEL // 2
NORM_EPS = 1e-6
NEG_INF = -1e30

LANES = 128
MXU_DIM = 256
SUBLANES = 8
PAIRS = WIDTH // LANES
CHUNK = 64
SOLVE_BLOCK = 16
RWKV_ROWS = 4 * CHUNK
MOBA_SUB = 256
MOBA_PAIRS = 4
VMEM_LIMIT = 56 * 1024 * 1024

ROUTER_LANES = LANES
EXPERT_LANE0 = N_GROUPS


def _dot(a, b):
    return jnp.dot(a.astype(BF16), b.astype(BF16), preferred_element_type=F32)


def _dot_nt(a, b):
    return lax.dot_general(a.astype(BF16), b.astype(BF16), (((1,), (1,)), ((), ())),
                           preferred_element_type=F32)


def _dot_f32(a, b):
    return jnp.dot(a, b, precision=HIGHEST, preferred_element_type=F32)


def _split_bf16(a):
    hi = a.astype(BF16)
    return hi, (a - hi.astype(F32)).astype(BF16)


def _dot_x3(a, b):
    a_hi, a_lo = _split_bf16(a)
    b_hi, b_lo = _split_bf16(b)
    return (jnp.dot(a_hi, b_hi, preferred_element_type=F32)
            + jnp.dot(a_hi, b_lo, preferred_element_type=F32)
            + jnp.dot(a_lo, b_hi, preferred_element_type=F32))


def _dot_wide_rhs(a, b):
    a = a.astype(BF16)
    hi, lo = _split_bf16(b)
    return (jnp.dot(a, hi, preferred_element_type=F32)
            + jnp.dot(a, lo, preferred_element_type=F32))


def _seg_sum(x, ones_bd, wide=True):
    group = ones_bd.shape[0]
    parts = _split_bf16(x) if wide else (x.astype(BF16),)
    cols = []
    for c0 in range(0, x.shape[1], group):
        cols.append(sum(jnp.dot(part[:, c0:c0 + group], ones_bd, preferred_element_type=F32)
                        for part in parts))
    return jnp.concatenate(cols, axis=1)


def _iota(shape, axis):
    return lax.broadcasted_iota(jnp.int32, shape, axis)


def _mod_kernel(c_ref, w_ref, b_ref, o_ref):
    c = c_ref[...]
    o_ref[...] = _dot_f32(c * jax.nn.sigmoid(c), w_ref[...]) + b_ref[...]


def _mod_call(c, w_ada, b_ada):
    batch = c.shape[0]
    n_out = w_ada.shape[1]
    tn = D_MODEL
    return pl.pallas_call(
        _mod_kernel,
        grid=(n_out // tn,),
        in_specs=[pl.BlockSpec((batch, D_MODEL), lambda j: (0, 0)),
                  pl.BlockSpec((D_MODEL, tn), lambda j: (0, j)),
                  pl.BlockSpec((1, tn), lambda j: (0, j))],
        out_specs=pl.BlockSpec((batch, tn), lambda j: (0, j)),
        out_shape=jax.ShapeDtypeStruct((batch, n_out), F32),
        name="adaln_mod",
    )(c, w_ada, b_ada.reshape(1, n_out))


def _swap_halves(x):
    first = (_iota(x.shape, 1) & (HEAD - 1)) < HEAD // 2
    up = pltpu.roll(x, LANES - HEAD // 2, axis=1)
    down = pltpu.roll(x, HEAD // 2, axis=1)
    return jnp.where(first, up, down)


def _head_norm_rope(x, gain, cos, sin_signed, ones_bd):
    ms = _seg_sum(x * x, ones_bd, wide=False) * (1.0 / HEAD)
    y = x * lax.rsqrt(ms + NORM_EPS) * gain
    cols = []
    for p in range(PAIRS):
        sl = slice(p * LANES, (p + 1) * LANES)
        yb = y[:, sl]
        cols.append(yb * cos[:, sl] + _swap_halves(yb) * sin_signed[:, sl])
    return jnp.concatenate(cols, axis=1)


SEL_GROUP = 8


def _block_bias(q, km_ref, blk_idx):
    q_hi, q_lo = _split_bf16(q)
    km_hi, km_lo = _split_bf16(km_ref[...])
    nt = lambda a, b: lax.dot_general(a, b, (((1,), (1,)), ((), ())), preferred_element_type=F32)
    gate = nt(q_hi, km_hi) + nt(q_hi, km_lo) + nt(q_lo, km_hi)
    lane = _iota(gate.shape, 1)
    blk = lane & (SEL_GROUP - 1)
    valid = (blk < blk_idx) & (lane < RWKV_HEADS * SEL_GROUP)
    g = jnp.where(valid, gate, NEG_INF)
    rank = jnp.zeros(gate.shape, F32)
    for shift in range(1, SEL_GROUP):
        wrapped = blk + shift >= SEL_GROUP
        partner = jnp.where(wrapped, pltpu.roll(g, SEL_GROUP - shift, axis=1),
                            pltpu.roll(g, LANES - shift, axis=1))
        ahead = (partner > g) | ((partner == g) & wrapped)
        rank = rank + ahead.astype(F32)
    return jnp.where(valid & (rank < MOBA_TOPK), 0.0, NEG_INF)


def _inproj_kernel(x_ref, shift_ref, scale_ref, g_ref, w_ref, qg_ref, kg_ref, cos_ref, sin_ref,
                   ones_ref, zr_ref, q_ref, k_ref, v_ref, gate_ref, bias_ref, km_ref):
    i = pl.program_id(1)

    @pl.when(i == 0)
    def _():
        km_ref[...] = jnp.zeros_like(km_ref)

    x = x_ref[0]
    ms = jnp.mean(x * x, axis=-1, keepdims=True)
    h = x * lax.rsqrt(ms + NORM_EPS) * g_ref[...]
    h = (h * (1.0 + scale_ref[0]) + shift_ref[0]).astype(BF16)

    zr_ref[0] = jnp.dot(h, w_ref[:, 0:RWKV_COLS], preferred_element_type=F32)

    za = jnp.dot(h, w_ref[:, RWKV_COLS:RWKV_COLS + ATT_COLS], preferred_element_type=F32)
    ones_bd = ones_ref[...]
    cos = cos_ref[...]
    sin = sin_ref[...]
    q = _head_norm_rope(za[:, 0:WIDTH], qg_ref[...], cos, sin, ones_bd)
    k = _head_norm_rope(za[:, WIDTH:2 * WIDTH], kg_ref[...], cos, sin, ones_bd)
    q_ref[0] = (q * (HEAD ** -0.5)).astype(BF16)
    k_ref[0] = k.astype(BF16)
    v_ref[0] = za[:, 2 * WIDTH:3 * WIDTH].astype(BF16)

    bias_ref[0] = _block_bias(q, km_ref, i).astype(BF16)
    k_mean = jnp.mean(k, axis=0, keepdims=True)
    head_of_lane = _iota(k_mean.shape, 1) // HEAD
    for head in range(RWKV_HEADS):
        km_ref[pl.ds(head * SEL_GROUP + i, 1), :] = jnp.where(head_of_lane == head, k_mean, 0.0)

    zg = jnp.dot(h, w_ref[:, RWKV_COLS + ATT_COLS:IN_COLS], preferred_element_type=F32)
    gate_ref[0] = jax.nn.sigmoid(zg).astype(BF16)


def _inproj_call(x, shift1, scale1, ln1_g, w_in_bf16, q_gain, k_gain, cos, sin_signed, ones_bd):
    batch, seq, _ = x.shape
    tm = MOBA_BLOCK
    n_t = seq // tm
    row = lambda width: pl.BlockSpec((1, tm, width), lambda b, i: (b, i, 0))
    per_batch = pl.BlockSpec((1, 1, D_MODEL), lambda b, i: (b, 0, 0))
    const = lambda shape: pl.BlockSpec(shape, lambda b, i: (0,) * len(shape))
    return pl.pallas_call(
        _inproj_kernel,
        grid=(batch, n_t),
        in_specs=[row(D_MODEL), per_batch, per_batch, const((1, D_MODEL)),
                  const((D_MODEL, IN_COLS)), const((1, WIDTH)), const((1, WIDTH)),
                  pl.BlockSpec((tm, WIDTH), lambda b, i: (i, 0)),
                  pl.BlockSpec((tm, WIDTH), lambda b, i: (i, 0)),
                  const((MXU_DIM, MXU_DIM))],
        out_specs=[row(RWKV_COLS), row(WIDTH), row(WIDTH), row(WIDTH), row(GATE_COLS),
                   row(LANES)],
        out_shape=[jax.ShapeDtypeStruct((batch, seq, RWKV_COLS), F32),
                   jax.ShapeDtypeStruct((batch, seq, WIDTH), BF16),
                   jax.ShapeDtypeStruct((batch, seq, WIDTH), BF16),
                   jax.ShapeDtypeStruct((batch, seq, WIDTH), BF16),
                   jax.ShapeDtypeStruct((batch, seq, GATE_COLS), BF16),
                   jax.ShapeDtypeStruct((batch, seq, LANES), BF16)],
        scratch_shapes=[pltpu.VMEM((LANES, WIDTH), F32)],
        compiler_params=pltpu.CompilerParams(
            dimension_semantics=("arbitrary", "arbitrary"), vmem_limit_bytes=VMEM_LIMIT),
        name="inproj",
    )(x, shift1, scale1, ln1_g, w_in_bf16, q_gain, k_gain, cos, sin_signed, ones_bd)


def _stack_heads(x):
    first = _iota(x.shape, 1) < HEAD
    return jnp.concatenate([jnp.where(first, x, 0.0), jnp.where(first, 0.0, x)], axis=0)


def _add_eye(x, eye):
    return jnp.where(eye, x + 1.0, x)


def _rwkv_kernel(z_ref, mu_ref, w0_ref, a0_ref, lora_ref, glora_ref, kk_ref, ka_ref, rk_ref,
                 lng_ref, lnb_ref, ones_ref, tri_ref, o_ref, prev_ref, state_ref):
    c = pl.program_id(1)

    @pl.when(c == 0)
    def _():
        prev_ref[...] = jnp.zeros_like(prev_ref)
        state_ref[...] = jnp.zeros_like(state_ref)

    z = z_ref[0]
    rows = z.shape[0]
    n_chunks = rows // CHUNK
    row = _iota(z.shape, 0)
    z_prev = jnp.where(row == 0, prev_ref[...], pltpu.roll(z, 1, axis=0))
    prev_ref[...] = z[rows - 1:rows, :]
    zs = z + (z_prev - z) * mu_ref[...]

    r = zs[:, 0:WIDTH]
    k = zs[:, WIDTH:2 * WIDTH]
    v = zs[:, 2 * WIDTH:3 * WIDTH]
    lo = 3 * WIDTH
    d_wa = zs[:, lo:lo + DECAY_LORA + AAA_LORA]
    d_g = zs[:, lo + DECAY_LORA + AAA_LORA:RWKV_COLS]
    is_decay = _iota(d_wa.shape, 1) < DECAY_LORA
    pre = _dot_x3(jnp.where(is_decay, jnp.tanh(d_wa), d_wa), lora_ref[...])
    log_w = -DECAY_SCALE * jax.nn.sigmoid(w0_ref[...] + pre[:, 0:WIDTH])
    a = jax.nn.sigmoid(a0_ref[...] + pre[:, WIDTH:2 * WIDTH])
    g = _dot(jax.nn.sigmoid(d_g), glora_ref[...])

    ones_bd = ones_ref[...]
    kk = k * kk_ref[...]
    kk = kk / jnp.maximum(jnp.sqrt(_seg_sum(kk * kk, ones_bd)), 1e-12)
    k = k * (1.0 + (a - 1.0) * ka_ref[...])
    bonus = _seg_sum(r * k * rk_ref[...], ones_bd, wide=False) * v

    cl = _dot_wide_rhs(tri_ref[...], log_w)
    p_end_rows = [cl[(c_i + 1) * CHUNK - 1:(c_i + 1) * CHUNK, :] for c_i in range(n_chunks)]
    cl_last = p_end_rows[0]
    chunk_of_row = _iota(cl.shape, 0) // CHUNK
    for c_i in range(1, n_chunks):
        cl_last = jnp.where(chunk_of_row == c_i, p_end_rows[c_i], cl_last)
    a_t = -kk * jnp.exp(cl - log_w)
    e_neg = jnp.exp(-cl)
    b_t = kk * a * e_neg
    k_t = k * e_neg
    r_t = r * jnp.exp(cl)
    e_end = jnp.exp(cl_last - cl)
    b_end = kk * a * e_end
    k_end = k * e_end
    p_end = jnp.exp(cl_last)

    n2 = 2 * CHUNK
    ri = _iota((n2, n2), 0)
    ci = _iota((n2, n2), 1)
    eye = ri == ci
    same_blk = (ri // SOLVE_BLOCK) == (ci // SOLVE_BLOCK)
    ri4 = _iota((2 * n2, 2 * n2), 0)
    ci4 = _iota((2 * n2, 2 * n2), 1) & (n2 - 1)
    causal4 = ((ri4 < n2) & (ri4 > ci4)) | ((ri4 >= n2) & ((ri4 - n2) >= ci4))
    zeros = jnp.zeros((n2, n2), BF16)

    items = [(c_i, p) for c_i in range(n_chunks) for p in range(PAIRS)]

    def tile(t, item):
        c_i, p = item
        return t[c_i * CHUNK:(c_i + 1) * CHUNK, p * LANES:(p + 1) * LANES]

    def bf16_all(xs):
        return [x.astype(BF16) for x in xs]

    sa = [_stack_heads(tile(a_t, it)).astype(BF16) for it in items]
    sr = [_stack_heads(tile(r_t, it)) for it in items]
    sv = [_stack_heads(tile(v, it)).astype(BF16) for it in items]
    big = [jnp.where(causal4,
                     _dot_nt(jnp.concatenate([sa[i], sr[i].astype(BF16)], axis=0),
                             jnp.concatenate([_stack_heads(tile(b_t, it)).astype(BF16),
                                              _stack_heads(tile(k_t, it)).astype(BF16)],
                                             axis=0)), 0.0)
           for i, it in enumerate(items)]
    a_ab = [b[0:n2, 0:n2] for b in big]
    a_ak = bf16_all(b[0:n2, n2:2 * n2] for b in big)
    a_rbk = bf16_all(b[n2:2 * n2, :] for b in big)

    def pair_dot(xs, ys, prod=_dot):
        out = []
        for i in range(0, len(xs), 2):
            z = jnp.zeros_like(ys[i])
            wide = prod(jnp.concatenate([xs[i], xs[i + 1]], axis=1),
                        jnp.concatenate([jnp.concatenate([ys[i], z], axis=1),
                                         jnp.concatenate([z, ys[i + 1]], axis=1)], axis=0))
            out += [wide[:, 0:n2], wide[:, n2:2 * n2]]
        return out

    def plus_eye(xs):
        return [_add_eye(x, eye).astype(BF16) for x in xs]

    d1_f = [jnp.where(same_blk, a, 0.0) for a in a_ab]
    e1 = bf16_all(jnp.where(same_blk, 0.0, a) for a in a_ab)
    d1 = bf16_all(d1_f)
    d2_f = pair_dot(d1, d1)
    d2 = bf16_all(d2_f)
    d4_f = pair_dot(d2, d2)
    d4 = bf16_all(d4_f)
    d8_f = pair_dot(d4, d4)
    p12 = bf16_all(pair_dot(plus_eye(d1_f), plus_eye(d2_f)))
    p48 = bf16_all(pair_dot(plus_eye(d4_f), plus_eye(d8_f)))
    t_d = bf16_all(pair_dot(p12, p48))
    g1_f = pair_dot(t_d, e1)
    g1 = bf16_all(g1_f)
    g2_f = pair_dot(g1, g1)
    gx = bf16_all(pair_dot(plus_eye(g1_f), plus_eye(g2_f)))
    t_inv = bf16_all(pair_dot(gx, t_d))

    akv = bf16_all(pair_dot(a_ak, sv))
    wu = bf16_all(_dot(t, jnp.concatenate([s, x], axis=1)) for t, s, x in zip(t_inv, sa, akv))
    rhs = [jnp.concatenate([w, jnp.concatenate([zeros, s], axis=1)], axis=0)
           for w, s in zip(wu, sv)]
    out_c = [_dot(a, x) for a, x in zip(a_rbk, rhs)]
    end_t = [jnp.concatenate([_stack_heads(tile(b_end, it)).T,
                              _stack_heads(tile(k_end, it)).T], axis=1).astype(BF16)
             for it in items]
    end_c = [_dot(e, x) for e, x in zip(end_t, rhs)]

    y_rows = []
    for c_i in range(n_chunks):
        psi_phi = []
        for p in range(PAIRS):
            i = c_i * PAIRS + p
            psi = sr[i] + out_c[i][:, 0:n2]
            pe = p_end[c_i * CHUNK:c_i * CHUNK + 1, p * LANES:(p + 1) * LANES]
            phi = jnp.where(eye, pe, 0.0) + end_c[i][:, 0:n2]
            psi_phi.append(jnp.concatenate([psi, phi], axis=0))
        both = pair_dot(psi_phi, [state_ref[p] for p in range(PAIRS)], prod=_dot_wide_rhs)
        y_cols = []
        for p in range(PAIRS):
            i = c_i * PAIRS + p
            y = both[p][0:n2, :] + out_c[i][:, n2:2 * n2]
            state_ref[p] = both[p][n2:2 * n2, :] + end_c[i][:, n2:2 * n2]
            y_cols.append(y[0:CHUNK, :] + y[CHUNK:n2, :])
        y_rows.append(jnp.concatenate(y_cols, axis=1))
    y = jnp.concatenate(y_rows, axis=0)
    mean = _seg_sum(y, ones_bd) * (1.0 / HEAD)
    yc = y - mean
    var = _seg_sum(yc * yc, ones_bd, wide=False) * (1.0 / HEAD)
    yn = yc * lax.rsqrt(var + LN_X_EPS) * lng_ref[...] + lnb_ref[...]
    o_ref[0] = ((yn + bonus) * g).astype(o_ref.dtype)


def _rwkv_call(z_rwkv, mu_shift, w0, a0, lora_cat, g_lora_up, k_k, k_a, r_k, lnx_g, lnx_b,
               ones_bd, tri):
    batch, seq, _ = z_rwkv.shape
    rows = RWKV_ROWS
    const = lambda shape: pl.BlockSpec(shape, lambda b, c: (0,) * len(shape))
    vec = const((1, WIDTH))
    return pl.pallas_call(
        _rwkv_kernel,
        grid=(batch, seq // rows),
        in_specs=[pl.BlockSpec((1, rows, RWKV_COLS), lambda b, c: (b, c, 0)),
                  const((1, RWKV_COLS)), vec, vec,
                  const((DECAY_LORA + AAA_LORA, 2 * WIDTH)), const((GATE_LORA, WIDTH)),
                  vec, vec, vec, vec, vec, const((MXU_DIM, MXU_DIM)), const((rows, rows))],
        out_specs=pl.BlockSpec((1, rows, WIDTH), lambda b, c: (b, c, 0)),
        out_shape=jax.ShapeDtypeStruct((batch, seq, WIDTH), BF16),
        scratch_shapes=[pltpu.VMEM((1, RWKV_COLS), F32),
                        pltpu.VMEM((PAIRS, LANES, LANES), F32)],
        compiler_params=pltpu.CompilerParams(
            dimension_semantics=("arbitrary", "arbitrary"), vmem_limit_bytes=VMEM_LIMIT),
        name="rwkv7",
    )(z_rwkv, mu_shift, w0, a0, lora_cat, g_lora_up, k_k, k_a, r_k, lnx_g, lnx_b, ones_bd, tri)


def _moba_kernel(q_ref, k_ref, v_ref, sel_ref, o_ref, m_ref, l_ref, acc_ref):
    qi = pl.program_id(2)
    tq = q_ref.shape[1]
    n_pairs = q_ref.shape[2] // LANES
    n_heads = 2 * n_pairs
    first = _iota((tq, LANES), 1) < HEAD
    pair_lanes = [slice(p * LANES, (p + 1) * LANES) for p in range(n_pairs)]
    qh = []
    for lanes in pair_lanes:
        q = q_ref[0, :, lanes]
        zero = jnp.zeros_like(q)
        qh += [jnp.where(first, q, zero), jnp.where(first, zero, q)]

    m_ref[...] = jnp.full(m_ref.shape, NEG_INF, F32)
    l_ref[...] = jnp.zeros(l_ref.shape, F32)
    acc_ref[...] = jnp.zeros(acc_ref.shape, F32)
    tiles = [(h, r0) for h in range(n_heads) for r0 in range(0, tq, MOBA_SUB)]
    q_tiles = [qh[h][r0:r0 + MOBA_SUB] for h, r0 in tiles]

    def update(scores, v_blks):
        n_keys = v_blks[0].shape[0]
        v_ext = [jnp.concatenate([v, jnp.ones((n_keys, LANES), BF16)], axis=1) for v in v_blks]
        refs = [(h, slice(r0, r0 + MOBA_SUB)) for h, r0 in tiles]
        m_old = [m_ref[h, rows, :] for h, rows in refs]
        l_old = [l_ref[h, rows, :] for h, rows in refs]
        acc_old = [acc_ref[h, rows, :] for h, rows in refs]
        m_new = [jnp.maximum(m, jnp.broadcast_to(jnp.max(s, axis=-1, keepdims=True),
                                                 (MOBA_SUB, LANES)))
                 for m, s in zip(m_old, scores)]
        alpha = [jnp.exp(m - n) for m, n in zip(m_old, m_new)]
        p = [jnp.exp(s - jnp.concatenate([n] * (n_keys // LANES), axis=1)).astype(BF16)
             for s, n in zip(scores, m_new)]
        pv = [jnp.dot(x, v_ext[h // 2], preferred_element_type=F32)
              for x, (h, _) in zip(p, tiles)]
        for (h, rows), m, a, l, acc, y in zip(refs, m_new, alpha, l_old, acc_old, pv):
            m_ref[h, rows, :] = m
            l_ref[h, rows, :] = a * l + y[:, LANES:2 * LANES]
            acc_ref[h, rows, :] = a * acc + y[:, 0:LANES]

    start = pl.multiple_of(qi * MOBA_BLOCK, MOBA_BLOCK)
    k_own = [k_ref[0, pl.ds(start, MOBA_BLOCK), lanes] for lanes in pair_lanes]
    v_own = [v_ref[0, pl.ds(start, MOBA_BLOCK), lanes] for lanes in pair_lanes]
    row = _iota((MOBA_SUB, MOBA_BLOCK), 0)
    col = _iota((MOBA_SUB, MOBA_BLOCK), 1)
    update([jnp.where(col <= row + r0, _dot_nt(qt, k_own[h // 2]), NEG_INF)
            for qt, (h, r0) in zip(q_tiles, tiles)], v_own)

    sel = sel_ref[0]
    sel_head = _iota(sel.shape, 1) // SEL_GROUP
    head0 = pl.program_id(1) * n_heads
    sel_h = [jnp.where(sel_head == head0 + h, sel, jnp.zeros_like(sel)) for h in range(n_heads)]
    q_ext = [jnp.concatenate([qt, sel_h[h][r0:r0 + MOBA_SUB]], axis=1)
             for qt, (h, r0) in zip(q_tiles, tiles)]
    key_blk = _iota((2 * MOBA_BLOCK, LANES), 0) // MOBA_BLOCK
    key_lane = _iota((2 * MOBA_BLOCK, LANES), 1)
    in_table = key_lane < RWKV_HEADS * SEL_GROUP

    def past_blocks(j, carry):
        start = pl.multiple_of(j * (2 * MOBA_BLOCK), 2 * MOBA_BLOCK)
        block_onehot = (in_table & ((key_lane & (SEL_GROUP - 1)) == 2 * j + key_blk)).astype(BF16)
        k_ext = [jnp.concatenate([k_ref[0, pl.ds(start, 2 * MOBA_BLOCK), lanes], block_onehot],
                                 axis=1) for lanes in pair_lanes]
        v_blks = [v_ref[0, pl.ds(start, 2 * MOBA_BLOCK), lanes] for lanes in pair_lanes]
        update([_dot_nt(qe, k_ext[h // 2]) for qe, (h, _) in zip(q_ext, tiles)], v_blks)
        return carry

    lax.fori_loop(0, (qi + 1) // 2, past_blocks, 0)

    for p, lanes in enumerate(pair_lanes):
        o_ref[0, :, lanes] = jnp.where(first, acc_ref[2 * p] / l_ref[2 * p],
                                       acc_ref[2 * p + 1] / l_ref[2 * p + 1]).astype(o_ref.dtype)


def _moba_call(q, k, v, sel_bias):
    batch, seq, _ = q.shape
    tq = MOBA_BLOCK
    width = MOBA_PAIRS * LANES
    n_heads = 2 * MOBA_PAIRS
    return pl.pallas_call(
        _moba_kernel,
        grid=(batch, PAIRS // MOBA_PAIRS, seq // tq),
        in_specs=[pl.BlockSpec((1, tq, width), lambda b, p, i: (b, i, p)),
                  pl.BlockSpec((1, seq, width), lambda b, p, i: (b, 0, p)),
                  pl.BlockSpec((1, seq, width), lambda b, p, i: (b, 0, p)),
                  pl.BlockSpec((1, tq, LANES), lambda b, p, i: (b, i, 0))],
        out_specs=pl.BlockSpec((1, tq, width), lambda b, p, i: (b, i, p)),
        out_shape=jax.ShapeDtypeStruct((batch, seq, WIDTH), BF16),
        scratch_shapes=[pltpu.VMEM((n_heads, tq, LANES), F32),
                        pltpu.VMEM((n_heads, tq, LANES), F32),
                        pltpu.VMEM((n_heads, tq, LANES), F32)],
        compiler_params=pltpu.CompilerParams(
            dimension_semantics=("arbitrary", "arbitrary", "arbitrary"),
            vmem_limit_bytes=VMEM_LIMIT),
        name="moba",
    )(q, k, v, sel_bias)


def _pack_bf16_pair(lo, hi):
    lo_bits = lax.bitcast_convert_type(lo.astype(BF16).astype(F32), jnp.uint32)
    hi_bits = lax.bitcast_convert_type(hi.astype(BF16).astype(F32), jnp.uint32)
    return (lo_bits >> 16) | (hi_bits & jnp.uint32(0xFFFF0000))


def _unpack_bf16_pair(u):
    lo = lax.bitcast_convert_type(u << 16, F32)
    hi = lax.bitcast_convert_type(u & jnp.uint32(0xFFFF0000), F32)
    return lo, hi


def _merge_kernel(x_ref, ya_ref, yb_ref, gate_ref, wa_ref, wb_ref, wo_ref, g1_ref, ln_ref,
                  scale_ref, shift_ref, wr_ref, br_ref, tri_ref, x1_ref, h2_ref, route_ref,
                  count_ref, carry_ref):
    @pl.when(pl.program_id(0) == 0)
    def _():
        carry_ref[...] = jnp.zeros_like(carry_ref)

    ya = jnp.dot(ya_ref[...], wa_ref[...], preferred_element_type=F32)
    yb = jnp.dot(yb_ref[...], wb_ref[...], preferred_element_type=F32)
    gates = gate_ref[...]
    merged = (gates[:, 0:D_MODEL].astype(F32) * ya + gates[:, D_MODEL:GATE_COLS].astype(F32) * yb)
    x1 = x_ref[...] + g1_ref[0] * jnp.dot(merged.astype(BF16), wo_ref[...],
                                          preferred_element_type=F32)
    x1_ref[...] = x1

    ms = jnp.mean(x1 * x1, axis=-1, keepdims=True)
    h2 = x1 * lax.rsqrt(ms + NORM_EPS) * ln_ref[...]
    h2 = h2 * (1.0 + scale_ref[0]) + shift_ref[0]
    half = D_MODEL // 2
    h2_ref[...] = _pack_bf16_pair(h2[:, 0:half], h2[:, half:D_MODEL])

    logits = _dot_x3(h2, wr_ref[...]) + br_ref[...]
    lane = _iota(logits.shape, 1)
    lane_f = lane.astype(F32)
    far = float(ROUTER_LANES)

    def top(vals):
        m = jnp.max(vals, axis=-1, keepdims=True)
        idx = jnp.min(jnp.where(vals == m, lane_f, far), axis=-1, keepdims=True)
        return m, idx

    grp = jnp.where(lane < N_GROUPS, logits, NEG_INF)
    g_max, g_idx = top(grp)
    p_group = 1.0 / jnp.sum(jnp.exp(grp - g_max), axis=-1, keepdims=True)

    e_lo = EXPERT_LANE0 + EXPERTS_PER_GROUP * g_idx
    in_grp = (lane_f >= e_lo) & (lane_f < e_lo + EXPERTS_PER_GROUP)
    el = jnp.where(in_grp, logits, NEG_INF)
    m1, i1 = top(el)
    m2, i2 = top(jnp.where(lane_f == i1, NEG_INF, el))
    ratio = jnp.exp(m2 - m1)
    w_first = p_group / (1.0 + ratio)
    w_second = w_first * ratio

    first = lane_f == i1
    second = lane_f == i2
    hits = (first | second).astype(BF16)
    before = carry_ref[...] + jnp.dot(tri_ref[...], hits, preferred_element_type=F32)
    rank1 = jnp.sum(jnp.where(first, before, 0.0), axis=-1, keepdims=True)
    rank2 = jnp.sum(jnp.where(second, before, 0.0), axis=-1, keepdims=True)
    carry = carry_ref[...] + jnp.sum(hits.astype(F32), axis=0, keepdims=True)
    carry_ref[...] = carry
    count_ref[...] = carry

    fields = (i1 - EXPERT_LANE0, i2 - EXPERT_LANE0, w_first, w_second, rank1, rank2)
    route = jnp.zeros(logits.shape, F32)
    for n, field in enumerate(fields):
        route = jnp.where(lane == n, field, route)
    route_ref[...] = route


R_EXPERT1, R_EXPERT2, R_WEIGHT1, R_WEIGHT2, R_RANK1, R_RANK2 = range(6)
MERGE_ROWS = 512


def _merge_call(x2d, ya, yb, gates, w_br_rwkv, w_br_moba, w_out, gate1, ln2_g, scale2, shift2,
                w_router, b_router, tri, seq):
    tokens = x2d.shape[0]
    tm = MERGE_ROWS
    per_seq = seq // tm
    row = lambda width: pl.BlockSpec((tm, width), lambda i: (i, 0))
    per_batch = pl.BlockSpec((1, 1, D_MODEL), lambda i: (i // per_seq, 0, 0))
    const = lambda shape: pl.BlockSpec(shape, lambda i: (0,) * len(shape))
    return pl.pallas_call(
        _merge_kernel,
        grid=(tokens // tm,),
        in_specs=[row(D_MODEL), row(WIDTH), row(WIDTH), row(GATE_COLS),
                  const((WIDTH, D_MODEL)), const((WIDTH, D_MODEL)), const((D_MODEL, D_MODEL)),
                  per_batch, const((1, D_MODEL)), per_batch, per_batch,
                  const((D_MODEL, ROUTER_LANES)), const((1, ROUTER_LANES)), const((tm, tm))],
        out_specs=[row(D_MODEL), row(D_MODEL // 2), row(ROUTER_LANES),
                   const((1, ROUTER_LANES))],
        out_shape=[jax.ShapeDtypeStruct((tokens, D_MODEL), F32),
                   jax.ShapeDtypeStruct((tokens, D_MODEL // 2), jnp.uint32),
                   jax.ShapeDtypeStruct((tokens, ROUTER_LANES), F32),
                   jax.ShapeDtypeStruct((1, ROUTER_LANES), F32)],
        scratch_shapes=[pltpu.VMEM((1, ROUTER_LANES), F32)],
        compiler_params=pltpu.CompilerParams(
            dimension_semantics=("arbitrary",), vmem_limit_bytes=VMEM_LIMIT),
        name="merge",
    )(x2d, ya, yb, gates, w_br_rwkv, w_br_moba, w_out, gate1, ln2_g, scale2, shift2,
      w_router, b_router, tri)


EXPERT_ROWS = 512
DISPATCH_TOKENS = 1024
COMBINE_TOKENS = 256


def _dest_kernel(route_ref, start_ref, o_ref):
    route = route_ref[...]
    lane = _iota(route.shape, 1)
    lane_f = lane.astype(F32)
    starts = start_ref[...]
    out = jnp.zeros(route.shape, F32)
    for n, (e_lane, r_lane) in enumerate(((R_EXPERT1, R_RANK1), (R_EXPERT2, R_RANK2))):
        e_col = route[:, e_lane:e_lane + 1] + EXPERT_LANE0
        first_row = jnp.sum(jnp.where(lane_f == e_col, starts, 0.0), axis=-1, keepdims=True)
        out = jnp.where(lane == n, first_row + route[:, r_lane:r_lane + 1], out)
    o_ref[...] = out


def _dest_call(route, start_row):
    tokens = route.shape[0]
    tm = DISPATCH_TOKENS
    return pl.pallas_call(
        _dest_kernel,
        grid=(tokens // tm,),
        in_specs=[pl.BlockSpec((tm, ROUTER_LANES), lambda i: (i, 0)),
                  pl.BlockSpec((1, ROUTER_LANES), lambda i: (0, 0))],
        out_specs=pl.BlockSpec((tm, ROUTER_LANES), lambda i: (i, 0)),
        out_shape=jax.ShapeDtypeStruct((tokens, ROUTER_LANES), F32),
        name="moe_dest",
    )(route, start_row)


def _dispatch_kernel(d1_ref, d2_ref, h_ref, xs_init_hbm, xs_hbm, sem):
    del xs_init_hbm
    base = pl.program_id(0) * DISPATCH_TOKENS

    def issue(t, carry):
        src = h_ref.at[pl.ds(t, 1)]
        pltpu.make_async_copy(src, xs_hbm.at[pl.ds(d1_ref[base + t], 1)], sem).start()
        pltpu.make_async_copy(src, xs_hbm.at[pl.ds(d2_ref[base + t], 1)], sem).start()
        return carry

    lax.fori_loop(0, DISPATCH_TOKENS, issue, 0, unroll=8)
    for _ in range(2):
        pltpu.make_async_copy(h_ref, xs_hbm.at[pl.ds(0, DISPATCH_TOKENS)], sem).wait()


def _dispatch_call(dest1, dest2, h2p, n_rows):
    tokens = h2p.shape[0]
    any_spec = pl.BlockSpec(memory_space=pl.ANY)
    return pl.pallas_call(
        _dispatch_kernel,
        grid_spec=pltpu.PrefetchScalarGridSpec(
            num_scalar_prefetch=2,
            grid=(tokens // DISPATCH_TOKENS,),
            in_specs=[pl.BlockSpec((DISPATCH_TOKENS, D_MODEL // 2), lambda i, d1, d2: (i, 0)),
                      any_spec],
            out_specs=any_spec,
            scratch_shapes=[pltpu.SemaphoreType.DMA(())]),
        out_shape=jax.ShapeDtypeStruct((n_rows, D_MODEL // 2), jnp.uint32),
        input_output_aliases={3: 0},
        compiler_params=pltpu.CompilerParams(dimension_semantics=("arbitrary",)),
        name="moe_dispatch",
    )(dest1, dest2, h2p, jnp.zeros((n_rows, D_MODEL // 2), jnp.uint32))


def _expert_kernel(te_ref, nu_ref, xs_ref, w1_ref, w3_ref, w2_ref, ys_ref):
    half = D_MODEL // 2

    @pl.when(pl.program_id(0) < nu_ref[0])
    def _():
        x_lo, x_hi = _unpack_bf16_pair(xs_ref[...])
        x_lo = x_lo.astype(BF16)
        x_hi = x_hi.astype(BF16)

        def proj(w_ref):
            return (jnp.dot(x_lo, w_ref[0, 0:half, :], preferred_element_type=F32)
                    + jnp.dot(x_hi, w_ref[0, half:D_MODEL, :], preferred_element_type=F32))

        a = proj(w1_ref)
        hid = (a * jax.nn.sigmoid(a) * proj(w3_ref)).astype(BF16)
        y = jnp.dot(hid, w2_ref[0], preferred_element_type=F32)
        ys_ref[...] = _pack_bf16_pair(y[:, 0:half], y[:, half:D_MODEL])

    @pl.when(pl.program_id(0) >= nu_ref[0])
    def _():
        ys_ref[...] = jnp.zeros_like(ys_ref)


def _expert_call(tile_expert, n_used, xs, w1, w3, w2):
    n_rows = xs.shape[0]
    half = D_MODEL // 2
    w_spec = lambda shape: pl.BlockSpec((1,) + shape, lambda j, te, nu: (te[j], 0, 0))
    return pl.pallas_call(
        _expert_kernel,
        grid_spec=pltpu.PrefetchScalarGridSpec(
            num_scalar_prefetch=2,
            grid=(n_rows // EXPERT_ROWS,),
            in_specs=[pl.BlockSpec((EXPERT_ROWS, half), lambda j, te, nu: (j, 0)),
                      w_spec((D_MODEL, D_EXPERT)), w_spec((D_MODEL, D_EXPERT)),
                      w_spec((D_EXPERT, D_MODEL))],
            out_specs=pl.BlockSpec((EXPERT_ROWS, half), lambda j, te, nu: (j, 0))),
        out_shape=jax.ShapeDtypeStruct((n_rows, half), jnp.uint32),
        compiler_params=pltpu.CompilerParams(
            dimension_semantics=("arbitrary",), vmem_limit_bytes=VMEM_LIMIT),
        name="moe_experts",
    )(tile_expert, n_used, xs, w1, w3, w2)


def _combine_kernel(d1_ref, d2_ref, ys_hbm, x1_ref, route_ref, g2_ref, o_ref, buf_ref, sem):
    i = pl.program_id(0)
    n_steps = pl.num_programs(0)
    tc = COMBINE_TOKENS
    half = D_MODEL // 2

    groups = tc // SUBLANES

    def issue(step, slot):
        base = step * tc

        def one_group(grp, carry):
            for r in range(SUBLANES):
                tok = base + grp * SUBLANES + r
                d1 = d1_ref[tok]
                d2 = d2_ref[tok]
                pltpu.make_async_copy(ys_hbm.at[d1 >> 3, pl.ds(d1 & (SUBLANES - 1), 1)],
                                      buf_ref.at[slot, grp, pl.ds(r, 1)], sem.at[slot]).start()
                pltpu.make_async_copy(ys_hbm.at[d2 >> 3, pl.ds(d2 & (SUBLANES - 1), 1)],
                                      buf_ref.at[slot, groups + grp, pl.ds(r, 1)],
                                      sem.at[slot]).start()
            return carry

        lax.fori_loop(0, groups, one_group, 0)

    slot = i % 2

    @pl.when(i == 0)
    def _():
        issue(0, 0)

    @pl.when(i + 1 < n_steps)
    def _():
        issue(i + 1, 1 - slot)

    pltpu.make_async_copy(ys_hbm.at[pl.ds(0, 2 * groups)], buf_ref.at[slot], sem.at[slot]).wait()

    rows = buf_ref[slot].reshape(2 * tc, half)
    a_lo, a_hi = _unpack_bf16_pair(rows[0:tc])
    b_lo, b_hi = _unpack_bf16_pair(rows[tc:2 * tc])
    route = route_ref[...]
    w_a = route[:, R_WEIGHT1:R_WEIGHT1 + 1]
    w_b = route[:, R_WEIGHT2:R_WEIGHT2 + 1]
    g2 = g2_ref[0]
    o_ref[:, 0:half] = x1_ref[:, 0:half] + g2[:, 0:half] * (w_a * a_lo + w_b * b_lo)
    o_ref[:, half:D_MODEL] = (x1_ref[:, half:D_MODEL]
                              + g2[:, half:D_MODEL] * (w_a * a_hi + w_b * b_hi))


def _combine_call(dest1, dest2, ys, x1, route, gate2, seq):
    tokens = x1.shape[0]
    tc = COMBINE_TOKENS
    per_seq = seq // tc
    half = D_MODEL // 2
    return pl.pallas_call(
        _combine_kernel,
        grid_spec=pltpu.PrefetchScalarGridSpec(
            num_scalar_prefetch=2,
            grid=(tokens // tc,),
            in_specs=[pl.BlockSpec(memory_space=pl.ANY),
                      pl.BlockSpec((tc, D_MODEL), lambda i, d1, d2: (i, 0)),
                      pl.BlockSpec((tc, ROUTER_LANES), lambda i, d1, d2: (i, 0)),
                      pl.BlockSpec((1, 1, D_MODEL), lambda i, d1, d2: (i // per_seq, 0, 0))],
            out_specs=pl.BlockSpec((tc, D_MODEL), lambda i, d1, d2: (i, 0)),
            scratch_shapes=[pltpu.VMEM((2, 2 * tc // SUBLANES, SUBLANES, half), jnp.uint32),
                            pltpu.SemaphoreType.DMA((2,))]),
        out_shape=jax.ShapeDtypeStruct((tokens, D_MODEL), F32),
        compiler_params=pltpu.CompilerParams(
            dimension_semantics=("arbitrary",), vmem_limit_bytes=VMEM_LIMIT),
        name="moe_combine",
    )(dest1, dest2, ys.reshape(-1, SUBLANES, half), x1, route, gate2)


def _moe(h2p, route, counts, x1, gate2, w1, w3, w2, seq):
    tokens = h2p.shape[0]
    n_rows = 2 * tokens + N_EXPERTS * EXPERT_ROWS
    n_rows -= n_rows % EXPERT_ROWS
    n_tiles = n_rows // EXPERT_ROWS

    count = counts[0, EXPERT_LANE0:EXPERT_LANE0 + N_EXPERTS].astype(jnp.int32)
    padded = (count + EXPERT_ROWS - 1) // EXPERT_ROWS * EXPERT_ROWS
    ends = jnp.cumsum(padded)
    starts = ends - padded
    tile_start = jnp.arange(n_tiles, dtype=jnp.int32) * EXPERT_ROWS
    tile_expert = jnp.minimum(jnp.sum(ends[None, :] <= tile_start[:, None], axis=1),
                              N_EXPERTS - 1).astype(jnp.int32)
    n_used = (ends[-1:] // EXPERT_ROWS).astype(jnp.int32)

    start_row = jnp.zeros((1, ROUTER_LANES), F32)
    start_row = start_row.at[0, EXPERT_LANE0:EXPERT_LANE0 + N_EXPERTS].set(starts.astype(F32))
    dest = _dest_call(route, start_row)[:, 0:2].astype(jnp.int32)
    dest1, dest2 = dest[:, 0], dest[:, 1]

    xs = _dispatch_call(dest1, dest2, h2p, n_rows)
    ys = _expert_call(tile_expert, n_used, xs, w1, w3, w2)
    return _combine_call(dest1, dest2, ys, x1, route, gate2, seq)


def _rope_tables(seq):
    half = HEAD // 2
    inv_freq = ROPE_THETA ** (-jnp.arange(half, dtype=F32) / half)
    ang = jnp.arange(seq, dtype=F32)[:, None] * inv_freq[None, :]
    cos = jnp.cos(ang)
    sin = jnp.sin(ang)
    cos_head = jnp.concatenate([cos, cos], axis=1)
    sin_head = jnp.concatenate([-sin, sin], axis=1)
    return jnp.tile(cos_head, (1, RWKV_HEADS)), jnp.tile(sin_head, (1, RWKV_HEADS))


def _layer(x, c, w_ada, b_ada, ln1_g, ln2_g, w_in, mu_shift, w0, w_lora_up, a0, a_lora_up,
           g_lora_up, k_k, k_a, r_k, lnx_g, lnx_b, q_norm_g, k_norm_g, w_br_rwkv, w_br_moba,
           w_out, w_rg, b_rg, w_re, b_re, w1, w3, w2):
    batch, seq, _ = x.shape
    vec = lambda a: a.reshape(1, -1)

    mod = _mod_call(c, w_ada, b_ada)
    shift1, scale1, gate1, shift2, scale2, gate2 = (
        m.reshape(batch, 1, D_MODEL) for m in jnp.split(mod, 6, axis=-1))

    idx = np.arange(MXU_DIM)
    ones_bd = jnp.asarray(idx[:, None] // HEAD == idx[None, :] // HEAD, BF16)
    t_idx = np.arange(RWKV_ROWS)
    tri = jnp.asarray((t_idx[:, None] >= t_idx[None, :])
                      & (t_idx[:, None] // CHUNK == t_idx[None, :] // CHUNK), BF16)
    cos, sin_signed = _rope_tables(seq)

    z_rwkv, q, k, v, gates, sel_bias = _inproj_call(
        x, shift1, scale1, vec(ln1_g), w_in.astype(BF16),
        vec(jnp.tile(q_norm_g, RWKV_HEADS)), vec(jnp.tile(k_norm_g, RWKV_HEADS)),
        cos, sin_signed, ones_bd)

    lora_cat = jnp.zeros((DECAY_LORA + AAA_LORA, 2 * WIDTH), F32)
    lora_cat = lora_cat.at[:DECAY_LORA, :WIDTH].set(w_lora_up).at[DECAY_LORA:, WIDTH:].set(a_lora_up)
    y_a = _rwkv_call(z_rwkv, vec(mu_shift), vec(w0), vec(a0), lora_cat, g_lora_up, vec(k_k),
                     vec(k_a), vec(r_k), vec(lnx_g), vec(lnx_b), ones_bd, tri)

    y_b = _moba_call(q, k, v, sel_bias)

    tokens = batch * seq
    w_router = jnp.zeros((D_MODEL, ROUTER_LANES), F32)
    w_router = w_router.at[:, :N_GROUPS].set(w_rg).at[:, EXPERT_LANE0:EXPERT_LANE0 + N_EXPERTS].set(w_re)
    b_router = jnp.zeros((1, ROUTER_LANES), F32)
    b_router = b_router.at[0, :N_GROUPS].set(b_rg).at[0, EXPERT_LANE0:EXPERT_LANE0 + N_EXPERTS].set(b_re)
    m_idx = np.arange(MERGE_ROWS)
    tri_strict = jnp.asarray(m_idx[:, None] > m_idx[None, :], BF16)
    x1, h2p, route, counts = _merge_call(
        x.reshape(tokens, D_MODEL), y_a.reshape(tokens, WIDTH), y_b.reshape(tokens, WIDTH),
        gates.reshape(tokens, GATE_COLS), w_br_rwkv.astype(BF16), w_br_moba.astype(BF16),
        w_out.astype(BF16), gate1, vec(ln2_g), scale2, shift2, w_router, b_router, tri_strict,
        seq)

    out = _moe(h2p, route, counts, x1, gate2, w1.astype(BF16), w3.astype(BF16), w2.astype(BF16),
               seq)
    return out.reshape(batch, seq, D_MODEL)


def kernel(x, c, w_ada, b_ada, ln1_g, ln2_g, w_in, mu_shift, w0, w_lora_up, a0, a_lora_up,
           g_lora_up, k_k, k_a, r_k, lnx_g, lnx_b, q_norm_g, k_norm_g, w_br_rwkv, w_br_moba,
           w_out, w_rg, b_rg, w_re, b_re, w1, w3, w2):
    assert w_ada.shape[0] == 1, "single-layer problem"
    layer_params = (w_ada, b_ada, ln1_g, ln2_g, w_in, mu_shift, w0, w_lora_up, a0, a_lora_up,
                    g_lora_up, k_k, k_a, r_k, lnx_g, lnx_b, q_norm_g, k_norm_g, w_br_rwkv,
                    w_br_moba, w_out, w_rg, b_rg, w_re, b_re, w1, w3, w2)
    return _layer(x, c, *(p[0] for p in layer_params))
```

```python
import functools

import jax
import jax.numpy as jnp
import numpy as np
from jax import lax
from jax.experimental import pallas as pl
from jax.experimental.pallas import tpu as pltpu

F32 = jnp.float32
BF16 = jnp.bfloat16
HIGHEST = lax.Precision.HIGHEST

D_MODEL = 1024
RWKV_HEADS = 8
HEAD = 64
WIDTH = RWKV_HEADS * HEAD
DECAY_LORA = 64
AAA_LORA = 64
GATE_LORA = 128
RWKV_COLS = 3 * WIDTH + DECAY_LORA + AAA_LORA + GATE_LORA
ATT_COLS = 3 * WIDTH
GATE_COLS = 2 * D_MODEL
IN_COLS = RWKV_COLS + ATT_COLS + GATE_COLS
DECAY_SCALE = 0.606531
LN_X_EPS = 64e-5
MOBA_BLOCK = 256
MOBA_TOPK = 3
ROPE_THETA = 10000.0
N_GROUPS = 4
EXPERTS_PER_GROUP = 8
N_EXPERTS = N_GROUPS * EXPERTS_PER_GROUP
D_EXPERT = D_MODEL // 2
NORM_EPS = 1e-6
NEG_INF = -1e30

LANES = 128
MXU_DIM = 256
SUBLANES = 8
PAIRS = WIDTH // LANES
CHUNK = 64
SOLVE_BLOCK = 16
RWKV_ROWS = 4 * CHUNK
MOBA_SUB = 256
INPROJ_ROWS = MOBA_BLOCK
MOBA_PAIRS = 4
VMEM_LIMIT = 56 * 1024 * 1024

ROUTER_LANES = LANES
EXPERT_LANE0 = N_GROUPS


def _dot(a, b):
    return jnp.dot(a.astype(BF16), b.astype(BF16), preferred_element_type=F32)


def _dot_nt(a, b):
    return lax.dot_general(a.astype(BF16), b.astype(BF16), (((1,), (1,)), ((), ())),
                           preferred_element_type=F32)


def _dot_f32(a, b):
    return jnp.dot(a, b, precision=HIGHEST, preferred_element_type=F32)


def _split_bf16(a):
    hi = a.astype(BF16)
    return hi, (a - hi.astype(F32)).astype(BF16)


def _dot_x3(a, b):
    a_hi, a_lo = _split_bf16(a)
    b_hi, b_lo = _split_bf16(b)
    return (jnp.dot(a_hi, b_hi, preferred_element_type=F32)
            + jnp.dot(a_hi, b_lo, preferred_element_type=F32)
            + jnp.dot(a_lo, b_hi, preferred_element_type=F32))


def _dot_wide_rhs(a, b):
    a = a.astype(BF16)
    hi, lo = _split_bf16(b)
    return (jnp.dot(a, hi, preferred_element_type=F32)
            + jnp.dot(a, lo, preferred_element_type=F32))


def _seg_sum(x, ones_bd, wide=True):
    group = ones_bd.shape[0]
    parts = _split_bf16(x) if wide else (x.astype(BF16),)
    cols = []
    for c0 in range(0, x.shape[1], group):
        cols.append(sum(jnp.dot(part[:, c0:c0 + group], ones_bd, preferred_element_type=F32)
                        for part in parts))
    return jnp.concatenate(cols, axis=1)


def _iota(shape, axis):
    return lax.broadcasted_iota(jnp.int32, shape, axis)


def _mod_kernel(c_ref, w_ref, b_ref, o_ref):
    c = c_ref[...]
    o_ref[...] = _dot_f32(c * jax.nn.sigmoid(c), w_ref[...]) + b_ref[...]


def _mod_call(c, w_ada, b_ada):
    batch = c.shape[0]
    n_out = w_ada.shape[1]
    tn = D_MODEL
    return pl.pallas_call(
        _mod_kernel,
        grid=(n_out // tn,),
        in_specs=[pl.BlockSpec((batch, D_MODEL), lambda j: (0, 0)),
                  pl.BlockSpec((D_MODEL, tn), lambda j: (0, j)),
                  pl.BlockSpec((1, tn), lambda j: (0, j))],
        out_specs=pl.BlockSpec((batch, tn), lambda j: (0, j)),
        out_shape=jax.ShapeDtypeStruct((batch, n_out), F32),
        name="adaln_mod",
    )(c, w_ada, b_ada.reshape(1, n_out))


def _swap_halves(x):
    first = (_iota(x.shape, 1) & (HEAD - 1)) < HEAD // 2
    up = pltpu.roll(x, LANES - HEAD // 2, axis=1)
    down = pltpu.roll(x, HEAD // 2, axis=1)
    return jnp.where(first, up, down)


def _head_norm_rope(x, gain, cos, sin_signed, ones_bd):
    ms = _seg_sum(x * x, ones_bd, wide=False) * (1.0 / HEAD)
    y = x * lax.rsqrt(ms + NORM_EPS) * gain
    cols = []
    for p in range(PAIRS):
        sl = slice(p * LANES, (p + 1) * LANES)
        yb = y[:, sl]
        cols.append(yb * cos[:, sl] + _swap_halves(yb) * sin_signed[:, sl])
    return jnp.concatenate(cols, axis=1)


SEL_GROUP = 8


def _block_bias(q, km_ref, blk_idx):
    q_hi, q_lo = _split_bf16(q)
    km_hi, km_lo = _split_bf16(km_ref[...])
    nt = lambda a, b: lax.dot_general(a, b, (((1,), (1,)), ((), ())), preferred_element_type=F32)
    gate = nt(q_hi, km_hi) + nt(q_hi, km_lo) + nt(q_lo, km_hi)
    lane = _iota(gate.shape, 1)
    blk = lane & (SEL_GROUP - 1)
    valid = (blk < blk_idx) & (lane < RWKV_HEADS * SEL_GROUP)
    g = jnp.where(valid, gate, NEG_INF)
    rank = jnp.zeros(gate.shape, F32)
    for shift in range(1, SEL_GROUP):
        wrapped = blk + shift >= SEL_GROUP
        partner = jnp.where(wrapped, pltpu.roll(g, SEL_GROUP - shift, axis=1),
                            pltpu.roll(g, LANES - shift, axis=1))
        ahead = (partner > g) | ((partner == g) & wrapped)
        rank = rank + ahead.astype(F32)
    return jnp.where(valid & (rank < MOBA_TOPK), 0.0, NEG_INF)


def _inproj_kernel(x_ref, shift_ref, scale_ref, g_ref, w_ref, qg_ref, kg_ref, cos_ref, sin_ref,
                   ones_ref, zr_ref, q_ref, k_ref, v_ref, gate_ref, bias_ref, km_ref):
    i = pl.program_id(1)

    @pl.when(i == 0)
    def _():
        km_ref[...] = jnp.zeros_like(km_ref)

    x = x_ref[0]
    ms = jnp.mean(x * x, axis=-1, keepdims=True)
    h = x * lax.rsqrt(ms + NORM_EPS) * g_ref[...]
    h = (h * (1.0 + scale_ref[0]) + shift_ref[0]).astype(BF16)

    zr_ref[0] = jnp.dot(h, w_ref[:, 0:RWKV_COLS], preferred_element_type=F32)

    za = jnp.dot(h, w_ref[:, RWKV_COLS:RWKV_COLS + ATT_COLS], preferred_element_type=F32)
    ones_bd = ones_ref[...]
    cos = cos_ref[...]
    sin = sin_ref[...]
    q = _head_norm_rope(za[:, 0:WIDTH], qg_ref[...], cos, sin, ones_bd)
    k = _head_norm_rope(za[:, WIDTH:2 * WIDTH], kg_ref[...], cos, sin, ones_bd)
    q_ref[0] = (q * (HEAD ** -0.5)).astype(BF16)
    k_ref[0] = k.astype(BF16)
    v_ref[0] = za[:, 2 * WIDTH:3 * WIDTH].astype(BF16)

    head_of_lane = _iota((1, WIDTH), 1) // HEAD
    for sub in range(x.shape[0] // MOBA_BLOCK):
        rows = slice(sub * MOBA_BLOCK, (sub + 1) * MOBA_BLOCK)
        blk = i * (x.shape[0] // MOBA_BLOCK) + sub
        bias_ref[0, rows, :] = _block_bias(q[rows], km_ref, blk).astype(BF16)
        k_mean = jnp.mean(k[rows], axis=0, keepdims=True)
        for head in range(RWKV_HEADS):
            km_ref[pl.ds(head * SEL_GROUP + blk, 1), :] = jnp.where(head_of_lane == head,
                                                                   k_mean, 0.0)

    zg = jnp.dot(h, w_ref[:, RWKV_COLS + ATT_COLS:IN_COLS], preferred_element_type=F32)
    gate_ref[0] = jax.nn.sigmoid(zg).astype(BF16)


def _inproj_call(x, shift1, scale1, ln1_g, w_in_bf16, q_gain, k_gain, cos, sin_signed, ones_bd):
    batch, seq, _ = x.shape
    tm = INPROJ_ROWS
    n_t = seq // tm
    row = lambda width: pl.BlockSpec((1, tm, width), lambda b, i: (b, i, 0))
    per_batch = pl.BlockSpec((1, 1, D_MODEL), lambda b, i: (b, 0, 0))
    const = lambda shape: pl.BlockSpec(shape, lambda b, i: (0,) * len(shape))
    return pl.pallas_call(
        _inproj_kernel,
        grid=(batch, n_t),
        in_specs=[row(D_MODEL), per_batch, per_batch, const((1, D_MODEL)),
                  const((D_MODEL, IN_COLS)), const((1, WIDTH)), const((1, WIDTH)),
                  pl.BlockSpec((tm, WIDTH), lambda b, i: (i, 0)),
                  pl.BlockSpec((tm, WIDTH), lambda b, i: (i, 0)),
                  const((MXU_DIM, MXU_DIM))],
        out_specs=[row(RWKV_COLS), row(WIDTH), row(WIDTH), row(WIDTH), row(GATE_COLS),
                   row(LANES)],
        out_shape=[jax.ShapeDtypeStruct((batch, seq, RWKV_COLS), F32),
                   jax.ShapeDtypeStruct((batch, seq, WIDTH), BF16),
                   jax.ShapeDtypeStruct((batch, seq, WIDTH), BF16),
                   jax.ShapeDtypeStruct((batch, seq, WIDTH), BF16),
                   jax.ShapeDtypeStruct((batch, seq, GATE_COLS), BF16),
                   jax.ShapeDtypeStruct((batch, seq, LANES), BF16)],
        scratch_shapes=[pltpu.VMEM((LANES, WIDTH), F32)],
        compiler_params=pltpu.CompilerParams(
            dimension_semantics=("arbitrary", "arbitrary"), vmem_limit_bytes=VMEM_LIMIT),
        name="inproj",
    )(x, shift1, scale1, ln1_g, w_in_bf16, q_gain, k_gain, cos, sin_signed, ones_bd)


def _stack_heads(x):
    first = _iota(x.shape, 1) < HEAD
    return jnp.concatenate([jnp.where(first, x, 0.0), jnp.where(first, 0.0, x)], axis=0)


def _add_eye(x, eye):
    return jnp.where(eye, x + 1.0, x)


def _rwkv_kernel(z_ref, mu_ref, w0_ref, a0_ref, lora_ref, glora_ref, kk_ref, ka_ref, rk_ref,
                 lng_ref, lnb_ref, ones_ref, tri_ref, o_ref, prev_ref, state_ref):
    c = pl.program_id(1)

    @pl.when(c == 0)
    def _():
        prev_ref[...] = jnp.zeros_like(prev_ref)
        state_ref[...] = jnp.zeros_like(state_ref)

    z = z_ref[0]
    rows = z.shape[0]
    n_chunks = rows // CHUNK
    row = _iota(z.shape, 0)
    z_prev = jnp.where(row == 0, prev_ref[...], pltpu.roll(z, 1, axis=0))
    prev_ref[...] = z[rows - 1:rows, :]
    zs = z + (z_prev - z) * mu_ref[...]

    r = zs[:, 0:WIDTH]
    k = zs[:, WIDTH:2 * WIDTH]
    v = zs[:, 2 * WIDTH:3 * WIDTH]
    lo = 3 * WIDTH
    d_wa = zs[:, lo:lo + DECAY_LORA + AAA_LORA]
    d_g = zs[:, lo + DECAY_LORA + AAA_LORA:RWKV_COLS]
    is_decay = _iota(d_wa.shape, 1) < DECAY_LORA
    pre = _dot_x3(jnp.where(is_decay, jnp.tanh(d_wa), d_wa), lora_ref[...])
    log_w = -DECAY_SCALE * jax.nn.sigmoid(w0_ref[...] + pre[:, 0:WIDTH])
    a = jax.nn.sigmoid(a0_ref[...] + pre[:, WIDTH:2 * WIDTH])
    g = _dot(jax.nn.sigmoid(d_g), glora_ref[...])

    ones_bd = ones_ref[...]
    kk = k * kk_ref[...]
    kk = kk / jnp.maximum(jnp.sqrt(_seg_sum(kk * kk, ones_bd)), 1e-12)
    k = k * (1.0 + (a - 1.0) * ka_ref[...])
    bonus = _seg_sum(r * k * rk_ref[...], ones_bd, wide=False) * v

    cl = _dot_wide_rhs(tri_ref[...], log_w)
    p_end_rows = [cl[(c_i + 1) * CHUNK - 1:(c_i + 1) * CHUNK, :] for c_i in range(n_chunks)]
    cl_last = p_end_rows[0]
    chunk_of_row = _iota(cl.shape, 0) // CHUNK
    for c_i in range(1, n_chunks):
        cl_last = jnp.where(chunk_of_row == c_i, p_end_rows[c_i], cl_last)
    a_t = -kk * jnp.exp(cl - log_w)
    e_neg = jnp.exp(-cl)
    b_t = kk * a * e_neg
    k_t = k * e_neg
    r_t = r * jnp.exp(cl)
    e_end = jnp.exp(cl_last - cl)
    b_end = kk * a * e_end
    k_end = k * e_end
    p_end = jnp.exp(cl_last)

    n2 = 2 * CHUNK
    ri = _iota((n2, n2), 0)
    ci = _iota((n2, n2), 1)
    eye = ri == ci
    same_blk = (ri // SOLVE_BLOCK) == (ci // SOLVE_BLOCK)
    ri4 = _iota((2 * n2, 2 * n2), 0)
    ci4 = _iota((2 * n2, 2 * n2), 1) & (n2 - 1)
    causal4 = ((ri4 < n2) & (ri4 > ci4)) | ((ri4 >= n2) & ((ri4 - n2) >= ci4))
    zeros = jnp.zeros((n2, n2), BF16)

    items = [(c_i, p) for c_i in range(n_chunks) for p in range(PAIRS)]

    def tile(t, item):
        c_i, p = item
        return t[c_i * CHUNK:(c_i + 1) * CHUNK, p * LANES:(p + 1) * LANES]

    def bf16_all(xs):
        return [x.astype(BF16) for x in xs]

    sa = [_stack_heads(tile(a_t, it)).astype(BF16) for it in items]
    sr = [_stack_heads(tile(r_t, it)) for it in items]
    sv = [_stack_heads(tile(v, it)).astype(BF16) for it in items]
    big = [jnp.where(causal4,
                     _dot_nt(jnp.concatenate([sa[i], sr[i].astype(BF16)], axis=0),
                             jnp.concatenate([_stack_heads(tile(b_t, it)).astype(BF16),
                                              _stack_heads(tile(k_t, it)).astype(BF16)],
                                             axis=0)), 0.0)
           for i, it in enumerate(items)]
    a_ab = [b[0:n2, 0:n2] for b in big]
    a_ak = bf16_all(b[0:n2, n2:2 * n2] for b in big)
    a_rbk = bf16_all(b[n2:2 * n2, :] for b in big)

    def pair_dot(xs, ys, prod=_dot):
        out = []
        for i in range(0, len(xs), 2):
            z = jnp.zeros_like(ys[i])
            wide = prod(jnp.concatenate([xs[i], xs[i + 1]], axis=1),
                        jnp.concatenate([jnp.concatenate([ys[i], z], axis=1),
                                         jnp.concatenate([z, ys[i + 1]], axis=1)], axis=0))
            out += [wide[:, 0:n2], wide[:, n2:2 * n2]]
        return out

    def plus_eye(xs):
        return [_add_eye(x, eye).astype(BF16) for x in xs]

    d1_f = [jnp.where(same_blk, a, 0.0) for a in a_ab]
    e1 = bf16_all(jnp.where(same_blk, 0.0, a) for a in a_ab)
    d1 = bf16_all(d1_f)
    d2_f = pair_dot(d1, d1)
    d2 = bf16_all(d2_f)
    d4_f = pair_dot(d2, d2)
    d4 = bf16_all(d4_f)
    d8_f = pair_dot(d4, d4)
    p12 = bf16_all(pair_dot(plus_eye(d1_f), plus_eye(d2_f)))
    p48 = bf16_all(pair_dot(plus_eye(d4_f), plus_eye(d8_f)))
    t_d = bf16_all(pair_dot(p12, p48))
    g1_f = pair_dot(t_d, e1)
    g1 = bf16_all(g1_f)
    g2_f = pair_dot(g1, g1)
    gx = bf16_all(pair_dot(plus_eye(g1_f), plus_eye(g2_f)))
    t_inv = bf16_all(pair_dot(gx, t_d))

    akv = bf16_all(pair_dot(a_ak, sv))
    wu = bf16_all(_dot(t, jnp.concatenate([s, x], axis=1)) for t, s, x in zip(t_inv, sa, akv))
    rhs = [jnp.concatenate([w, jnp.concatenate([zeros, s], axis=1)], axis=0)
           for w, s in zip(wu, sv)]
    out_c = [_dot(a, x) for a, x in zip(a_rbk, rhs)]
    end_t = [jnp.concatenate([_stack_heads(tile(b_end, it)).T,
                              _stack_heads(tile(k_end, it)).T], axis=1).astype(BF16)
             for it in items]
    end_c = [_dot(e, x) for e, x in zip(end_t, rhs)]

    y_rows = []
    for c_i in range(n_chunks):
        psi_phi = []
        for p in range(PAIRS):
            i = c_i * PAIRS + p
            psi = sr[i] + out_c[i][:, 0:n2]
            pe = p_end[c_i * CHUNK:c_i * CHUNK + 1, p * LANES:(p + 1) * LANES]
            phi = jnp.where(eye, pe, 0.0) + end_c[i][:, 0:n2]
            psi_phi.append(jnp.concatenate([psi, phi], axis=0))
        both = pair_dot(psi_phi, [state_ref[p] for p in range(PAIRS)], prod=_dot_wide_rhs)
        y_cols = []
        for p in range(PAIRS):
            i = c_i * PAIRS + p
            y = both[p][0:n2, :] + out_c[i][:, n2:2 * n2]
            state_ref[p] = both[p][n2:2 * n2, :] + end_c[i][:, n2:2 * n2]
            y_cols.append(y[0:CHUNK, :] + y[CHUNK:n2, :])
        y_rows.append(jnp.concatenate(y_cols, axis=1))
    y = jnp.concatenate(y_rows, axis=0)
    mean = _seg_sum(y, ones_bd) * (1.0 / HEAD)
    yc = y - mean
    var = _seg_sum(yc * yc, ones_bd, wide=False) * (1.0 / HEAD)
    yn = yc * lax.rsqrt(var + LN_X_EPS) * lng_ref[...] + lnb_ref[...]
    o_ref[0] = ((yn + bonus) * g).astype(o_ref.dtype)


def _rwkv_call(z_rwkv, mu_shift, w0, a0, lora_cat, g_lora_up, k_k, k_a, r_k, lnx_g, lnx_b,
               ones_bd, tri):
    batch, seq, _ = z_rwkv.shape
    rows = RWKV_ROWS
    const = lambda shape: pl.BlockSpec(shape, lambda b, c: (0,) * len(shape))
    vec = const((1, WIDTH))
    return pl.pallas_call(
        _rwkv_kernel,
        grid=(batch, seq // rows),
        in_specs=[pl.BlockSpec((1, rows, RWKV_COLS), lambda b, c: (b, c, 0)),
                  const((1, RWKV_COLS)), vec, vec,
                  const((DECAY_LORA + AAA_LORA, 2 * WIDTH)), const((GATE_LORA, WIDTH)),
                  vec, vec, vec, vec, vec, const((MXU_DIM, MXU_DIM)), const((rows, rows))],
        out_specs=pl.BlockSpec((1, rows, WIDTH), lambda b, c: (b, c, 0)),
        out_shape=jax.ShapeDtypeStruct((batch, seq, WIDTH), BF16),
        scratch_shapes=[pltpu.VMEM((1, RWKV_COLS), F32),
                        pltpu.VMEM((PAIRS, LANES, LANES), F32)],
        compiler_params=pltpu.CompilerParams(
            dimension_semantics=("arbitrary", "arbitrary"), vmem_limit_bytes=VMEM_LIMIT),
        name="rwkv7",
    )(z_rwkv, mu_shift, w0, a0, lora_cat, g_lora_up, k_k, k_a, r_k, lnx_g, lnx_b, ones_bd, tri)


def _moba_kernel(q_ref, k_ref, v_ref, sel_ref, o_ref, m_ref, l_ref, acc_ref):
    qi = pl.program_id(2)
    tq = q_ref.shape[1]
    n_pairs = q_ref.shape[2] // LANES
    n_heads = 2 * n_pairs
    first = _iota((tq, LANES), 1) < HEAD
    pair_lanes = [slice(p * LANES, (p + 1) * LANES) for p in range(n_pairs)]
    qh = []
    for lanes in pair_lanes:
        q = q_ref[0, :, lanes]
        zero = jnp.zeros_like(q)
        qh += [jnp.where(first, q, zero), jnp.where(first, zero, q)]

    m_ref[...] = jnp.full(m_ref.shape, NEG_INF, F32)
    l_ref[...] = jnp.zeros(l_ref.shape, F32)
    acc_ref[...] = jnp.zeros(acc_ref.shape, F32)
    tiles = [(h, r0) for h in range(n_heads) for r0 in range(0, tq, MOBA_SUB)]
    q_tiles = [qh[h][r0:r0 + MOBA_SUB] for h, r0 in tiles]

    def update(scores, v_blks):
        n_keys = v_blks[0].shape[0]
        v_ext = [jnp.concatenate([v, jnp.ones((n_keys, LANES), BF16)], axis=1) for v in v_blks]
        refs = [(h, slice(r0, r0 + MOBA_SUB)) for h, r0 in tiles]
        m_old = [m_ref[h, rows, :] for h, rows in refs]
        l_old = [l_ref[h, rows, :] for h, rows in refs]
        acc_old = [acc_ref[h, rows, :] for h, rows in refs]
        m_new = [jnp.maximum(m, jnp.broadcast_to(jnp.max(s, axis=-1, keepdims=True),
                                                 (MOBA_SUB, LANES)))
                 for m, s in zip(m_old, scores)]
        alpha = [jnp.exp(m - n) for m, n in zip(m_old, m_new)]
        p = [jnp.exp(s - jnp.concatenate([n] * (n_keys // LANES), axis=1)).astype(BF16)
             for s, n in zip(scores, m_new)]
        pv = [jnp.dot(x, v_ext[h // 2], preferred_element_type=F32)
              for x, (h, _) in zip(p, tiles)]
        for (h, rows), m, a, l, acc, y in zip(refs, m_new, alpha, l_old, acc_old, pv):
            m_ref[h, rows, :] = m
            l_ref[h, rows, :] = a * l + y[:, LANES:2 * LANES]
            acc_ref[h, rows, :] = a * acc + y[:, 0:LANES]

    start = pl.multiple_of(qi * MOBA_BLOCK, MOBA_BLOCK)
    k_own = [k_ref[0, pl.ds(start, MOBA_BLOCK), lanes] for lanes in pair_lanes]
    v_own = [v_ref[0, pl.ds(start, MOBA_BLOCK), lanes] for lanes in pair_lanes]
    row = _iota((MOBA_SUB, MOBA_BLOCK), 0)
    col = _iota((MOBA_SUB, MOBA_BLOCK), 1)
    update([jnp.where(col <= row + r0, _dot_nt(qt, k_own[h // 2]), NEG_INF)
            for qt, (h, r0) in zip(q_tiles, tiles)], v_own)

    sel = sel_ref[0]
    sel_head = _iota(sel.shape, 1) // SEL_GROUP
    head0 = pl.program_id(1) * n_heads
    sel_h = [jnp.where(sel_head == head0 + h, sel, jnp.zeros_like(sel)) for h in range(n_heads)]
    q_ext = [jnp.concatenate([qt, sel_h[h][r0:r0 + MOBA_SUB]], axis=1)
             for qt, (h, r0) in zip(q_tiles, tiles)]
    key_blk = _iota((2 * MOBA_BLOCK, LANES), 0) // MOBA_BLOCK
    key_lane = _iota((2 * MOBA_BLOCK, LANES), 1)
    in_table = key_lane < RWKV_HEADS * SEL_GROUP

    def past_blocks(j, carry):
        start = pl.multiple_of(j * (2 * MOBA_BLOCK), 2 * MOBA_BLOCK)
        block_onehot = (in_table & ((key_lane & (SEL_GROUP - 1)) == 2 * j + key_blk)).astype(BF16)
        k_ext = [jnp.concatenate([k_ref[0, pl.ds(start, 2 * MOBA_BLOCK), lanes], block_onehot],
                                 axis=1) for lanes in pair_lanes]
        v_blks = [v_ref[0, pl.ds(start, 2 * MOBA_BLOCK), lanes] for lanes in pair_lanes]
        update([_dot_nt(qe, k_ext[h // 2]) for qe, (h, _) in zip(q_ext, tiles)], v_blks)
        return carry

    lax.fori_loop(0, (qi + 1) // 2, past_blocks, 0)

    for p, lanes in enumerate(pair_lanes):
        o_ref[0, :, lanes] = jnp.where(first, acc_ref[2 * p] / l_ref[2 * p],
                                       acc_ref[2 * p + 1] / l_ref[2 * p + 1]).astype(o_ref.dtype)


def _moba_call(q, k, v, sel_bias):
    batch, seq, _ = q.shape
    tq = MOBA_BLOCK
    width = MOBA_PAIRS * LANES
    n_heads = 2 * MOBA_PAIRS
    return pl.pallas_call(
        _moba_kernel,
        grid=(batch, PAIRS // MOBA_PAIRS, seq // tq),
        in_specs=[pl.BlockSpec((1, tq, width), lambda b, p, i: (b, i, p)),
                  pl.BlockSpec((1, seq, width), lambda b, p, i: (b, 0, p)),
                  pl.BlockSpec((1, seq, width), lambda b, p, i: (b, 0, p)),
                  pl.BlockSpec((1, tq, LANES), lambda b, p, i: (b, i, 0))],
        out_specs=pl.BlockSpec((1, tq, width), lambda b, p, i: (b, i, p)),
        out_shape=jax.ShapeDtypeStruct((batch, seq, WIDTH), BF16),
        scratch_shapes=[pltpu.VMEM((n_heads, tq, LANES), F32),
                        pltpu.VMEM((n_heads, tq, LANES), F32),
                        pltpu.VMEM((n_heads, tq, LANES), F32)],
        compiler_params=pltpu.CompilerParams(
            dimension_semantics=("arbitrary", "arbitrary", "arbitrary"),
            vmem_limit_bytes=VMEM_LIMIT),
        name="moba",
    )(q, k, v, sel_bias)


def _pack_bf16_pair(lo, hi):
    lo_bits = lax.bitcast_convert_type(lo.astype(BF16).astype(F32), jnp.uint32)
    hi_bits = lax.bitcast_convert_type(hi.astype(BF16).astype(F32), jnp.uint32)
    return (lo_bits >> 16) | (hi_bits & jnp.uint32(0xFFFF0000))


def _unpack_bf16_pair(u):
    lo = lax.bitcast_convert_type(u << 16, F32)
    hi = lax.bitcast_convert_type(u & jnp.uint32(0xFFFF0000), F32)
    return lo, hi


def _merge_kernel(x_ref, ya_ref, yb_ref, gate_ref, wa_ref, wb_ref, wo_ref, g1_ref, ln_ref,
                  scale_ref, shift_ref, wr_ref, br_ref, tri_ref, x1_ref, h2_ref, route_ref,
                  count_ref, carry_ref):
    @pl.when(pl.program_id(0) == 0)
    def _():
        carry_ref[...] = jnp.zeros_like(carry_ref)

    ya = jnp.dot(ya_ref[...], wa_ref[...], preferred_element_type=F32)
    yb = jnp.dot(yb_ref[...], wb_ref[...], preferred_element_type=F32)
    gates = gate_ref[...]
    merged = (gates[:, 0:D_MODEL].astype(F32) * ya + gates[:, D_MODEL:GATE_COLS].astype(F32) * yb)
    x1 = x_ref[...] + g1_ref[0] * jnp.dot(merged.astype(BF16), wo_ref[...],
                                          preferred_element_type=F32)
    x1_ref[...] = x1

    ms = jnp.mean(x1 * x1, axis=-1, keepdims=True)
    h2 = x1 * lax.rsqrt(ms + NORM_EPS) * ln_ref[...]
    h2 = h2 * (1.0 + scale_ref[0]) + shift_ref[0]
    half = D_MODEL // 2
    h2_ref[...] = _pack_bf16_pair(h2[:, 0:half], h2[:, half:D_MODEL])

    logits = _dot_x3(h2, wr_ref[...]) + br_ref[...]
    lane = _iota(logits.shape, 1)
    lane_f = lane.astype(F32)
    far = float(ROUTER_LANES)

    def top(vals):
        m = jnp.max(vals, axis=-1, keepdims=True)
        idx = jnp.min(jnp.where(vals == m, lane_f, far), axis=-1, keepdims=True)
        return m, idx

    grp = jnp.where(lane < N_GROUPS, logits, NEG_INF)
    g_max, g_idx = top(grp)
    p_group = 1.0 / jnp.sum(jnp.exp(grp - g_max), axis=-1, keepdims=True)

    e_lo = EXPERT_LANE0 + EXPERTS_PER_GROUP * g_idx
    in_grp = (lane_f >= e_lo) & (lane_f < e_lo + EXPERTS_PER_GROUP)
    el = jnp.where(in_grp, logits, NEG_INF)
    m1, i1 = top(el)
    m2, i2 = top(jnp.where(lane_f == i1, NEG_INF, el))
    ratio = jnp.exp(m2 - m1)
    w_first = p_group / (1.0 + ratio)
    w_second = w_first * ratio

    first = lane_f == i1
    second = lane_f == i2
    hits = (first | second).astype(BF16)
    before = carry_ref[...] + jnp.dot(tri_ref[...], hits, preferred_element_type=F32)
    rank1 = jnp.sum(jnp.where(first, before, 0.0), axis=-1, keepdims=True)
    rank2 = jnp.sum(jnp.where(second, before, 0.0), axis=-1, keepdims=True)
    carry = carry_ref[...] + jnp.sum(hits.astype(F32), axis=0, keepdims=True)
    carry_ref[...] = carry
    count_ref[...] = carry

    fields = (i1 - EXPERT_LANE0, i2 - EXPERT_LANE0, w_first, w_second, rank1, rank2)
    route = jnp.zeros(logits.shape, F32)
    for n, field in enumerate(fields):
        route = jnp.where(lane == n, field, route)
    route_ref[...] = route


R_EXPERT1, R_EXPERT2, R_WEIGHT1, R_WEIGHT2, R_RANK1, R_RANK2 = range(6)
MERGE_ROWS = 512


def _merge_call(x2d, ya, yb, gates, w_br_rwkv, w_br_moba, w_out, gate1, ln2_g, scale2, shift2,
                w_router, b_router, tri, seq):
    tokens = x2d.shape[0]
    tm = MERGE_ROWS
    per_seq = seq // tm
    row = lambda width: pl.BlockSpec((tm, width), lambda i: (i, 0))
    per_batch = pl.BlockSpec((1, 1, D_MODEL), lambda i: (i // per_seq, 0, 0))
    const = lambda shape: pl.BlockSpec(shape, lambda i: (0,) * len(shape))
    return pl.pallas_call(
        _merge_kernel,
        grid=(tokens // tm,),
        in_specs=[row(D_MODEL), row(WIDTH), row(WIDTH), row(GATE_COLS),
                  const((WIDTH, D_MODEL)), const((WIDTH, D_MODEL)), const((D_MODEL, D_MODEL)),
                  per_batch, const((1, D_MODEL)), per_batch, per_batch,
                  const((D_MODEL, ROUTER_LANES)), const((1, ROUTER_LANES)), const((tm, tm))],
        out_specs=[row(D_MODEL), row(D_MODEL // 2), row(ROUTER_LANES),
                   const((1, ROUTER_LANES))],
        out_shape=[jax.ShapeDtypeStruct((tokens, D_MODEL), F32),
                   jax.ShapeDtypeStruct((tokens, D_MODEL // 2), jnp.uint32),
                   jax.ShapeDtypeStruct((tokens, ROUTER_LANES), F32),
                   jax.ShapeDtypeStruct((1, ROUTER_LANES), F32)],
        scratch_shapes=[pltpu.VMEM((1, ROUTER_LANES), F32)],
        compiler_params=pltpu.CompilerParams(
            dimension_semantics=("arbitrary",), vmem_limit_bytes=VMEM_LIMIT),
        name="merge",
    )(x2d, ya, yb, gates, w_br_rwkv, w_br_moba, w_out, gate1, ln2_g, scale2, shift2,
      w_router, b_router, tri)


EXPERT_ROWS = 512
DISPATCH_TOKENS = 1024
COMBINE_TOKENS = 256


def _dest_kernel(route_ref, start_ref, o_ref):
    route = route_ref[...]
    lane = _iota(route.shape, 1)
    lane_f = lane.astype(F32)
    starts = start_ref[...]
    out = jnp.zeros(route.shape, F32)
    for n, (e_lane, r_lane) in enumerate(((R_EXPERT1, R_RANK1), (R_EXPERT2, R_RANK2))):
        e_col = route[:, e_lane:e_lane + 1] + EXPERT_LANE0
        first_row = jnp.sum(jnp.where(lane_f == e_col, starts, 0.0), axis=-1, keepdims=True)
        out = jnp.where(lane == n, first_row + route[:, r_lane:r_lane + 1], out)
    o_ref[...] = out


def _dest_call(route, start_row):
    tokens = route.shape[0]
    tm = DISPATCH_TOKENS
    return pl.pallas_call(
        _dest_kernel,
        grid=(tokens // tm,),
        in_specs=[pl.BlockSpec((tm, ROUTER_LANES), lambda i: (i, 0)),
                  pl.BlockSpec((1, ROUTER_LANES), lambda i: (0, 0))],
        out_specs=pl.BlockSpec((tm, ROUTER_LANES), lambda i: (i, 0)),
        out_shape=jax.ShapeDtypeStruct((tokens, ROUTER_LANES), F32),
        name="moe_dest",
    )(route, start_row)


def _dispatch_kernel(d1_ref, d2_ref, ends_ref, h_ref, xs_hbm, zero_ref, sem, zero_sem):
    base = pl.program_id(0) * DISPATCH_TOKENS

    @pl.when(pl.program_id(0) == 0)
    def _():
        zero_ref[...] = jnp.zeros_like(zero_ref)

        def fill(e):
            end = ends_ref[e]
            start = pl.multiple_of(end - EXPERT_ROWS, EXPERT_ROWS)
            return pltpu.make_async_copy(zero_ref, xs_hbm.at[pl.ds(start, EXPERT_ROWS)],
                                         zero_sem)

        def non_empty(e):
            return ends_ref[e] > (ends_ref[e - 1] if e else 0)

        for e in range(N_EXPERTS):
            pl.when(non_empty(e))(lambda e=e: fill(e).start())
        for e in range(N_EXPERTS):
            pl.when(non_empty(e))(lambda e=e: fill(e).wait())

        def tail(t):
            start = pl.multiple_of(t * EXPERT_ROWS, EXPERT_ROWS)
            return pltpu.make_async_copy(zero_ref, xs_hbm.at[pl.ds(start, EXPERT_ROWS)],
                                         zero_sem)

        first_tail = ends_ref[N_EXPERTS - 1] // EXPERT_ROWS
        n_tiles = xs_hbm.shape[0] // EXPERT_ROWS
        lax.fori_loop(first_tail, n_tiles, lambda t, c: (tail(t).start(), c)[1], 0)
        lax.fori_loop(first_tail, n_tiles, lambda t, c: (tail(t).wait(), c)[1], 0)

    def issue(t, carry):
        src = h_ref.at[pl.ds(t, 1)]
        pltpu.make_async_copy(src, xs_hbm.at[pl.ds(d1_ref[base + t], 1)], sem).start()
        pltpu.make_async_copy(src, xs_hbm.at[pl.ds(d2_ref[base + t], 1)], sem).start()
        return carry

    lax.fori_loop(0, DISPATCH_TOKENS, issue, 0, unroll=8)
    for _ in range(2):
        pltpu.make_async_copy(h_ref, xs_hbm.at[pl.ds(0, DISPATCH_TOKENS)], sem).wait()


def _dispatch_call(dest1, dest2, ends, h2p, n_rows):
    tokens = h2p.shape[0]
    half = D_MODEL // 2
    return pl.pallas_call(
        _dispatch_kernel,
        grid_spec=pltpu.PrefetchScalarGridSpec(
            num_scalar_prefetch=3,
            grid=(tokens // DISPATCH_TOKENS,),
            in_specs=[pl.BlockSpec((DISPATCH_TOKENS, half), lambda i, d1, d2, ends: (i, 0))],
            out_specs=pl.BlockSpec(memory_space=pl.ANY),
            scratch_shapes=[pltpu.VMEM((EXPERT_ROWS, half), jnp.uint32),
                            pltpu.SemaphoreType.DMA(()), pltpu.SemaphoreType.DMA(())]),
        out_shape=jax.ShapeDtypeStruct((n_rows, half), jnp.uint32),
        compiler_params=pltpu.CompilerParams(dimension_semantics=("arbitrary",)),
        name="moe_dispatch",
    )(dest1, dest2, ends, h2p)


def _expert_kernel(te_ref, nu_ref, xs_ref, w1_ref, w3_ref, w2_ref, ys_ref, w1b_ref, w3b_ref,
                   w2b_ref):
    half = D_MODEL // 2
    j = pl.program_id(0)

    @pl.when((j == 0) | (te_ref[j] != te_ref[jnp.maximum(j - 1, 0)]))
    def _():
        w1b_ref[...] = w1_ref[0].astype(BF16)
        w3b_ref[...] = w3_ref[0].astype(BF16)
        w2b_ref[...] = w2_ref[0].astype(BF16)

    @pl.when(j < nu_ref[0])
    def _():
        x_lo, x_hi = _unpack_bf16_pair(xs_ref[...])
        x_lo = x_lo.astype(BF16)
        x_hi = x_hi.astype(BF16)

        def proj(w_ref):
            return (jnp.dot(x_lo, w_ref[0:half, :], preferred_element_type=F32)
                    + jnp.dot(x_hi, w_ref[half:D_MODEL, :], preferred_element_type=F32))

        a = proj(w1b_ref)
        hid = (a * jax.nn.sigmoid(a) * proj(w3b_ref)).astype(BF16)
        y = jnp.dot(hid, w2b_ref[...], preferred_element_type=F32)
        ys_ref[...] = _pack_bf16_pair(y[:, 0:half], y[:, half:D_MODEL])

    @pl.when(j >= nu_ref[0])
    def _():
        ys_ref[...] = jnp.zeros_like(ys_ref)


def _expert_call(tile_expert, n_used, xs, w1, w3, w2):
    n_rows = xs.shape[0]
    half = D_MODEL // 2
    w_spec = lambda shape: pl.BlockSpec((1,) + shape, lambda j, te, nu: (te[j], 0, 0))
    return pl.pallas_call(
        _expert_kernel,
        grid_spec=pltpu.PrefetchScalarGridSpec(
            num_scalar_prefetch=2,
            grid=(n_rows // EXPERT_ROWS,),
            in_specs=[pl.BlockSpec((EXPERT_ROWS, half),
                                   lambda j, te, nu: (jnp.minimum(j, nu[0] - 1), 0)),
                      w_spec((D_MODEL, D_EXPERT)), w_spec((D_MODEL, D_EXPERT)),
                      w_spec((D_EXPERT, D_MODEL))],
            out_specs=pl.BlockSpec((EXPERT_ROWS, half), lambda j, te, nu: (j, 0)),
            scratch_shapes=[pltpu.VMEM((D_MODEL, D_EXPERT), BF16),
                            pltpu.VMEM((D_MODEL, D_EXPERT), BF16),
                            pltpu.VMEM((D_EXPERT, D_MODEL), BF16)]),
        out_shape=jax.ShapeDtypeStruct((n_rows, half), jnp.uint32),
        compiler_params=pltpu.CompilerParams(
            dimension_semantics=("arbitrary",), vmem_limit_bytes=VMEM_LIMIT),
        name="moe_experts",
    )(tile_expert, n_used, xs, w1, w3, w2)


def _combine_kernel(d1_ref, d2_ref, ys_hbm, x1_ref, route_ref, g2_ref, o_ref, buf_ref, sem):
    i = pl.program_id(0)
    n_steps = pl.num_programs(0)
    tc = COMBINE_TOKENS
    half = D_MODEL // 2

    groups = tc // SUBLANES

    def issue(step, slot):
        base = step * tc

        def one_group(grp, carry):
            for r in range(SUBLANES):
                tok = base + grp * SUBLANES + r
                d1 = d1_ref[tok]
                d2 = d2_ref[tok]
                pltpu.make_async_copy(ys_hbm.at[d1 >> 3, pl.ds(d1 & (SUBLANES - 1), 1)],
                                      buf_ref.at[slot, grp, pl.ds(r, 1)], sem.at[slot]).start()
                pltpu.make_async_copy(ys_hbm.at[d2 >> 3, pl.ds(d2 & (SUBLANES - 1), 1)],
                                      buf_ref.at[slot, groups + grp, pl.ds(r, 1)],
                                      sem.at[slot]).start()
            return carry

        lax.fori_loop(0, groups, one_group, 0)

    slot = i % 2

    @pl.when(i == 0)
    def _():
        issue(0, 0)

    @pl.when(i + 1 < n_steps)
    def _():
        issue(i + 1, 1 - slot)

    pltpu.make_async_copy(ys_hbm.at[pl.ds(0, 2 * groups)], buf_ref.at[slot], sem.at[slot]).wait()

    rows = buf_ref[slot].reshape(2 * tc, half)
    a_lo, a_hi = _unpack_bf16_pair(rows[0:tc])
    b_lo, b_hi = _unpack_bf16_pair(rows[tc:2 * tc])
    route = route_ref[...]
    w_a = route[:, R_WEIGHT1:R_WEIGHT1 + 1]
    w_b = route[:, R_WEIGHT2:R_WEIGHT2 + 1]
    g2 = g2_ref[0]
    o_ref[:, 0:half] = x1_ref[:, 0:half] + g2[:, 0:half] * (w_a * a_lo + w_b * b_lo)
    o_ref[:, half:D_MODEL] = (x1_ref[:, half:D_MODEL]
                              + g2[:, half:D_MODEL] * (w_a * a_hi + w_b * b_hi))


def _combine_call(dest1, dest2, ys, x1, route, gate2, seq):
    tokens = x1.shape[0]
    tc = COMBINE_TOKENS
    per_seq = seq // tc
    half = D_MODEL // 2
    return pl.pallas_call(
        _combine_kernel,
        grid_spec=pltpu.PrefetchScalarGridSpec(
            num_scalar_prefetch=2,
            grid=(tokens // tc,),
            in_specs=[pl.BlockSpec(memory_space=pl.ANY),
                      pl.BlockSpec((tc, D_MODEL), lambda i, d1, d2: (i, 0)),
                      pl.BlockSpec((tc, ROUTER_LANES), lambda i, d1, d2: (i, 0)),
                      pl.BlockSpec((1, 1, D_MODEL), lambda i, d1, d2: (i // per_seq, 0, 0))],
            out_specs=pl.BlockSpec((tc, D_MODEL), lambda i, d1, d2: (i, 0)),
            scratch_shapes=[pltpu.VMEM((2, 2 * tc // SUBLANES, SUBLANES, half), jnp.uint32),
                            pltpu.SemaphoreType.DMA((2,))]),
        out_shape=jax.ShapeDtypeStruct((tokens, D_MODEL), F32),
        compiler_params=pltpu.CompilerParams(
            dimension_semantics=("arbitrary",), vmem_limit_bytes=VMEM_LIMIT),
        name="moe_combine",
    )(dest1, dest2, ys.reshape(-1, SUBLANES, half), x1, route, gate2)


def _moe(h2p, route, counts, x1, gate2, w1, w3, w2, seq):
    tokens = h2p.shape[0]
    n_rows = 2 * tokens + N_EXPERTS * EXPERT_ROWS
    n_rows -= n_rows % EXPERT_ROWS
    n_tiles = n_rows // EXPERT_ROWS

    count = counts[0, EXPERT_LANE0:EXPERT_LANE0 + N_EXPERTS].astype(jnp.int32)
    padded = (count + EXPERT_ROWS - 1) // EXPERT_ROWS * EXPERT_ROWS
    ends = jnp.cumsum(padded)
    starts = ends - padded
    tile_start = jnp.arange(n_tiles, dtype=jnp.int32) * EXPERT_ROWS
    tile_expert = jnp.minimum(jnp.sum(ends[None, :] <= tile_start[:, None], axis=1),
                              N_EXPERTS - 1).astype(jnp.int32)
    n_used = (ends[-1:] // EXPERT_ROWS).astype(jnp.int32)

    start_row = jnp.zeros((1, ROUTER_LANES), F32)
    start_row = start_row.at[0, EXPERT_LANE0:EXPERT_LANE0 + N_EXPERTS].set(starts.astype(F32))
    dest = _dest_call(route, start_row)[:, 0:2].astype(jnp.int32)
    dest1, dest2 = dest[:, 0], dest[:, 1]

    xs = _dispatch_call(dest1, dest2, ends.astype(jnp.int32), h2p, n_rows)
    ys = _expert_call(tile_expert, n_used, xs, w1, w3, w2)
    return _combine_call(dest1, dest2, ys, x1, route, gate2, seq)


def _rope_tables(seq):
    half = HEAD // 2
    inv_freq = ROPE_THETA ** (-jnp.arange(half, dtype=F32) / half)
    ang = jnp.arange(seq, dtype=F32)[:, None] * inv_freq[None, :]
    cos = jnp.cos(ang)
    sin = jnp.sin(ang)
    cos_head = jnp.concatenate([cos, cos], axis=1)
    sin_head = jnp.concatenate([-sin, sin], axis=1)
    return jnp.tile(cos_head, (1, RWKV_HEADS)), jnp.tile(sin_head, (1, RWKV_HEADS))


def _layer(x, c, w_ada, b_ada, ln1_g, ln2_g, w_in, mu_shift, w0, w_lora_up, a0, a_lora_up,
           g_lora_up, k_k, k_a, r_k, lnx_g, lnx_b, q_norm_g, k_norm_g, w_br_rwkv, w_br_moba,
           w_out, w_rg, b_rg, w_re, b_re, w1, w3, w2):
    batch, seq, _ = x.shape
    vec = lambda a: a.reshape(1, -1)

    mod = _mod_call(c, w_ada, b_ada)
    shift1, scale1, gate1, shift2, scale2, gate2 = (
        m.reshape(batch, 1, D_MODEL) for m in jnp.split(mod, 6, axis=-1))

    idx = np.arange(MXU_DIM)
    ones_bd = jnp.asarray(idx[:, None] // HEAD == idx[None, :] // HEAD, BF16)
    t_idx = np.arange(RWKV_ROWS)
    tri = jnp.asarray((t_idx[:, None] >= t_idx[None, :])
                      & (t_idx[:, None] // CHUNK == t_idx[None, :] // CHUNK), BF16)
    cos, sin_signed = _rope_tables(seq)

    z_rwkv, q, k, v, gates, sel_bias = _inproj_call(
        x, shift1, scale1, vec(ln1_g), w_in.astype(BF16),
        vec(jnp.tile(q_norm_g, RWKV_HEADS)), vec(jnp.tile(k_norm_g, RWKV_HEADS)),
        cos, sin_signed, ones_bd)

    lora_cat = jnp.zeros((DECAY_LORA + AAA_LORA, 2 * WIDTH), F32)
    lora_cat = lora_cat.at[:DECAY_LORA, :WIDTH].set(w_lora_up).at[DECAY_LORA:, WIDTH:].set(a_lora_up)
    y_a = _rwkv_call(z_rwkv, vec(mu_shift), vec(w0), vec(a0), lora_cat, g_lora_up, vec(k_k),
                     vec(k_a), vec(r_k), vec(lnx_g), vec(lnx_b), ones_bd, tri)

    y_b = _moba_call(q, k, v, sel_bias)

    tokens = batch * seq
    w_router = jnp.zeros((D_MODEL, ROUTER_LANES), F32)
    w_router = w_router.at[:, :N_GROUPS].set(w_rg).at[:, EXPERT_LANE0:EXPERT_LANE0 + N_EXPERTS].set(w_re)
    b_router = jnp.zeros((1, ROUTER_LANES), F32)
    b_router = b_router.at[0, :N_GROUPS].set(b_rg).at[0, EXPERT_LANE0:EXPERT_LANE0 + N_EXPERTS].set(b_re)
    m_idx = np.arange(MERGE_ROWS)
    tri_strict = jnp.asarray(m_idx[:, None] > m_idx[None, :], BF16)
    x1, h2p, route, counts = _merge_call(
        x.reshape(tokens, D_MODEL), y_a.reshape(tokens, WIDTH), y_b.reshape(tokens, WIDTH),
        gates.reshape(tokens, GATE_COLS), w_br_rwkv.astype(BF16), w_br_moba.astype(BF16),
        w_out.astype(BF16), gate1, vec(ln2_g), scale2, shift2, w_router, b_router, tri_strict,
        seq)

    out = _moe(h2p, route, counts, x1, gate2, w1, w3, w2, seq)
    return out.reshape(batch, seq, D_MODEL)


def kernel(x, c, w_ada, b_ada, ln1_g, ln2_g, w_in, mu_shift, w0, w_lora_up, a0, a_lora_up,
           g_lora_up, k_k, k_a, r_k, lnx_g, lnx_b, q_norm_g, k_norm_g, w_br_rwkv, w_br_moba,
           w_out, w_rg, b_rg, w_re, b_re, w1, w3, w2):
    assert w_ada.shape[0] == 1, "single-layer problem"
    layer_params = (w_ada, b_ada, ln1_g, ln2_g, w_in, mu_shift, w0, w_lora_up, a0, a_lora_up,
                    g_lora_up, k_k, k_a, r_k, lnx_g, lnx_b, q_norm_g, k_norm_g, w_br_rwkv,
                    w_br_moba, w_out, w_rg, b_rg, w_re, b_re, w1, w3, w2)
    return _layer(x, c, *(p[0] for p in layer_params))
```

```python
import jax
import jax.numpy as jnp
import numpy as np
from jax import lax
from jax.experimental import pallas as pl
from jax.experimental.pallas import tpu as pltpu

F32 = jnp.float32
BF16 = jnp.bfloat16
HIGHEST = lax.Precision.HIGHEST

D_MODEL = 1024
RWKV_HEADS = 8
HEAD = 64
WIDTH = RWKV_HEADS * HEAD
DECAY_LORA = 64
AAA_LORA = 64
GATE_LORA = 128
RWKV_COLS = 3 * WIDTH + DECAY_LORA + AAA_LORA + GATE_LORA
ATT_COLS = 3 * WIDTH
GATE_COLS = 2 * D_MODEL
IN_COLS = RWKV_COLS + ATT_COLS + GATE_COLS
DECAY_SCALE = 0.606531
LN_X_EPS = 64e-5
MOBA_BLOCK = 256
MOBA_TOPK = 3
ROPE_THETA = 10000.0
N_GROUPS = 4
EXPERTS_PER_GROUP = 8
N_EXPERTS = N_GROUPS * EXPERTS_PER_GROUP
D_EXPERT = D_MODEL // 2
NORM_EPS = 1e-6
NEG_INF = -1e30

LANES = 128
MXU_DIM = 256
SUBLANES = 8
PAIRS = WIDTH // LANES
CHUNK = 64
SOLVE_BLOCK = 16
RWKV_ROWS = 4 * CHUNK
MOBA_SUB = 256
INPROJ_ROWS = MOBA_BLOCK
MOBA_PAIRS = 4
VMEM_LIMIT = 56 * 1024 * 1024

ROUTER_LANES = LANES
EXPERT_LANE0 = N_GROUPS


def _dot(a, b):
    return jnp.dot(a.astype(BF16), b.astype(BF16), preferred_element_type=F32)


def _dot_nt(a, b):
    return lax.dot_general(a.astype(BF16), b.astype(BF16), (((1,), (1,)), ((), ())),
                           preferred_element_type=F32)


def _dot_f32(a, b):
    return jnp.dot(a, b, precision=HIGHEST, preferred_element_type=F32)


def _split_bf16(a):
    hi = a.astype(BF16)
    return hi, (a - hi.astype(F32)).astype(BF16)


def _dot_x3(a, b):
    a_hi, a_lo = _split_bf16(a)
    b_hi, b_lo = _split_bf16(b)
    return (jnp.dot(a_hi, b_hi, preferred_element_type=F32)
            + jnp.dot(a_hi, b_lo, preferred_element_type=F32)
            + jnp.dot(a_lo, b_hi, preferred_element_type=F32))


def _dot_wide_rhs(a, b):
    a = a.astype(BF16)
    hi, lo = _split_bf16(b)
    return (jnp.dot(a, hi, preferred_element_type=F32)
            + jnp.dot(a, lo, preferred_element_type=F32))


def _seg_sum(x, ones_bd, wide=True):
    group = ones_bd.shape[0]
    parts = _split_bf16(x) if wide else (x.astype(BF16),)
    cols = []
    for c0 in range(0, x.shape[1], group):
        cols.append(sum(jnp.dot(part[:, c0:c0 + group], ones_bd, preferred_element_type=F32)
                        for part in parts))
    return jnp.concatenate(cols, axis=1)


def _iota(shape, axis):
    return lax.broadcasted_iota(jnp.int32, shape, axis)


def _mod_kernel(c_ref, w_ref, b_ref, o_ref):
    c = c_ref[...]
    o_ref[...] = _dot_f32(c * jax.nn.sigmoid(c), w_ref[...]) + b_ref[...]


def _mod_call(c, w_ada, b_ada):
    batch = c.shape[0]
    n_out = w_ada.shape[1]
    tn = D_MODEL
    return pl.pallas_call(
        _mod_kernel,
        grid=(n_out // tn,),
        in_specs=[pl.BlockSpec((batch, D_MODEL), lambda j: (0, 0)),
                  pl.BlockSpec((D_MODEL, tn), lambda j: (0, j)),
                  pl.BlockSpec((1, tn), lambda j: (0, j))],
        out_specs=pl.BlockSpec((batch, tn), lambda j: (0, j)),
        out_shape=jax.ShapeDtypeStruct((batch, n_out), F32),
        name="adaln_mod",
    )(c, w_ada, b_ada.reshape(1, n_out))


def _swap_halves(x):
    first = (_iota(x.shape, 1) & (HEAD - 1)) < HEAD // 2
    up = pltpu.roll(x, LANES - HEAD // 2, axis=1)
    down = pltpu.roll(x, HEAD // 2, axis=1)
    return jnp.where(first, up, down)


def _head_norm_rope(x, gain, cos, sin_signed, ones_bd):
    ms = _seg_sum(x * x, ones_bd, wide=False) * (1.0 / HEAD)
    y = x * lax.rsqrt(ms + NORM_EPS) * gain
    cols = []
    for p in range(PAIRS):
        sl = slice(p * LANES, (p + 1) * LANES)
        yb = y[:, sl]
        cols.append(yb * cos[:, sl] + _swap_halves(yb) * sin_signed[:, sl])
    return jnp.concatenate(cols, axis=1)


SEL_GROUP = 8


def _block_bias(q, km_ref, blk_idx):
    q_hi, q_lo = _split_bf16(q)
    km_hi, km_lo = _split_bf16(km_ref[...])
    nt = lambda a, b: lax.dot_general(a, b, (((1,), (1,)), ((), ())), preferred_element_type=F32)
    gate = nt(q_hi, km_hi) + nt(q_hi, km_lo) + nt(q_lo, km_hi)
    lane = _iota(gate.shape, 1)
    blk = lane & (SEL_GROUP - 1)
    valid = (blk < blk_idx) & (lane < RWKV_HEADS * SEL_GROUP)
    g = jnp.where(valid, gate, NEG_INF)
    rank = jnp.zeros(gate.shape, F32)
    for shift in range(1, SEL_GROUP):
        wrapped = blk + shift >= SEL_GROUP
        partner = jnp.where(wrapped, pltpu.roll(g, SEL_GROUP - shift, axis=1),
                            pltpu.roll(g, LANES - shift, axis=1))
        ahead = (partner > g) | ((partner == g) & wrapped)
        rank = rank + ahead.astype(F32)
    return jnp.where(valid & (rank < MOBA_TOPK), 0.0, NEG_INF)


def _inproj_kernel(x_ref, shift_ref, scale_ref, g_ref, w_ref, qg_ref, kg_ref, cos_ref, sin_ref,
                   ones_ref, zr_ref, q_ref, k_ref, v_ref, gate_ref, bias_ref, km_ref):
    i = pl.program_id(1)

    @pl.when(i == 0)
    def _():
        km_ref[...] = jnp.zeros_like(km_ref)

    x = x_ref[0]
    ms = jnp.mean(x * x, axis=-1, keepdims=True)
    h = x * lax.rsqrt(ms + NORM_EPS) * g_ref[...]
    h = (h * (1.0 + scale_ref[0]) + shift_ref[0]).astype(BF16)

    zr_ref[0] = jnp.dot(h, w_ref[:, 0:RWKV_COLS], preferred_element_type=F32)

    za = jnp.dot(h, w_ref[:, RWKV_COLS:RWKV_COLS + ATT_COLS], preferred_element_type=F32)
    ones_bd = ones_ref[...]
    cos = cos_ref[...]
    sin = sin_ref[...]
    q = _head_norm_rope(za[:, 0:WIDTH], qg_ref[...], cos, sin, ones_bd)
    k = _head_norm_rope(za[:, WIDTH:2 * WIDTH], kg_ref[...], cos, sin, ones_bd)
    q_ref[0] = (q * (HEAD ** -0.5)).astype(BF16)
    k_ref[0] = k.astype(BF16)
    v_ref[0] = za[:, 2 * WIDTH:3 * WIDTH].astype(BF16)

    head_of_lane = _iota((1, WIDTH), 1) // HEAD
    for sub in range(x.shape[0] // MOBA_BLOCK):
        rows = slice(sub * MOBA_BLOCK, (sub + 1) * MOBA_BLOCK)
        blk = i * (x.shape[0] // MOBA_BLOCK) + sub
        bias_ref[0, rows, :] = _block_bias(q[rows], km_ref, blk).astype(BF16)
        k_mean = jnp.mean(k[rows], axis=0, keepdims=True)
        for head in range(RWKV_HEADS):
            km_ref[pl.ds(head * SEL_GROUP + blk, 1), :] = jnp.where(head_of_lane == head,
                                                                   k_mean, 0.0)

    zg = jnp.dot(h, w_ref[:, RWKV_COLS + ATT_COLS:IN_COLS], preferred_element_type=F32)
    gate_ref[0] = jax.nn.sigmoid(zg).astype(BF16)


def _inproj_call(x, shift1, scale1, ln1_g, w_in_bf16, q_gain, k_gain, cos, sin_signed, ones_bd):
    batch, seq, _ = x.shape
    tm = INPROJ_ROWS
    n_t = seq // tm
    row = lambda width: pl.BlockSpec((1, tm, width), lambda b, i: (b, i, 0))
    per_batch = pl.BlockSpec((1, 1, D_MODEL), lambda b, i: (b, 0, 0))
    const = lambda shape: pl.BlockSpec(shape, lambda b, i: (0,) * len(shape))
    return pl.pallas_call(
        _inproj_kernel,
        grid=(batch, n_t),
        in_specs=[row(D_MODEL), per_batch, per_batch, const((1, D_MODEL)),
                  const((D_MODEL, IN_COLS)), const((1, WIDTH)), const((1, WIDTH)),
                  pl.BlockSpec((tm, WIDTH), lambda b, i: (i, 0)),
                  pl.BlockSpec((tm, WIDTH), lambda b, i: (i, 0)),
                  const((MXU_DIM, MXU_DIM))],
        out_specs=[row(RWKV_COLS), row(WIDTH), row(WIDTH), row(WIDTH), row(GATE_COLS),
                   row(LANES)],
        out_shape=[jax.ShapeDtypeStruct((batch, seq, RWKV_COLS), F32),
                   jax.ShapeDtypeStruct((batch, seq, WIDTH), BF16),
                   jax.ShapeDtypeStruct((batch, seq, WIDTH), BF16),
                   jax.ShapeDtypeStruct((batch, seq, WIDTH), BF16),
                   jax.ShapeDtypeStruct((batch, seq, GATE_COLS), BF16),
                   jax.ShapeDtypeStruct((batch, seq, LANES), BF16)],
        scratch_shapes=[pltpu.VMEM((LANES, WIDTH), F32)],
        compiler_params=pltpu.CompilerParams(
            dimension_semantics=("arbitrary", "arbitrary"), vmem_limit_bytes=VMEM_LIMIT),
        name="inproj",
    )(x, shift1, scale1, ln1_g, w_in_bf16, q_gain, k_gain, cos, sin_signed, ones_bd)


def _stack_heads(x):
    first = _iota(x.shape, 1) < HEAD
    return jnp.concatenate([jnp.where(first, x, 0.0), jnp.where(first, 0.0, x)], axis=0)


def _add_eye(x, eye):
    return jnp.where(eye, x + 1.0, x)


def _rwkv_kernel(z_ref, mu_ref, w0_ref, a0_ref, lora_ref, glora_ref, kk_ref, ka_ref, rk_ref,
                 lng_ref, lnb_ref, ones_ref, tri_ref, o_ref, prev_ref, state_ref):
    c = pl.program_id(1)

    @pl.when(c == 0)
    def _():
        prev_ref[...] = jnp.zeros_like(prev_ref)
        state_ref[...] = jnp.zeros_like(state_ref)

    z = z_ref[0]
    rows = z.shape[0]
    n_chunks = rows // CHUNK
    row = _iota(z.shape, 0)
    z_prev = jnp.where(row == 0, prev_ref[...], pltpu.roll(z, 1, axis=0))
    prev_ref[...] = z[rows - 1:rows, :]
    zs = z + (z_prev - z) * mu_ref[...]

    r = zs[:, 0:WIDTH]
    k = zs[:, WIDTH:2 * WIDTH]
    v = zs[:, 2 * WIDTH:3 * WIDTH]
    lo = 3 * WIDTH
    d_wa = zs[:, lo:lo + DECAY_LORA + AAA_LORA]
    d_g = zs[:, lo + DECAY_LORA + AAA_LORA:RWKV_COLS]
    is_decay = _iota(d_wa.shape, 1) < DECAY_LORA
    pre = _dot_x3(jnp.where(is_decay, jnp.tanh(d_wa), d_wa), lora_ref[...])
    log_w = -DECAY_SCALE * jax.nn.sigmoid(w0_ref[...] + pre[:, 0:WIDTH])
    a = jax.nn.sigmoid(a0_ref[...] + pre[:, WIDTH:2 * WIDTH])
    g = _dot(jax.nn.sigmoid(d_g), glora_ref[...])

    ones_bd = ones_ref[...]
    kk = k * kk_ref[...]
    kk = kk / jnp.maximum(jnp.sqrt(_seg_sum(kk * kk, ones_bd)), 1e-12)
    k = k * (1.0 + (a - 1.0) * ka_ref[...])
    bonus = _seg_sum(r * k * rk_ref[...], ones_bd, wide=False) * v

    cl = _dot_wide_rhs(tri_ref[...], log_w)
    p_end_rows = [cl[(c_i + 1) * CHUNK - 1:(c_i + 1) * CHUNK, :] for c_i in range(n_chunks)]
    cl_last = p_end_rows[0]
    chunk_of_row = _iota(cl.shape, 0) // CHUNK
    for c_i in range(1, n_chunks):
        cl_last = jnp.where(chunk_of_row == c_i, p_end_rows[c_i], cl_last)
    a_t = -kk * jnp.exp(cl - log_w)
    e_neg = jnp.exp(-cl)
    b_t = kk * a * e_neg
    k_t = k * e_neg
    r_t = r * jnp.exp(cl)
    e_end = jnp.exp(cl_last - cl)
    b_end = kk * a * e_end
    k_end = k * e_end
    p_end = jnp.exp(cl_last)

    n2 = 2 * CHUNK
    ri = _iota((n2, n2), 0)
    ci = _iota((n2, n2), 1)
    eye = ri == ci
    same_blk = (ri // SOLVE_BLOCK) == (ci // SOLVE_BLOCK)
    ri4 = _iota((2 * n2, 2 * n2), 0)
    ci4 = _iota((2 * n2, 2 * n2), 1) & (n2 - 1)
    causal4 = ((ri4 < n2) & (ri4 > ci4)) | ((ri4 >= n2) & ((ri4 - n2) >= ci4))
    zeros = jnp.zeros((n2, n2), BF16)

    all_items = [(c_i, p) for c_i in range(n_chunks) for p in range(PAIRS)]

    def tile(t, item):
        c_i, p = item
        return t[c_i * CHUNK:(c_i + 1) * CHUNK, p * LANES:(p + 1) * LANES]

    def bf16_all(xs):
        return [x.astype(BF16) for x in xs]

    def pair_dot(xs, ys, prod=_dot):
        out = []
        for i in range(0, len(xs), 2):
            z = jnp.zeros_like(ys[i])
            wide = prod(jnp.concatenate([xs[i], xs[i + 1]], axis=1),
                        jnp.concatenate([jnp.concatenate([ys[i], z], axis=1),
                                         jnp.concatenate([z, ys[i + 1]], axis=1)], axis=0))
            out += [wide[:, 0:n2], wide[:, n2:2 * n2]]
        return out

    def plus_eye(xs):
        return [_add_eye(x, eye).astype(BF16) for x in xs]

    def solve(items):
        sa = [_stack_heads(tile(a_t, it)).astype(BF16) for it in items]
        sr = [_stack_heads(tile(r_t, it)) for it in items]
        sv = [_stack_heads(tile(v, it)).astype(BF16) for it in items]
        big = [jnp.where(causal4,
                         _dot_nt(jnp.concatenate([sa[i], sr[i].astype(BF16)], axis=0),
                                 jnp.concatenate([_stack_heads(tile(b_t, it)).astype(BF16),
                                                  _stack_heads(tile(k_t, it)).astype(BF16)],
                                                 axis=0)), 0.0)
               for i, it in enumerate(items)]
        a_ab = [b[0:n2, 0:n2] for b in big]
        a_ak = bf16_all(b[0:n2, n2:2 * n2] for b in big)
        a_rbk = bf16_all(b[n2:2 * n2, :] for b in big)

        d1_f = [jnp.where(same_blk, a, 0.0) for a in a_ab]
        e1 = bf16_all(jnp.where(same_blk, 0.0, a) for a in a_ab)
        d1 = bf16_all(d1_f)
        d2_f = pair_dot(d1, d1)
        d2 = bf16_all(d2_f)
        d4_f = pair_dot(d2, d2)
        d4 = bf16_all(d4_f)
        d8_f = pair_dot(d4, d4)
        p12 = bf16_all(pair_dot(plus_eye(d1_f), plus_eye(d2_f)))
        p48 = bf16_all(pair_dot(plus_eye(d4_f), plus_eye(d8_f)))
        t_d = bf16_all(pair_dot(p12, p48))
        g1_f = pair_dot(t_d, e1)
        g1 = bf16_all(g1_f)
        g2_f = pair_dot(g1, g1)
        gx = bf16_all(pair_dot(plus_eye(g1_f), plus_eye(g2_f)))
        t_inv = bf16_all(pair_dot(gx, t_d))

        akv = bf16_all(pair_dot(a_ak, sv))
        wu = bf16_all(_dot(t, jnp.concatenate([s, x], axis=1))
                      for t, s, x in zip(t_inv, sa, akv))
        rhs = [jnp.concatenate([w, jnp.concatenate([zeros, s], axis=1)], axis=0)
               for w, s in zip(wu, sv)]
        out_c = [_dot(a, x) for a, x in zip(a_rbk, rhs)]
        end_t = [jnp.concatenate([_stack_heads(tile(b_end, it)).T,
                                  _stack_heads(tile(k_end, it)).T], axis=1).astype(BF16)
                 for it in items]
        end_c = [_dot(e, x) for e, x in zip(end_t, rhs)]
        return sr, out_c, end_c

    sr, out_c, end_c = solve(all_items)

    y_rows = []
    for c_i in range(n_chunks):
        psi_phi = []
        for p in range(PAIRS):
            i = c_i * PAIRS + p
            psi = sr[i] + out_c[i][:, 0:n2]
            pe = p_end[c_i * CHUNK:c_i * CHUNK + 1, p * LANES:(p + 1) * LANES]
            phi = jnp.where(eye, pe, 0.0) + end_c[i][:, 0:n2]
            psi_phi.append(jnp.concatenate([psi, phi], axis=0))
        both = pair_dot(psi_phi, [state_ref[p] for p in range(PAIRS)], prod=_dot_wide_rhs)
        y_cols = []
        for p in range(PAIRS):
            i = c_i * PAIRS + p
            y = both[p][0:n2, :] + out_c[i][:, n2:2 * n2]
            state_ref[p] = both[p][n2:2 * n2, :] + end_c[i][:, n2:2 * n2]
            y_cols.append(y[0:CHUNK, :] + y[CHUNK:n2, :])
        y_rows.append(jnp.concatenate(y_cols, axis=1))
    y = jnp.concatenate(y_rows, axis=0)
    mean = _seg_sum(y, ones_bd) * (1.0 / HEAD)
    yc = y - mean
    var = _seg_sum(yc * yc, ones_bd, wide=False) * (1.0 / HEAD)
    yn = yc * lax.rsqrt(var + LN_X_EPS) * lng_ref[...] + lnb_ref[...]
    o_ref[0] = ((yn + bonus) * g).astype(o_ref.dtype)


def _rwkv_call(z_rwkv, mu_shift, w0, a0, lora_cat, g_lora_up, k_k, k_a, r_k, lnx_g, lnx_b,
               ones_bd, tri):
    batch, seq, _ = z_rwkv.shape
    rows = RWKV_ROWS
    const = lambda shape: pl.BlockSpec(shape, lambda b, c: (0,) * len(shape))
    vec = const((1, WIDTH))
    return pl.pallas_call(
        _rwkv_kernel,
        grid=(batch, seq // rows),
        in_specs=[pl.BlockSpec((1, rows, RWKV_COLS), lambda b, c: (b, c, 0)),
                  const((1, RWKV_COLS)), vec, vec,
                  const((DECAY_LORA + AAA_LORA, 2 * WIDTH)), const((GATE_LORA, WIDTH)),
                  vec, vec, vec, vec, vec, const((MXU_DIM, MXU_DIM)), const((rows, rows))],
        out_specs=pl.BlockSpec((1, rows, WIDTH), lambda b, c: (b, c, 0)),
        out_shape=jax.ShapeDtypeStruct((batch, seq, WIDTH), BF16),
        scratch_shapes=[pltpu.VMEM((1, RWKV_COLS), F32),
                        pltpu.VMEM((PAIRS, LANES, LANES), F32)],
        compiler_params=pltpu.CompilerParams(
            dimension_semantics=("arbitrary", "arbitrary"), vmem_limit_bytes=VMEM_LIMIT),
        name="rwkv7",
    )(z_rwkv, mu_shift, w0, a0, lora_cat, g_lora_up, k_k, k_a, r_k, lnx_g, lnx_b, ones_bd, tri)


def _moba_kernel(q_ref, k_ref, v_ref, sel_ref, o_ref, m_ref, l_ref, acc_ref):
    qi = pl.program_id(2)
    tq = q_ref.shape[1]
    n_pairs = q_ref.shape[2] // LANES
    n_heads = 2 * n_pairs
    first = _iota((tq, LANES), 1) < HEAD
    pair_lanes = [slice(p * LANES, (p + 1) * LANES) for p in range(n_pairs)]
    qh = []
    for lanes in pair_lanes:
        q = q_ref[0, :, lanes]
        zero = jnp.zeros_like(q)
        qh += [jnp.where(first, q, zero), jnp.where(first, zero, q)]

    m_ref[...] = jnp.full(m_ref.shape, NEG_INF, F32)
    l_ref[...] = jnp.zeros(l_ref.shape, F32)
    acc_ref[...] = jnp.zeros(acc_ref.shape, F32)
    tiles = [(h, r0) for h in range(n_heads) for r0 in range(0, tq, MOBA_SUB)]
    q_tiles = [qh[h][r0:r0 + MOBA_SUB] for h, r0 in tiles]

    def update(scores, v_blks):
        n_keys = v_blks[0].shape[0]
        v_ext = [jnp.concatenate([v, jnp.ones((n_keys, LANES), BF16)], axis=1) for v in v_blks]
        refs = [(h, slice(r0, r0 + MOBA_SUB)) for h, r0 in tiles]
        m_old = [m_ref[h, rows, :] for h, rows in refs]
        l_old = [l_ref[h, rows, :] for h, rows in refs]
        acc_old = [acc_ref[h, rows, :] for h, rows in refs]
        m_new = [jnp.maximum(m, jnp.broadcast_to(jnp.max(s, axis=-1, keepdims=True),
                                                 (MOBA_SUB, LANES)))
                 for m, s in zip(m_old, scores)]
        alpha = [jnp.exp(m - n) for m, n in zip(m_old, m_new)]
        p = [jnp.exp(s - jnp.concatenate([n] * (n_keys // LANES), axis=1)).astype(BF16)
             for s, n in zip(scores, m_new)]
        pv = [jnp.dot(x, v_ext[h // 2], preferred_element_type=F32)
              for x, (h, _) in zip(p, tiles)]
        for (h, rows), m, a, l, acc, y in zip(refs, m_new, alpha, l_old, acc_old, pv):
            m_ref[h, rows, :] = m
            l_ref[h, rows, :] = a * l + y[:, LANES:2 * LANES]
            acc_ref[h, rows, :] = a * acc + y[:, 0:LANES]

    start = pl.multiple_of(qi * MOBA_BLOCK, MOBA_BLOCK)
    k_own = [k_ref[0, pl.ds(start, MOBA_BLOCK), lanes] for lanes in pair_lanes]
    v_own = [v_ref[0, pl.ds(start, MOBA_BLOCK), lanes] for lanes in pair_lanes]
    row = _iota((MOBA_SUB, MOBA_BLOCK), 0)
    col = _iota((MOBA_SUB, MOBA_BLOCK), 1)
    update([jnp.where(col <= row + r0, _dot_nt(qt, k_own[h // 2]), NEG_INF)
            for qt, (h, r0) in zip(q_tiles, tiles)], v_own)

    sel = sel_ref[0]
    sel_head = _iota(sel.shape, 1) // SEL_GROUP
    head0 = pl.program_id(1) * n_heads
    sel_h = [jnp.where(sel_head == head0 + h, sel, jnp.zeros_like(sel)) for h in range(n_heads)]
    q_ext = [jnp.concatenate([qt, sel_h[h][r0:r0 + MOBA_SUB]], axis=1)
             for qt, (h, r0) in zip(q_tiles, tiles)]
    key_blk = _iota((2 * MOBA_BLOCK, LANES), 0) // MOBA_BLOCK
    key_lane = _iota((2 * MOBA_BLOCK, LANES), 1)
    in_table = key_lane < RWKV_HEADS * SEL_GROUP

    def past_blocks(j, carry):
        start = pl.multiple_of(j * (2 * MOBA_BLOCK), 2 * MOBA_BLOCK)
        block_onehot = (in_table & ((key_lane & (SEL_GROUP - 1)) == 2 * j + key_blk)).astype(BF16)
        k_ext = [jnp.concatenate([k_ref[0, pl.ds(start, 2 * MOBA_BLOCK), lanes], block_onehot],
                                 axis=1) for lanes in pair_lanes]
        v_blks = [v_ref[0, pl.ds(start, 2 * MOBA_BLOCK), lanes] for lanes in pair_lanes]
        update([_dot_nt(qe, k_ext[h // 2]) for qe, (h, _) in zip(q_ext, tiles)], v_blks)
        return carry

    lax.fori_loop(0, (qi + 1) // 2, past_blocks, 0)

    for p, lanes in enumerate(pair_lanes):
        o_ref[0, :, lanes] = jnp.where(first, acc_ref[2 * p] / l_ref[2 * p],
                                       acc_ref[2 * p + 1] / l_ref[2 * p + 1]).astype(o_ref.dtype)


def _moba_call(q, k, v, sel_bias):
    batch, seq, _ = q.shape
    tq = MOBA_BLOCK
    width = MOBA_PAIRS * LANES
    n_heads = 2 * MOBA_PAIRS
    return pl.pallas_call(
        _moba_kernel,
        grid=(batch, PAIRS // MOBA_PAIRS, seq // tq),
        in_specs=[pl.BlockSpec((1, tq, width), lambda b, p, i: (b, i, p)),
                  pl.BlockSpec((1, seq, width), lambda b, p, i: (b, 0, p)),
                  pl.BlockSpec((1, seq, width), lambda b, p, i: (b, 0, p)),
                  pl.BlockSpec((1, tq, LANES), lambda b, p, i: (b, i, 0))],
        out_specs=pl.BlockSpec((1, tq, width), lambda b, p, i: (b, i, p)),
        out_shape=jax.ShapeDtypeStruct((batch, seq, WIDTH), BF16),
        scratch_shapes=[pltpu.VMEM((n_heads, tq, LANES), F32),
                        pltpu.VMEM((n_heads, tq, LANES), F32),
                        pltpu.VMEM((n_heads, tq, LANES), F32)],
        compiler_params=pltpu.CompilerParams(
            dimension_semantics=("arbitrary", "arbitrary", "arbitrary"),
            vmem_limit_bytes=VMEM_LIMIT),
        name="moba",
    )(q, k, v, sel_bias)


def _pack_bf16_pair(lo, hi):
    lo_bits = lax.bitcast_convert_type(lo.astype(BF16).astype(F32), jnp.uint32)
    hi_bits = lax.bitcast_convert_type(hi.astype(BF16).astype(F32), jnp.uint32)
    return (lo_bits >> 16) | (hi_bits & jnp.uint32(0xFFFF0000))


def _unpack_bf16_pair(u):
    lo = lax.bitcast_convert_type(u << 16, F32)
    hi = lax.bitcast_convert_type(u & jnp.uint32(0xFFFF0000), F32)
    return lo, hi


def _merge_kernel(x_ref, ya_ref, yb_ref, gate_ref, wa_ref, wb_ref, wo_ref, g1_ref, ln_ref,
                  scale_ref, shift_ref, wr_ref, br_ref, tri_ref, x1_ref, h2_ref, route_ref,
                  count_ref, carry_ref):
    @pl.when(pl.program_id(0) == 0)
    def _():
        carry_ref[...] = jnp.zeros_like(carry_ref)

    ya = jnp.dot(ya_ref[...], wa_ref[...], preferred_element_type=F32)
    yb = jnp.dot(yb_ref[...], wb_ref[...], preferred_element_type=F32)
    gates = gate_ref[...]
    merged = (gates[:, 0:D_MODEL].astype(F32) * ya + gates[:, D_MODEL:GATE_COLS].astype(F32) * yb)
    x1 = x_ref[...] + g1_ref[0] * jnp.dot(merged.astype(BF16), wo_ref[...],
                                          preferred_element_type=F32)
    x1_ref[...] = x1

    ms = jnp.mean(x1 * x1, axis=-1, keepdims=True)
    h2 = x1 * lax.rsqrt(ms + NORM_EPS) * ln_ref[...]
    h2 = h2 * (1.0 + scale_ref[0]) + shift_ref[0]
    half = D_MODEL // 2
    h2_ref[...] = _pack_bf16_pair(h2[:, 0:half], h2[:, half:D_MODEL])

    logits = _dot_x3(h2, wr_ref[...]) + br_ref[...]
    lane = _iota(logits.shape, 1)
    lane_f = lane.astype(F32)
    far = float(ROUTER_LANES)

    def top(vals):
        m = jnp.max(vals, axis=-1, keepdims=True)
        idx = jnp.min(jnp.where(vals == m, lane_f, far), axis=-1, keepdims=True)
        return m, idx

    grp = jnp.where(lane < N_GROUPS, logits, NEG_INF)
    g_max, g_idx = top(grp)
    p_group = 1.0 / jnp.sum(jnp.exp(grp - g_max), axis=-1, keepdims=True)

    e_lo = EXPERT_LANE0 + EXPERTS_PER_GROUP * g_idx
    in_grp = (lane_f >= e_lo) & (lane_f < e_lo + EXPERTS_PER_GROUP)
    el = jnp.where(in_grp, logits, NEG_INF)
    m1, i1 = top(el)
    m2, i2 = top(jnp.where(lane_f == i1, NEG_INF, el))
    ratio = jnp.exp(m2 - m1)
    w_first = p_group / (1.0 + ratio)
    w_second = w_first * ratio

    first = lane_f == i1
    second = lane_f == i2
    hits = (first | second).astype(BF16)
    before = carry_ref[...] + jnp.dot(tri_ref[...], hits, preferred_element_type=F32)
    rank1 = jnp.sum(jnp.where(first, before, 0.0), axis=-1, keepdims=True)
    rank2 = jnp.sum(jnp.where(second, before, 0.0), axis=-1, keepdims=True)
    carry = carry_ref[...] + jnp.sum(hits.astype(F32), axis=0, keepdims=True)
    carry_ref[...] = carry
    count_ref[...] = carry

    fields = (i1 - EXPERT_LANE0, i2 - EXPERT_LANE0, w_first, w_second, rank1, rank2)
    route = jnp.zeros(logits.shape, F32)
    for n, field in enumerate(fields):
        route = jnp.where(lane == n, field, route)
    route_ref[...] = route


R_EXPERT1, R_EXPERT2, R_WEIGHT1, R_WEIGHT2, R_RANK1, R_RANK2 = range(6)
MERGE_ROWS = 1024


def _merge_call(x2d, ya, yb, gates, w_br_rwkv, w_br_moba, w_out, gate1, ln2_g, scale2, shift2,
                w_router, b_router, tri, seq):
    tokens = x2d.shape[0]
    tm = MERGE_ROWS
    per_seq = seq // tm
    row = lambda width: pl.BlockSpec((tm, width), lambda i: (i, 0))
    per_batch = pl.BlockSpec((1, 1, D_MODEL), lambda i: (i // per_seq, 0, 0))
    const = lambda shape: pl.BlockSpec(shape, lambda i: (0,) * len(shape))
    return pl.pallas_call(
        _merge_kernel,
        grid=(tokens // tm,),
        in_specs=[row(D_MODEL), row(WIDTH), row(WIDTH), row(GATE_COLS),
                  const((WIDTH, D_MODEL)), const((WIDTH, D_MODEL)), const((D_MODEL, D_MODEL)),
                  per_batch, const((1, D_MODEL)), per_batch, per_batch,
                  const((D_MODEL, ROUTER_LANES)), const((1, ROUTER_LANES)), const((tm, tm))],
        out_specs=[row(D_MODEL), row(D_MODEL // 2), row(ROUTER_LANES),
                   const((1, ROUTER_LANES))],
        out_shape=[jax.ShapeDtypeStruct((tokens, D_MODEL), F32),
                   jax.ShapeDtypeStruct((tokens, D_MODEL // 2), jnp.uint32),
                   jax.ShapeDtypeStruct((tokens, ROUTER_LANES), F32),
                   jax.ShapeDtypeStruct((1, ROUTER_LANES), F32)],
        scratch_shapes=[pltpu.VMEM((1, ROUTER_LANES), F32)],
        compiler_params=pltpu.CompilerParams(
            dimension_semantics=("arbitrary",), vmem_limit_bytes=VMEM_LIMIT),
        name="merge",
    )(x2d, ya, yb, gates, w_br_rwkv, w_br_moba, w_out, gate1, ln2_g, scale2, shift2,
      w_router, b_router, tri)


EXPERT_ROWS = 512
DISPATCH_TOKENS = 1024
COMBINE_TOKENS = 256


def _dest_kernel(route_ref, start_ref, o_ref):
    route = route_ref[...]
    lane = _iota(route.shape, 1)
    lane_f = lane.astype(F32)
    starts = start_ref[...]
    out = jnp.zeros(route.shape, F32)
    for n, (e_lane, r_lane) in enumerate(((R_EXPERT1, R_RANK1), (R_EXPERT2, R_RANK2))):
        e_col = route[:, e_lane:e_lane + 1] + EXPERT_LANE0
        first_row = jnp.sum(jnp.where(lane_f == e_col, starts, 0.0), axis=-1, keepdims=True)
        out = jnp.where(lane == n, first_row + route[:, r_lane:r_lane + 1], out)
    o_ref[...] = out


def _dest_call(route, start_row):
    tokens = route.shape[0]
    tm = DISPATCH_TOKENS
    return pl.pallas_call(
        _dest_kernel,
        grid=(tokens // tm,),
        in_specs=[pl.BlockSpec((tm, ROUTER_LANES), lambda i: (i, 0)),
                  pl.BlockSpec((1, ROUTER_LANES), lambda i: (0, 0))],
        out_specs=pl.BlockSpec((tm, ROUTER_LANES), lambda i: (i, 0)),
        out_shape=jax.ShapeDtypeStruct((tokens, ROUTER_LANES), F32),
        name="moe_dest",
    )(route, start_row)


def _dispatch_kernel(d1_ref, d2_ref, ends_ref, h_ref, xs_hbm, zero_ref, sem, zero_sem):
    base = pl.program_id(0) * DISPATCH_TOKENS

    @pl.when(pl.program_id(0) == 0)
    def _():
        zero_ref[...] = jnp.zeros_like(zero_ref)

        def fill(e):
            end = ends_ref[e]
            start = pl.multiple_of(end - EXPERT_ROWS, EXPERT_ROWS)
            return pltpu.make_async_copy(zero_ref, xs_hbm.at[pl.ds(start, EXPERT_ROWS)],
                                         zero_sem)

        def non_empty(e):
            return ends_ref[e] > (ends_ref[e - 1] if e else 0)

        for e in range(N_EXPERTS):
            pl.when(non_empty(e))(lambda e=e: fill(e).start())
        for e in range(N_EXPERTS):
            pl.when(non_empty(e))(lambda e=e: fill(e).wait())

        def tail(t):
            start = pl.multiple_of(t * EXPERT_ROWS, EXPERT_ROWS)
            return pltpu.make_async_copy(zero_ref, xs_hbm.at[pl.ds(start, EXPERT_ROWS)],
                                         zero_sem)

        first_tail = ends_ref[N_EXPERTS - 1] // EXPERT_ROWS
        n_tiles = xs_hbm.shape[0] // EXPERT_ROWS
        lax.fori_loop(first_tail, n_tiles, lambda t, c: (tail(t).start(), c)[1], 0)
        lax.fori_loop(first_tail, n_tiles, lambda t, c: (tail(t).wait(), c)[1], 0)

    def issue(t, carry):
        src = h_ref.at[pl.ds(t, 1)]
        pltpu.make_async_copy(src, xs_hbm.at[pl.ds(d1_ref[base + t], 1)], sem).start()
        pltpu.make_async_copy(src, xs_hbm.at[pl.ds(d2_ref[base + t], 1)], sem).start()
        return carry

    lax.fori_loop(0, DISPATCH_TOKENS, issue, 0, unroll=8)
    for _ in range(2):
        pltpu.make_async_copy(h_ref, xs_hbm.at[pl.ds(0, DISPATCH_TOKENS)], sem).wait()


def _dispatch_call(dest1, dest2, ends, h2p, n_rows):
    tokens = h2p.shape[0]
    half = D_MODEL // 2
    return pl.pallas_call(
        _dispatch_kernel,
        grid_spec=pltpu.PrefetchScalarGridSpec(
            num_scalar_prefetch=3,
            grid=(tokens // DISPATCH_TOKENS,),
            in_specs=[pl.BlockSpec((DISPATCH_TOKENS, half), lambda i, d1, d2, ends: (i, 0))],
            out_specs=pl.BlockSpec(memory_space=pl.ANY),
            scratch_shapes=[pltpu.VMEM((EXPERT_ROWS, half), jnp.uint32),
                            pltpu.SemaphoreType.DMA(()), pltpu.SemaphoreType.DMA(())]),
        out_shape=jax.ShapeDtypeStruct((n_rows, half), jnp.uint32),
        compiler_params=pltpu.CompilerParams(dimension_semantics=("arbitrary",)),
        name="moe_dispatch",
    )(dest1, dest2, ends, h2p)


def _expert_kernel(te_ref, nu_ref, xs_ref, w1_ref, w3_ref, w2_ref, ys_ref, w1b_ref, w3b_ref,
                   w2b_ref):
    half = D_MODEL // 2
    j = pl.program_id(0)

    @pl.when((j == 0) | (te_ref[j] != te_ref[jnp.maximum(j - 1, 0)]))
    def _():
        w1b_ref[...] = w1_ref[0].astype(BF16)
        w3b_ref[...] = w3_ref[0].astype(BF16)
        w2b_ref[...] = w2_ref[0].astype(BF16)

    @pl.when(j < nu_ref[0])
    def _():
        x_lo, x_hi = _unpack_bf16_pair(xs_ref[...])
        x_lo = x_lo.astype(BF16)
        x_hi = x_hi.astype(BF16)

        def proj(w_ref):
            return (jnp.dot(x_lo, w_ref[0:half, :], preferred_element_type=F32)
                    + jnp.dot(x_hi, w_ref[half:D_MODEL, :], preferred_element_type=F32))

        a = proj(w1b_ref)
        hid = (a * jax.nn.sigmoid(a) * proj(w3b_ref)).astype(BF16)
        y = jnp.dot(hid, w2b_ref[...], preferred_element_type=F32)
        ys_ref[...] = _pack_bf16_pair(y[:, 0:half], y[:, half:D_MODEL])

    @pl.when(j >= nu_ref[0])
    def _():
        ys_ref[...] = jnp.zeros_like(ys_ref)


def _expert_call(tile_expert, n_used, xs, w1, w3, w2):
    n_rows = xs.shape[0]
    half = D_MODEL // 2
    w_spec = lambda shape: pl.BlockSpec((1,) + shape, lambda j, te, nu: (te[j], 0, 0))
    return pl.pallas_call(
        _expert_kernel,
        grid_spec=pltpu.PrefetchScalarGridSpec(
            num_scalar_prefetch=2,
            grid=(n_rows // EXPERT_ROWS,),
            in_specs=[pl.BlockSpec((EXPERT_ROWS, half),
                                   lambda j, te, nu: (jnp.minimum(j, nu[0] - 1), 0)),
                      w_spec((D_MODEL, D_EXPERT)), w_spec((D_MODEL, D_EXPERT)),
                      w_spec((D_EXPERT, D_MODEL))],
            out_specs=pl.BlockSpec((EXPERT_ROWS, half), lambda j, te, nu: (j, 0)),
            scratch_shapes=[pltpu.VMEM((D_MODEL, D_EXPERT), BF16),
                            pltpu.VMEM((D_MODEL, D_EXPERT), BF16),
                            pltpu.VMEM((D_EXPERT, D_MODEL), BF16)]),
        out_shape=jax.ShapeDtypeStruct((n_rows, half), jnp.uint32),
        compiler_params=pltpu.CompilerParams(
            dimension_semantics=("arbitrary",), vmem_limit_bytes=VMEM_LIMIT),
        name="moe_experts",
    )(tile_expert, n_used, xs, w1, w3, w2)


def _combine_kernel(d1_ref, d2_ref, ys_hbm, x1_ref, route_ref, g2_ref, o_ref, buf_ref, sem):
    i = pl.program_id(0)
    n_steps = pl.num_programs(0)
    tc = COMBINE_TOKENS
    half = D_MODEL // 2

    groups = tc // SUBLANES

    def issue(step, slot):
        base = step * tc

        def one_group(grp, carry):
            for r in range(SUBLANES):
                tok = base + grp * SUBLANES + r
                d1 = d1_ref[tok]
                d2 = d2_ref[tok]
                pltpu.make_async_copy(ys_hbm.at[d1 >> 3, pl.ds(d1 & (SUBLANES - 1), 1)],
                                      buf_ref.at[slot, grp, pl.ds(r, 1)], sem.at[slot]).start()
                pltpu.make_async_copy(ys_hbm.at[d2 >> 3, pl.ds(d2 & (SUBLANES - 1), 1)],
                                      buf_ref.at[slot, groups + grp, pl.ds(r, 1)],
                                      sem.at[slot]).start()
            return carry

        lax.fori_loop(0, groups, one_group, 0)

    slot = i % 2

    @pl.when(i == 0)
    def _():
        issue(0, 0)

    @pl.when(i + 1 < n_steps)
    def _():
        issue(i + 1, 1 - slot)

    pltpu.make_async_copy(ys_hbm.at[pl.ds(0, 2 * groups)], buf_ref.at[slot], sem.at[slot]).wait()

    rows = buf_ref[slot].reshape(2 * tc, half)
    a_lo, a_hi = _unpack_bf16_pair(rows[0:tc])
    b_lo, b_hi = _unpack_bf16_pair(rows[tc:2 * tc])
    route = route_ref[...]
    w_a = route[:, R_WEIGHT1:R_WEIGHT1 + 1]
    w_b = route[:, R_WEIGHT2:R_WEIGHT2 + 1]
    g2 = g2_ref[0]
    o_ref[:, 0:half] = x1_ref[:, 0:half] + g2[:, 0:half] * (w_a * a_lo + w_b * b_lo)
    o_ref[:, half:D_MODEL] = (x1_ref[:, half:D_MODEL]
                              + g2[:, half:D_MODEL] * (w_a * a_hi + w_b * b_hi))


def _combine_call(dest1, dest2, ys, x1, route, gate2, seq):
    tokens = x1.shape[0]
    tc = COMBINE_TOKENS
    per_seq = seq // tc
    half = D_MODEL // 2
    return pl.pallas_call(
        _combine_kernel,
        grid_spec=pltpu.PrefetchScalarGridSpec(
            num_scalar_prefetch=2,
            grid=(tokens // tc,),
            in_specs=[pl.BlockSpec(memory_space=pl.ANY),
                      pl.BlockSpec((tc, D_MODEL), lambda i, d1, d2: (i, 0)),
                      pl.BlockSpec((tc, ROUTER_LANES), lambda i, d1, d2: (i, 0)),
                      pl.BlockSpec((1, 1, D_MODEL), lambda i, d1, d2: (i // per_seq, 0, 0))],
            out_specs=pl.BlockSpec((tc, D_MODEL), lambda i, d1, d2: (i, 0)),
            scratch_shapes=[pltpu.VMEM((2, 2 * tc // SUBLANES, SUBLANES, half), jnp.uint32),
                            pltpu.SemaphoreType.DMA((2,))]),
        out_shape=jax.ShapeDtypeStruct((tokens, D_MODEL), F32),
        compiler_params=pltpu.CompilerParams(
            dimension_semantics=("arbitrary",), vmem_limit_bytes=VMEM_LIMIT),
        name="moe_combine",
    )(dest1, dest2, ys.reshape(-1, SUBLANES, half), x1, route, gate2)


def _moe(h2p, route, counts, x1, gate2, w1, w3, w2, seq):
    tokens = h2p.shape[0]
    n_rows = 2 * tokens + N_EXPERTS * EXPERT_ROWS
    n_rows -= n_rows % EXPERT_ROWS
    n_tiles = n_rows // EXPERT_ROWS

    count = counts[0, EXPERT_LANE0:EXPERT_LANE0 + N_EXPERTS].astype(jnp.int32)
    padded = (count + EXPERT_ROWS - 1) // EXPERT_ROWS * EXPERT_ROWS
    ends = jnp.cumsum(padded)
    starts = ends - padded
    tile_start = jnp.arange(n_tiles, dtype=jnp.int32) * EXPERT_ROWS
    tile_expert = jnp.minimum(jnp.sum(ends[None, :] <= tile_start[:, None], axis=1),
                              N_EXPERTS - 1).astype(jnp.int32)
    n_used = (ends[-1:] // EXPERT_ROWS).astype(jnp.int32)

    start_row = jnp.zeros((1, ROUTER_LANES), F32)
    start_row = start_row.at[0, EXPERT_LANE0:EXPERT_LANE0 + N_EXPERTS].set(starts.astype(F32))
    dest = _dest_call(route, start_row)[:, 0:2].astype(jnp.int32)
    dest1, dest2 = dest[:, 0], dest[:, 1]

    xs = _dispatch_call(dest1, dest2, ends.astype(jnp.int32), h2p, n_rows)
    ys = _expert_call(tile_expert, n_used, xs, w1, w3, w2)
    return _combine_call(dest1, dest2, ys, x1, route, gate2, seq)


def _rope_tables(seq):
    half = HEAD // 2
    inv_freq = ROPE_THETA ** (-jnp.arange(half, dtype=F32) / half)
    ang = jnp.arange(seq, dtype=F32)[:, None] * inv_freq[None, :]
    cos = jnp.cos(ang)
    sin = jnp.sin(ang)
    cos_head = jnp.concatenate([cos, cos], axis=1)
    sin_head = jnp.concatenate([-sin, sin], axis=1)
    return jnp.tile(cos_head, (1, RWKV_HEADS)), jnp.tile(sin_head, (1, RWKV_HEADS))


def _layer(x, c, w_ada, b_ada, ln1_g, ln2_g, w_in, mu_shift, w0, w_lora_up, a0, a_lora_up,
           g_lora_up, k_k, k_a, r_k, lnx_g, lnx_b, q_norm_g, k_norm_g, w_br_rwkv, w_br_moba,
           w_out, w_rg, b_rg, w_re, b_re, w1, w3, w2):
    batch, seq, _ = x.shape
    vec = lambda a: a.reshape(1, -1)

    mod = _mod_call(c, w_ada, b_ada)
    shift1, scale1, gate1, shift2, scale2, gate2 = (
        m.reshape(batch, 1, D_MODEL) for m in jnp.split(mod, 6, axis=-1))

    idx = np.arange(MXU_DIM)
    ones_bd = jnp.asarray(idx[:, None] // HEAD == idx[None, :] // HEAD, BF16)
    t_idx = np.arange(RWKV_ROWS)
    tri = jnp.asarray((t_idx[:, None] >= t_idx[None, :])
                      & (t_idx[:, None] // CHUNK == t_idx[None, :] // CHUNK), BF16)
    cos, sin_signed = _rope_tables(seq)

    z_rwkv, q, k, v, gates, sel_bias = _inproj_call(
        x, shift1, scale1, vec(ln1_g), w_in.astype(BF16),
        vec(jnp.tile(q_norm_g, RWKV_HEADS)), vec(jnp.tile(k_norm_g, RWKV_HEADS)),
        cos, sin_signed, ones_bd)

    lora_cat = jnp.zeros((DECAY_LORA + AAA_LORA, 2 * WIDTH), F32)
    lora_cat = lora_cat.at[:DECAY_LORA, :WIDTH].set(w_lora_up).at[DECAY_LORA:, WIDTH:].set(a_lora_up)
    y_a = _rwkv_call(z_rwkv, vec(mu_shift), vec(w0), vec(a0), lora_cat, g_lora_up, vec(k_k),
                     vec(k_a), vec(r_k), vec(lnx_g), vec(lnx_b), ones_bd, tri)

    y_b = _moba_call(q, k, v, sel_bias)

    tokens = batch * seq
    w_router = jnp.zeros((D_MODEL, ROUTER_LANES), F32)
    w_router = w_router.at[:, :N_GROUPS].set(w_rg).at[:, EXPERT_LANE0:EXPERT_LANE0 + N_EXPERTS].set(w_re)
    b_router = jnp.zeros((1, ROUTER_LANES), F32)
    b_router = b_router.at[0, :N_GROUPS].set(b_rg).at[0, EXPERT_LANE0:EXPERT_LANE0 + N_EXPERTS].set(b_re)
    m_idx = np.arange(MERGE_ROWS)
    tri_strict = jnp.asarray(m_idx[:, None] > m_idx[None, :], BF16)
    x1, h2p, route, counts = _merge_call(
        x.reshape(tokens, D_MODEL), y_a.reshape(tokens, WIDTH), y_b.reshape(tokens, WIDTH),
        gates.reshape(tokens, GATE_COLS), w_br_rwkv.astype(BF16), w_br_moba.astype(BF16),
        w_out.astype(BF16), gate1, vec(ln2_g), scale2, shift2, w_router, b_router, tri_strict,
        seq)

    out = _moe(h2p, route, counts, x1, gate2, w1, w3, w2, seq)
    return out.reshape(batch, seq, D_MODEL)


def kernel(x, c, w_ada, b_ada, ln1_g, ln2_g, w_in, mu_shift, w0, w_lora_up, a0, a_lora_up,
           g_lora_up, k_k, k_a, r_k, lnx_g, lnx_b, q_norm_g, k_norm_g, w_br_rwkv, w_br_moba,
           w_out, w_rg, b_rg, w_re, b_re, w1, w3, w2):
    assert w_ada.shape[0] == 1, "single-layer problem"
    layer_params = (w_ada, b_ada, ln1_g, ln2_g, w_in, mu_shift, w0, w_lora_up, a0, a_lora_up,
                    g_lora_up, k_k, k_a, r_k, lnx_g, lnx_b, q_norm_g, k_norm_g, w_br_rwkv,
                    w_br_moba, w_out, w_rg, b_rg, w_re, b_re, w1, w3, w2)
    return _layer(x, c, *(p[0] for p in layer_params))
```

```python
import jax
import jax.numpy as jnp
import numpy as np
from jax import lax
from jax.experimental import pallas as pl
from jax.experimental.pallas import tpu as pltpu

F32 = jnp.float32
BF16 = jnp.bfloat16
HIGHEST = lax.Precision.HIGHEST

D_MODEL = 1024
RWKV_HEADS = 8
HEAD = 64
WIDTH = RWKV_HEADS * HEAD
DECAY_LORA = 64
AAA_LORA = 64
GATE_LORA = 128
RWKV_COLS = 3 * WIDTH + DECAY_LORA + AAA_LORA + GATE_LORA
ATT_COLS = 3 * WIDTH
GATE_COLS = 2 * D_MODEL
IN_COLS = RWKV_COLS + ATT_COLS + GATE_COLS
DECAY_SCALE = 0.606531
LN_X_EPS = 64e-5
MOBA_BLOCK = 256
MOBA_TOPK = 3
ROPE_THETA = 10000.0
N_GROUPS = 4
EXPERTS_PER_GROUP = 8
N_EXPERTS = N_GROUPS * EXPERTS_PER_GROUP
D_EXPERT = D_MODEL // 2
NORM_EPS = 1e-6
NEG_INF = -1e30

LANES = 128
MXU_DIM = 256
SUBLANES = 8
PAIRS = WIDTH // LANES
CHUNK = 64
SOLVE_BLOCK = 16
RWKV_ROWS = 4 * CHUNK
MOBA_SUB = 256
INPROJ_ROWS = MOBA_BLOCK
MOBA_PAIRS = 4
VMEM_LIMIT = 56 * 1024 * 1024

ROUTER_LANES = LANES
EXPERT_LANE0 = N_GROUPS


def _dot(a, b):
    return jnp.dot(a.astype(BF16), b.astype(BF16), preferred_element_type=F32)


def _dot_nt(a, b):
    return lax.dot_general(a.astype(BF16), b.astype(BF16), (((1,), (1,)), ((), ())),
                           preferred_element_type=F32)


def _dot_f32(a, b):
    return jnp.dot(a, b, precision=HIGHEST, preferred_element_type=F32)


def _split_bf16(a):
    hi = a.astype(BF16)
    return hi, (a - hi.astype(F32)).astype(BF16)


def _dot_x3(a, b):
    a_hi, a_lo = _split_bf16(a)
    b_hi, b_lo = _split_bf16(b)
    return (jnp.dot(a_hi, b_hi, preferred_element_type=F32)
            + jnp.dot(a_hi, b_lo, preferred_element_type=F32)
            + jnp.dot(a_lo, b_hi, preferred_element_type=F32))


def _dot_wide_rhs(a, b):
    a = a.astype(BF16)
    hi, lo = _split_bf16(b)
    return (jnp.dot(a, hi, preferred_element_type=F32)
            + jnp.dot(a, lo, preferred_element_type=F32))


def _seg_sum(x, ones_bd, wide=True):
    group = ones_bd.shape[0]
    parts = _split_bf16(x) if wide else (x.astype(BF16),)
    cols = []
    for c0 in range(0, x.shape[1], group):
        cols.append(sum(jnp.dot(part[:, c0:c0 + group], ones_bd, preferred_element_type=F32)
                        for part in parts))
    return jnp.concatenate(cols, axis=1)


def _iota(shape, axis):
    return lax.broadcasted_iota(jnp.int32, shape, axis)


def _mod_kernel(c_ref, w_ref, b_ref, o_ref):
    c = c_ref[...]
    o_ref[...] = _dot_f32(c * jax.nn.sigmoid(c), w_ref[...]) + b_ref[...]


def _mod_call(c, w_ada, b_ada):
    batch = c.shape[0]
    n_out = w_ada.shape[1]
    tn = D_MODEL
    return pl.pallas_call(
        _mod_kernel,
        grid=(n_out // tn,),
        in_specs=[pl.BlockSpec((batch, D_MODEL), lambda j: (0, 0)),
                  pl.BlockSpec((D_MODEL, tn), lambda j: (0, j)),
                  pl.BlockSpec((1, tn), lambda j: (0, j))],
        out_specs=pl.BlockSpec((batch, tn), lambda j: (0, j)),
        out_shape=jax.ShapeDtypeStruct((batch, n_out), F32),
        name="adaln_mod",
    )(c, w_ada, b_ada.reshape(1, n_out))


def _swap_halves(x):
    first = (_iota(x.shape, 1) & (HEAD - 1)) < HEAD // 2
    up = pltpu.roll(x, LANES - HEAD // 2, axis=1)
    down = pltpu.roll(x, HEAD // 2, axis=1)
    return jnp.where(first, up, down)


def _head_norm_rope(x, gain, cos, sin_signed, ones_bd):
    ms = _seg_sum(x * x, ones_bd, wide=False) * (1.0 / HEAD)
    y = x * lax.rsqrt(ms + NORM_EPS) * gain
    cols = []
    for p in range(PAIRS):
        sl = slice(p * LANES, (p + 1) * LANES)
        yb = y[:, sl]
        cols.append(yb * cos[:, sl] + _swap_halves(yb) * sin_signed[:, sl])
    return jnp.concatenate(cols, axis=1)


SEL_GROUP = 8


def _block_bias(q, km_ref, blk_idx):
    q_hi, q_lo = _split_bf16(q)
    km_hi, km_lo = _split_bf16(km_ref[...])
    nt = lambda a, b: lax.dot_general(a, b, (((1,), (1,)), ((), ())), preferred_element_type=F32)
    gate = nt(q_hi, km_hi) + nt(q_hi, km_lo) + nt(q_lo, km_hi)
    lane = _iota(gate.shape, 1)
    blk = lane & (SEL_GROUP - 1)
    valid = (blk < blk_idx) & (lane < RWKV_HEADS * SEL_GROUP)
    g = jnp.where(valid, gate, NEG_INF)
    rank = jnp.zeros(gate.shape, F32)
    for shift in range(1, SEL_GROUP):
        wrapped = blk + shift >= SEL_GROUP
        partner = jnp.where(wrapped, pltpu.roll(g, SEL_GROUP - shift, axis=1),
                            pltpu.roll(g, LANES - shift, axis=1))
        ahead = (partner > g) | ((partner == g) & wrapped)
        rank = rank + ahead.astype(F32)
    return jnp.where(valid & (rank < MOBA_TOPK), 0.0, NEG_INF)


def _inproj_kernel(x_ref, shift_ref, scale_ref, g_ref, w_ref, qg_ref, kg_ref, cos_ref, sin_ref,
                   ones_ref, zr_ref, q_ref, k_ref, v_ref, gate_ref, bias_ref, km_ref):
    i = pl.program_id(1)

    @pl.when(i == 0)
    def _():
        km_ref[...] = jnp.zeros_like(km_ref)

    x = x_ref[0]
    ms = jnp.mean(x * x, axis=-1, keepdims=True)
    h = x * lax.rsqrt(ms + NORM_EPS) * g_ref[...]
    h = (h * (1.0 + scale_ref[0]) + shift_ref[0]).astype(BF16)

    zr_ref[0] = jnp.dot(h, w_ref[:, 0:RWKV_COLS], preferred_element_type=F32)

    za = jnp.dot(h, w_ref[:, RWKV_COLS:RWKV_COLS + ATT_COLS], preferred_element_type=F32)
    ones_bd = ones_ref[...]
    cos = cos_ref[...]
    sin = sin_ref[...]
    q = _head_norm_rope(za[:, 0:WIDTH], qg_ref[...], cos, sin, ones_bd)
    k = _head_norm_rope(za[:, WIDTH:2 * WIDTH], kg_ref[...], cos, sin, ones_bd)
    q_ref[0] = (q * (HEAD ** -0.5)).astype(BF16)
    k_ref[0] = k.astype(BF16)
    v_ref[0] = za[:, 2 * WIDTH:3 * WIDTH].astype(BF16)

    head_of_lane = _iota((1, WIDTH), 1) // HEAD
    for sub in range(x.shape[0] // MOBA_BLOCK):
        rows = slice(sub * MOBA_BLOCK, (sub + 1) * MOBA_BLOCK)
        blk = i * (x.shape[0] // MOBA_BLOCK) + sub
        bias_ref[0, rows, :] = _block_bias(q[rows], km_ref, blk).astype(BF16)
        k_mean = jnp.mean(k[rows], axis=0, keepdims=True)
        for head in range(RWKV_HEADS):
            km_ref[pl.ds(head * SEL_GROUP + blk, 1), :] = jnp.where(head_of_lane == head,
                                                                   k_mean, 0.0)

    zg = jnp.dot(h, w_ref[:, RWKV_COLS + ATT_COLS:IN_COLS], preferred_element_type=F32)
    gate_ref[0] = jax.nn.sigmoid(zg).astype(BF16)


def _inproj_call(x, shift1, scale1, ln1_g, w_in_bf16, q_gain, k_gain, cos, sin_signed, ones_bd):
    batch, seq, _ = x.shape
    tm = INPROJ_ROWS
    n_t = seq // tm
    row = lambda width: pl.BlockSpec((1, tm, width), lambda b, i: (b, i, 0))
    per_batch = pl.BlockSpec((1, 1, D_MODEL), lambda b, i: (b, 0, 0))
    const = lambda shape: pl.BlockSpec(shape, lambda b, i: (0,) * len(shape))
    return pl.pallas_call(
        _inproj_kernel,
        grid=(batch, n_t),
        in_specs=[row(D_MODEL), per_batch, per_batch, const((1, D_MODEL)),
                  const((D_MODEL, IN_COLS)), const((1, WIDTH)), const((1, WIDTH)),
                  pl.BlockSpec((tm, WIDTH), lambda b, i: (i, 0)),
                  pl.BlockSpec((tm, WIDTH), lambda b, i: (i, 0)),
                  const((MXU_DIM, MXU_DIM))],
        out_specs=[row(RWKV_COLS), row(WIDTH), row(WIDTH), row(WIDTH), row(GATE_COLS),
                   row(LANES)],
        out_shape=[jax.ShapeDtypeStruct((batch, seq, RWKV_COLS), F32),
                   jax.ShapeDtypeStruct((batch, seq, WIDTH), BF16),
                   jax.ShapeDtypeStruct((batch, seq, WIDTH), BF16),
                   jax.ShapeDtypeStruct((batch, seq, WIDTH), BF16),
                   jax.ShapeDtypeStruct((batch, seq, GATE_COLS), BF16),
                   jax.ShapeDtypeStruct((batch, seq, LANES), BF16)],
        scratch_shapes=[pltpu.VMEM((LANES, WIDTH), F32)],
        compiler_params=pltpu.CompilerParams(
            dimension_semantics=("arbitrary", "arbitrary"), vmem_limit_bytes=VMEM_LIMIT),
        name="inproj",
    )(x, shift1, scale1, ln1_g, w_in_bf16, q_gain, k_gain, cos, sin_signed, ones_bd)


def _stack_heads(x):
    first = _iota(x.shape, 1) < HEAD
    return jnp.concatenate([jnp.where(first, x, 0.0), jnp.where(first, 0.0, x)], axis=0)


def _add_eye(x, eye):
    return jnp.where(eye, x + 1.0, x)


def _rwkv_kernel(z_ref, mu_ref, w0_ref, a0_ref, lora_ref, glora_ref, kk_ref, ka_ref, rk_ref,
                 lng_ref, lnb_ref, ones_ref, tri_ref, o_ref, prev_ref, state_ref):
    c = pl.program_id(1)

    @pl.when(c == 0)
    def _():
        prev_ref[...] = jnp.zeros_like(prev_ref)
        state_ref[...] = jnp.zeros_like(state_ref)

    z = z_ref[0]
    rows = z.shape[0]
    n_chunks = rows // CHUNK
    row = _iota(z.shape, 0)
    z_prev = jnp.where(row == 0, prev_ref[...], pltpu.roll(z, 1, axis=0))
    prev_ref[...] = z[rows - 1:rows, :]
    zs = z + (z_prev - z) * mu_ref[...]

    r = zs[:, 0:WIDTH]
    k = zs[:, WIDTH:2 * WIDTH]
    v = zs[:, 2 * WIDTH:3 * WIDTH]
    lo = 3 * WIDTH
    d_wa = zs[:, lo:lo + DECAY_LORA + AAA_LORA]
    d_g = zs[:, lo + DECAY_LORA + AAA_LORA:RWKV_COLS]
    is_decay = _iota(d_wa.shape, 1) < DECAY_LORA
    pre = _dot_x3(jnp.where(is_decay, jnp.tanh(d_wa), d_wa), lora_ref[...])
    log_w = -DECAY_SCALE * jax.nn.sigmoid(w0_ref[...] + pre[:, 0:WIDTH])
    a = jax.nn.sigmoid(a0_ref[...] + pre[:, WIDTH:2 * WIDTH])
    g = _dot(jax.nn.sigmoid(d_g), glora_ref[...])

    ones_bd = ones_ref[...]
    kk = k * kk_ref[...]
    kk = kk / jnp.maximum(jnp.sqrt(_seg_sum(kk * kk, ones_bd)), 1e-12)
    k = k * (1.0 + (a - 1.0) * ka_ref[...])
    bonus = _seg_sum(r * k * rk_ref[...], ones_bd, wide=False) * v

    cl = _dot_wide_rhs(tri_ref[...], log_w)
    p_end_rows = [cl[(c_i + 1) * CHUNK - 1:(c_i + 1) * CHUNK, :] for c_i in range(n_chunks)]
    cl_last = p_end_rows[0]
    chunk_of_row = _iota(cl.shape, 0) // CHUNK
    for c_i in range(1, n_chunks):
        cl_last = jnp.where(chunk_of_row == c_i, p_end_rows[c_i], cl_last)
    a_t = -kk * jnp.exp(cl - log_w)
    e_neg = jnp.exp(-cl)
    b_t = kk * a * e_neg
    k_t = k * e_neg
    r_t = r * jnp.exp(cl)
    e_end = jnp.exp(cl_last - cl)
    b_end = kk * a * e_end
    k_end = k * e_end
    p_end = jnp.exp(cl_last)

    n2 = 2 * CHUNK
    ri = _iota((n2, n2), 0)
    ci = _iota((n2, n2), 1)
    eye = ri == ci
    same_blk = (ri // SOLVE_BLOCK) == (ci // SOLVE_BLOCK)
    ri4 = _iota((2 * n2, 2 * n2), 0)
    ci4 = _iota((2 * n2, 2 * n2), 1) & (n2 - 1)
    causal4 = ((ri4 < n2) & (ri4 > ci4)) | ((ri4 >= n2) & ((ri4 - n2) >= ci4))
    zeros = jnp.zeros((n2, n2), BF16)

    all_items = [(c_i, p) for c_i in range(n_chunks) for p in range(PAIRS)]

    def tile(t, item):
        c_i, p = item
        return t[c_i * CHUNK:(c_i + 1) * CHUNK, p * LANES:(p + 1) * LANES]

    def bf16_all(xs):
        return [x.astype(BF16) for x in xs]

    def pair_dot(xs, ys, prod=_dot):
        out = []
        for i in range(0, len(xs), 2):
            z = jnp.zeros_like(ys[i])
            wide = prod(jnp.concatenate([xs[i], xs[i + 1]], axis=1),
                        jnp.concatenate([jnp.concatenate([ys[i], z], axis=1),
                                         jnp.concatenate([z, ys[i + 1]], axis=1)], axis=0))
            out += [wide[:, 0:n2], wide[:, n2:2 * n2]]
        return out

    def plus_eye(xs):
        return [_add_eye(x, eye).astype(BF16) for x in xs]

    def solve(items):
        sa = [_stack_heads(tile(a_t, it)).astype(BF16) for it in items]
        sr = [_stack_heads(tile(r_t, it)) for it in items]
        sv = [_stack_heads(tile(v, it)).astype(BF16) for it in items]
        big = [jnp.where(causal4,
                         _dot_nt(jnp.concatenate([sa[i], sr[i].astype(BF16)], axis=0),
                                 jnp.concatenate([_stack_heads(tile(b_t, it)).astype(BF16),
                                                  _stack_heads(tile(k_t, it)).astype(BF16)],
                                                 axis=0)), 0.0)
               for i, it in enumerate(items)]
        a_ab = [b[0:n2, 0:n2] for b in big]
        a_ak = bf16_all(b[0:n2, n2:2 * n2] for b in big)
        a_rbk = bf16_all(b[n2:2 * n2, :] for b in big)

        d1_f = [jnp.where(same_blk, a, 0.0) for a in a_ab]
        e1 = bf16_all(jnp.where(same_blk, 0.0, a) for a in a_ab)
        d1 = bf16_all(d1_f)
        d2_f = pair_dot(d1, d1)
        d2 = bf16_all(d2_f)
        d4_f = pair_dot(d2, d2)
        d4 = bf16_all(d4_f)
        d8_f = pair_dot(d4, d4)
        p12 = bf16_all(pair_dot(plus_eye(d1_f), plus_eye(d2_f)))
        p48 = bf16_all(pair_dot(plus_eye(d4_f), plus_eye(d8_f)))
        t_d = bf16_all(pair_dot(p12, p48))
        g1_f = pair_dot(t_d, e1)
        g1 = bf16_all(g1_f)
        g2_f = pair_dot(g1, g1)
        gx = bf16_all(pair_dot(plus_eye(g1_f), plus_eye(g2_f)))
        t_inv = bf16_all(pair_dot(gx, t_d))

        akv = bf16_all(pair_dot(a_ak, sv))
        wu = bf16_all(_dot(t, jnp.concatenate([s, x], axis=1))
                      for t, s, x in zip(t_inv, sa, akv))
        rhs = [jnp.concatenate([w, jnp.concatenate([zeros, s], axis=1)], axis=0)
               for w, s in zip(wu, sv)]
        out_c = [_dot(a, x) for a, x in zip(a_rbk, rhs)]
        end_t = [jnp.concatenate([_stack_heads(tile(b_end, it)).T,
                                  _stack_heads(tile(k_end, it)).T], axis=1).astype(BF16)
                 for it in items]
        end_c = [_dot(e, x) for e, x in zip(end_t, rhs)]
        return sr, out_c, end_c

    sr, out_c, end_c = solve(all_items)

    y_rows = []
    for c_i in range(n_chunks):
        psi_phi = []
        for p in range(PAIRS):
            i = c_i * PAIRS + p
            psi = sr[i] + out_c[i][:, 0:n2]
            pe = p_end[c_i * CHUNK:c_i * CHUNK + 1, p * LANES:(p + 1) * LANES]
            phi = jnp.where(eye, pe, 0.0) + end_c[i][:, 0:n2]
            psi_phi.append(jnp.concatenate([psi, phi], axis=0))
        both = pair_dot(psi_phi, [state_ref[p] for p in range(PAIRS)], prod=_dot_wide_rhs)
        y_cols = []
        for p in range(PAIRS):
            i = c_i * PAIRS + p
            y = both[p][0:n2, :] + out_c[i][:, n2:2 * n2]
            state_ref[p] = both[p][n2:2 * n2, :] + end_c[i][:, n2:2 * n2]
            y_cols.append(y[0:CHUNK, :] + y[CHUNK:n2, :])
        y_rows.append(jnp.concatenate(y_cols, axis=1))
    y = jnp.concatenate(y_rows, axis=0)
    mean = _seg_sum(y, ones_bd) * (1.0 / HEAD)
    yc = y - mean
    var = _seg_sum(yc * yc, ones_bd, wide=False) * (1.0 / HEAD)
    yn = yc * lax.rsqrt(var + LN_X_EPS) * lng_ref[...] + lnb_ref[...]
    o_ref[0] = ((yn + bonus) * g).astype(o_ref.dtype)


def _rwkv_call(z_rwkv, mu_shift, w0, a0, lora_cat, g_lora_up, k_k, k_a, r_k, lnx_g, lnx_b,
               ones_bd, tri):
    batch, seq, _ = z_rwkv.shape
    rows = RWKV_ROWS
    const = lambda shape: pl.BlockSpec(shape, lambda b, c: (0,) * len(shape))
    vec = const((1, WIDTH))
    return pl.pallas_call(
        _rwkv_kernel,
        grid=(batch, seq // rows),
        in_specs=[pl.BlockSpec((1, rows, RWKV_COLS), lambda b, c: (b, c, 0)),
                  const((1, RWKV_COLS)), vec, vec,
                  const((DECAY_LORA + AAA_LORA, 2 * WIDTH)), const((GATE_LORA, WIDTH)),
                  vec, vec, vec, vec, vec, const((MXU_DIM, MXU_DIM)), const((rows, rows))],
        out_specs=pl.BlockSpec((1, rows, WIDTH), lambda b, c: (b, c, 0)),
        out_shape=jax.ShapeDtypeStruct((batch, seq, WIDTH), BF16),
        scratch_shapes=[pltpu.VMEM((1, RWKV_COLS), F32),
                        pltpu.VMEM((PAIRS, LANES, LANES), F32)],
        compiler_params=pltpu.CompilerParams(
            dimension_semantics=("arbitrary", "arbitrary"), vmem_limit_bytes=VMEM_LIMIT),
        name="rwkv7",
    )(z_rwkv, mu_shift, w0, a0, lora_cat, g_lora_up, k_k, k_a, r_k, lnx_g, lnx_b, ones_bd, tri)


def _moba_kernel(q_ref, k_ref, v_ref, sel_ref, o_ref, m_ref, l_ref, acc_ref):
    qi = pl.program_id(2)
    tq = q_ref.shape[1]
    n_pairs = q_ref.shape[2] // LANES
    n_heads = 2 * n_pairs
    first = _iota((tq, LANES), 1) < HEAD
    pair_lanes = [slice(p * LANES, (p + 1) * LANES) for p in range(n_pairs)]
    qh = []
    for lanes in pair_lanes:
        q = q_ref[0, :, lanes]
        zero = jnp.zeros_like(q)
        qh += [jnp.where(first, q, zero), jnp.where(first, zero, q)]

    m_ref[...] = jnp.full(m_ref.shape, NEG_INF, F32)
    l_ref[...] = jnp.zeros(l_ref.shape, F32)
    acc_ref[...] = jnp.zeros(acc_ref.shape, F32)
    tiles = [(h, r0) for h in range(n_heads) for r0 in range(0, tq, MOBA_SUB)]
    q_tiles = [qh[h][r0:r0 + MOBA_SUB] for h, r0 in tiles]

    def update(scores, v_blks):
        n_keys = v_blks[0].shape[0]
        v_ext = [jnp.concatenate([v, jnp.ones((n_keys, LANES), BF16)], axis=1) for v in v_blks]
        refs = [(h, slice(r0, r0 + MOBA_SUB)) for h, r0 in tiles]
        m_old = [m_ref[h, rows, :] for h, rows in refs]
        l_old = [l_ref[h, rows, :] for h, rows in refs]
        acc_old = [acc_ref[h, rows, :] for h, rows in refs]
        m_new = [jnp.maximum(m, jnp.broadcast_to(jnp.max(s, axis=-1, keepdims=True),
                                                 (MOBA_SUB, LANES)))
                 for m, s in zip(m_old, scores)]
        alpha = [jnp.exp(m - n) for m, n in zip(m_old, m_new)]
        p = [jnp.exp(s - jnp.concatenate([n] * (n_keys // LANES), axis=1)).astype(BF16)
             for s, n in zip(scores, m_new)]
        pv = [jnp.dot(x, v_ext[h // 2], preferred_element_type=F32)
              for x, (h, _) in zip(p, tiles)]
        for (h, rows), m, a, l, acc, y in zip(refs, m_new, alpha, l_old, acc_old, pv):
            m_ref[h, rows, :] = m
            l_ref[h, rows, :] = a * l + y[:, LANES:2 * LANES]
            acc_ref[h, rows, :] = a * acc + y[:, 0:LANES]

    start = pl.multiple_of(qi * MOBA_BLOCK, MOBA_BLOCK)
    k_own = [k_ref[0, pl.ds(start, MOBA_BLOCK), lanes] for lanes in pair_lanes]
    v_own = [v_ref[0, pl.ds(start, MOBA_BLOCK), lanes] for lanes in pair_lanes]
    row = _iota((MOBA_SUB, MOBA_BLOCK), 0)
    col = _iota((MOBA_SUB, MOBA_BLOCK), 1)
    update([jnp.where(col <= row + r0, _dot_nt(qt, k_own[h // 2]), NEG_INF)
            for qt, (h, r0) in zip(q_tiles, tiles)], v_own)

    sel = sel_ref[0]
    sel_head = _iota(sel.shape, 1) // SEL_GROUP
    head0 = pl.program_id(1) * n_heads
    sel_h = [jnp.where(sel_head == head0 + h, sel, jnp.zeros_like(sel)) for h in range(n_heads)]
    q_ext = [jnp.concatenate([qt, sel_h[h][r0:r0 + MOBA_SUB]], axis=1)
             for qt, (h, r0) in zip(q_tiles, tiles)]
    def past_blocks(first_blk, n_blocks):
        n_keys = n_blocks * MOBA_BLOCK
        start = pl.multiple_of(first_blk * MOBA_BLOCK, MOBA_BLOCK)
        key_blk = first_blk + _iota((n_keys, LANES), 0) // MOBA_BLOCK
        key_lane = _iota((n_keys, LANES), 1)
        block_onehot = ((key_lane < RWKV_HEADS * SEL_GROUP)
                        & ((key_lane & (SEL_GROUP - 1)) == key_blk)).astype(BF16)
        k_ext = [jnp.concatenate([k_ref[0, pl.ds(start, n_keys), lanes], block_onehot], axis=1)
                 for lanes in pair_lanes]
        v_blks = [v_ref[0, pl.ds(start, n_keys), lanes] for lanes in pair_lanes]
        update([_dot_nt(qe, k_ext[h // 2]) for qe, (h, _) in zip(q_ext, tiles)], v_blks)

    def two_blocks(j, carry):
        past_blocks(2 * j, 2)
        return carry

    lax.fori_loop(0, qi // 2, two_blocks, 0)
    pl.when(qi % 2 == 1)(lambda: past_blocks(qi - 1, 1))

    for p, lanes in enumerate(pair_lanes):
        o_ref[0, :, lanes] = jnp.where(first, acc_ref[2 * p] / l_ref[2 * p],
                                       acc_ref[2 * p + 1] / l_ref[2 * p + 1]).astype(o_ref.dtype)


def _moba_call(q, k, v, sel_bias):
    batch, seq, _ = q.shape
    tq = MOBA_BLOCK
    width = MOBA_PAIRS * LANES
    n_heads = 2 * MOBA_PAIRS
    return pl.pallas_call(
        _moba_kernel,
        grid=(batch, PAIRS // MOBA_PAIRS, seq // tq),
        in_specs=[pl.BlockSpec((1, tq, width), lambda b, p, i: (b, i, p)),
                  pl.BlockSpec((1, seq, width), lambda b, p, i: (b, 0, p)),
                  pl.BlockSpec((1, seq, width), lambda b, p, i: (b, 0, p)),
                  pl.BlockSpec((1, tq, LANES), lambda b, p, i: (b, i, 0))],
        out_specs=pl.BlockSpec((1, tq, width), lambda b, p, i: (b, i, p)),
        out_shape=jax.ShapeDtypeStruct((batch, seq, WIDTH), BF16),
        scratch_shapes=[pltpu.VMEM((n_heads, tq, LANES), F32),
                        pltpu.VMEM((n_heads, tq, LANES), F32),
                        pltpu.VMEM((n_heads, tq, LANES), F32)],
        compiler_params=pltpu.CompilerParams(
            dimension_semantics=("arbitrary", "arbitrary", "arbitrary"),
            vmem_limit_bytes=VMEM_LIMIT),
        name="moba",
    )(q, k, v, sel_bias)


def _pack_bf16_pair(lo, hi):
    lo_bits = lax.bitcast_convert_type(lo.astype(BF16).astype(F32), jnp.uint32)
    hi_bits = lax.bitcast_convert_type(hi.astype(BF16).astype(F32), jnp.uint32)
    return (lo_bits >> 16) | (hi_bits & jnp.uint32(0xFFFF0000))


def _unpack_bf16_pair(u):
    lo = lax.bitcast_convert_type(u << 16, F32)
    hi = lax.bitcast_convert_type(u & jnp.uint32(0xFFFF0000), F32)
    return lo, hi


def _merge_kernel(x_ref, ya_ref, yb_ref, gate_ref, wa_ref, wb_ref, wo_ref, g1_ref, ln_ref,
                  scale_ref, shift_ref, wr_ref, br_ref, tri_ref, x1_ref, h2_ref, route_ref,
                  count_ref, carry_ref):
    @pl.when(pl.program_id(0) == 0)
    def _():
        carry_ref[...] = jnp.zeros_like(carry_ref)

    ya = jnp.dot(ya_ref[...], wa_ref[...], preferred_element_type=F32)
    yb = jnp.dot(yb_ref[...], wb_ref[...], preferred_element_type=F32)
    gates = gate_ref[...]
    merged = (gates[:, 0:D_MODEL].astype(F32) * ya + gates[:, D_MODEL:GATE_COLS].astype(F32) * yb)
    x1 = x_ref[...] + g1_ref[0] * jnp.dot(merged.astype(BF16), wo_ref[...],
                                          preferred_element_type=F32)
    x1_ref[...] = x1

    ms = jnp.mean(x1 * x1, axis=-1, keepdims=True)
    h2 = x1 * lax.rsqrt(ms + NORM_EPS) * ln_ref[...]
    h2 = h2 * (1.0 + scale_ref[0]) + shift_ref[0]
    half = D_MODEL // 2
    h2_ref[...] = _pack_bf16_pair(h2[:, 0:half], h2[:, half:D_MODEL])

    logits = _dot_x3(h2, wr_ref[...]) + br_ref[...]
    lane = _iota(logits.shape, 1)
    lane_f = lane.astype(F32)
    far = float(ROUTER_LANES)

    def top(vals):
        m = jnp.max(vals, axis=-1, keepdims=True)
        idx = jnp.min(jnp.where(vals == m, lane_f, far), axis=-1, keepdims=True)
        return m, idx

    grp = jnp.where(lane < N_GROUPS, logits, NEG_INF)
    g_max, g_idx = top(grp)
    p_group = 1.0 / jnp.sum(jnp.exp(grp - g_max), axis=-1, keepdims=True)

    e_lo = EXPERT_LANE0 + EXPERTS_PER_GROUP * g_idx
    in_grp = (lane_f >= e_lo) & (lane_f < e_lo + EXPERTS_PER_GROUP)
    el = jnp.where(in_grp, logits, NEG_INF)
    m1, i1 = top(el)
    m2, i2 = top(jnp.where(lane_f == i1, NEG_INF, el))
    ratio = jnp.exp(m2 - m1)
    w_first = p_group / (1.0 + ratio)
    w_second = w_first * ratio

    first = lane_f == i1
    second = lane_f == i2
    hits = (first | second).astype(BF16)
    before = carry_ref[...] + jnp.dot(tri_ref[...], hits, preferred_element_type=F32)
    rank1 = jnp.sum(jnp.where(first, before, 0.0), axis=-1, keepdims=True)
    rank2 = jnp.sum(jnp.where(second, before, 0.0), axis=-1, keepdims=True)
    carry = carry_ref[...] + jnp.sum(hits.astype(F32), axis=0, keepdims=True)
    carry_ref[...] = carry
    count_ref[...] = carry

    fields = (i1 - EXPERT_LANE0, i2 - EXPERT_LANE0, w_first, w_second, rank1, rank2)
    route = jnp.zeros(logits.shape, F32)
    for n, field in enumerate(fields):
        route = jnp.where(lane == n, field, route)
    route_ref[...] = route


R_EXPERT1, R_EXPERT2, R_WEIGHT1, R_WEIGHT2, R_RANK1, R_RANK2 = range(6)
MERGE_ROWS = 1024


def _merge_call(x2d, ya, yb, gates, w_br_rwkv, w_br_moba, w_out, gate1, ln2_g, scale2, shift2,
                w_router, b_router, tri, seq):
    tokens = x2d.shape[0]
    tm = MERGE_ROWS
    per_seq = seq // tm
    row = lambda width: pl.BlockSpec((tm, width), lambda i: (i, 0))
    per_batch = pl.BlockSpec((1, 1, D_MODEL), lambda i: (i // per_seq, 0, 0))
    const = lambda shape: pl.BlockSpec(shape, lambda i: (0,) * len(shape))
    return pl.pallas_call(
        _merge_kernel,
        grid=(tokens // tm,),
        in_specs=[row(D_MODEL), row(WIDTH), row(WIDTH), row(GATE_COLS),
                  const((WIDTH, D_MODEL)), const((WIDTH, D_MODEL)), const((D_MODEL, D_MODEL)),
                  per_batch, const((1, D_MODEL)), per_batch, per_batch,
                  const((D_MODEL, ROUTER_LANES)), const((1, ROUTER_LANES)), const((tm, tm))],
        out_specs=[row(D_MODEL), row(D_MODEL // 2), row(ROUTER_LANES),
                   const((1, ROUTER_LANES))],
        out_shape=[jax.ShapeDtypeStruct((tokens, D_MODEL), F32),
                   jax.ShapeDtypeStruct((tokens, D_MODEL // 2), jnp.uint32),
                   jax.ShapeDtypeStruct((tokens, ROUTER_LANES), F32),
                   jax.ShapeDtypeStruct((1, ROUTER_LANES), F32)],
        scratch_shapes=[pltpu.VMEM((1, ROUTER_LANES), F32)],
        compiler_params=pltpu.CompilerParams(
            dimension_semantics=("arbitrary",), vmem_limit_bytes=VMEM_LIMIT),
        name="merge",
    )(x2d, ya, yb, gates, w_br_rwkv, w_br_moba, w_out, gate1, ln2_g, scale2, shift2,
      w_router, b_router, tri)


EXPERT_ROWS = 512
DISPATCH_TOKENS = 1024
COMBINE_TOKENS = 512


def _dest_kernel(route_ref, start_ref, o_ref):
    route = route_ref[...]
    lane = _iota(route.shape, 1)
    lane_f = lane.astype(F32)
    starts = start_ref[...]
    out = jnp.zeros(route.shape, F32)
    for n, (e_lane, r_lane) in enumerate(((R_EXPERT1, R_RANK1), (R_EXPERT2, R_RANK2))):
        e_col = route[:, e_lane:e_lane + 1] + EXPERT_LANE0
        first_row = jnp.sum(jnp.where(lane_f == e_col, starts, 0.0), axis=-1, keepdims=True)
        out = jnp.where(lane == n, first_row + route[:, r_lane:r_lane + 1], out)
    o_ref[...] = out


def _dest_call(route, start_row):
    tokens = route.shape[0]
    tm = DISPATCH_TOKENS
    return pl.pallas_call(
        _dest_kernel,
        grid=(tokens // tm,),
        in_specs=[pl.BlockSpec((tm, ROUTER_LANES), lambda i: (i, 0)),
                  pl.BlockSpec((1, ROUTER_LANES), lambda i: (0, 0))],
        out_specs=pl.BlockSpec((tm, ROUTER_LANES), lambda i: (i, 0)),
        out_shape=jax.ShapeDtypeStruct((tokens, ROUTER_LANES), F32),
        name="moe_dest",
    )(route, start_row)


def _dispatch_kernel(d1_ref, d2_ref, ends_ref, h_ref, xs_hbm, zero_ref, sem, zero_sem):
    base = pl.program_id(0) * DISPATCH_TOKENS

    @pl.when(pl.program_id(0) == 0)
    def _():
        zero_ref[...] = jnp.zeros_like(zero_ref)

        def fill(e):
            end = ends_ref[e]
            start = pl.multiple_of(end - EXPERT_ROWS, EXPERT_ROWS)
            return pltpu.make_async_copy(zero_ref, xs_hbm.at[pl.ds(start, EXPERT_ROWS)],
                                         zero_sem)

        def non_empty(e):
            return ends_ref[e] > (ends_ref[e - 1] if e else 0)

        for e in range(N_EXPERTS):
            pl.when(non_empty(e))(lambda e=e: fill(e).start())
        for e in range(N_EXPERTS):
            pl.when(non_empty(e))(lambda e=e: fill(e).wait())

        def tail(t):
            start = pl.multiple_of(t * EXPERT_ROWS, EXPERT_ROWS)
            return pltpu.make_async_copy(zero_ref, xs_hbm.at[pl.ds(start, EXPERT_ROWS)],
                                         zero_sem)

        first_tail = ends_ref[N_EXPERTS - 1] // EXPERT_ROWS
        n_tiles = xs_hbm.shape[0] // EXPERT_ROWS
        lax.fori_loop(first_tail, n_tiles, lambda t, c: (tail(t).start(), c)[1], 0)
        lax.fori_loop(first_tail, n_tiles, lambda t, c: (tail(t).wait(), c)[1], 0)

    def issue(t, carry):
        src = h_ref.at[pl.ds(t, 1)]
        pltpu.make_async_copy(src, xs_hbm.at[pl.ds(d1_ref[base + t], 1)], sem).start()
        pltpu.make_async_copy(src, xs_hbm.at[pl.ds(d2_ref[base + t], 1)], sem).start()
        return carry

    lax.fori_loop(0, DISPATCH_TOKENS, issue, 0, unroll=8)
    for _ in range(2):
        pltpu.make_async_copy(h_ref, xs_hbm.at[pl.ds(0, DISPATCH_TOKENS)], sem).wait()


def _dispatch_call(dest1, dest2, ends, h2p, n_rows):
    tokens = h2p.shape[0]
    half = D_MODEL // 2
    return pl.pallas_call(
        _dispatch_kernel,
        grid_spec=pltpu.PrefetchScalarGridSpec(
            num_scalar_prefetch=3,
            grid=(tokens // DISPATCH_TOKENS,),
            in_specs=[pl.BlockSpec((DISPATCH_TOKENS, half), lambda i, d1, d2, ends: (i, 0))],
            out_specs=pl.BlockSpec(memory_space=pl.ANY),
            scratch_shapes=[pltpu.VMEM((EXPERT_ROWS, half), jnp.uint32),
                            pltpu.SemaphoreType.DMA(()), pltpu.SemaphoreType.DMA(())]),
        out_shape=jax.ShapeDtypeStruct((n_rows, half), jnp.uint32),
        compiler_params=pltpu.CompilerParams(dimension_semantics=("arbitrary",)),
        name="moe_dispatch",
    )(dest1, dest2, ends, h2p)


def _expert_kernel(te_ref, nu_ref, xs_ref, w1_ref, w3_ref, w2_ref, ys_ref, w1b_ref, w3b_ref,
                   w2b_ref):
    half = D_MODEL // 2
    j = pl.program_id(0)

    @pl.when((j == 0) | (te_ref[j] != te_ref[jnp.maximum(j - 1, 0)]))
    def _():
        w1b_ref[...] = w1_ref[0].astype(BF16)
        w3b_ref[...] = w3_ref[0].astype(BF16)
        w2b_ref[...] = w2_ref[0].astype(BF16)

    @pl.when(j < nu_ref[0])
    def _():
        x_lo, x_hi = _unpack_bf16_pair(xs_ref[...])
        x_lo = x_lo.astype(BF16)
        x_hi = x_hi.astype(BF16)

        def proj(w_ref):
            return (jnp.dot(x_lo, w_ref[0:half, :], preferred_element_type=F32)
                    + jnp.dot(x_hi, w_ref[half:D_MODEL, :], preferred_element_type=F32))

        a = proj(w1b_ref)
        hid = (a * jax.nn.sigmoid(a) * proj(w3b_ref)).astype(BF16)
        y = jnp.dot(hid, w2b_ref[...], preferred_element_type=F32)
        ys_ref[...] = _pack_bf16_pair(y[:, 0:half], y[:, half:D_MODEL])

    @pl.when(j >= nu_ref[0])
    def _():
        ys_ref[...] = jnp.zeros_like(ys_ref)


def _expert_call(tile_expert, n_used, xs, w1, w3, w2):
    n_rows = xs.shape[0]
    half = D_MODEL // 2
    w_spec = lambda shape: pl.BlockSpec((1,) + shape, lambda j, te, nu: (te[j], 0, 0))
    return pl.pallas_call(
        _expert_kernel,
        grid_spec=pltpu.PrefetchScalarGridSpec(
            num_scalar_prefetch=2,
            grid=(n_rows // EXPERT_ROWS,),
            in_specs=[pl.BlockSpec((EXPERT_ROWS, half),
                                   lambda j, te, nu: (jnp.minimum(j, nu[0] - 1), 0)),
                      w_spec((D_MODEL, D_EXPERT)), w_spec((D_MODEL, D_EXPERT)),
                      w_spec((D_EXPERT, D_MODEL))],
            out_specs=pl.BlockSpec((EXPERT_ROWS, half), lambda j, te, nu: (j, 0)),
            scratch_shapes=[pltpu.VMEM((D_MODEL, D_EXPERT), BF16),
                            pltpu.VMEM((D_MODEL, D_EXPERT), BF16),
                            pltpu.VMEM((D_EXPERT, D_MODEL), BF16)]),
        out_shape=jax.ShapeDtypeStruct((n_rows, half), jnp.uint32),
        compiler_params=pltpu.CompilerParams(
            dimension_semantics=("arbitrary",), vmem_limit_bytes=VMEM_LIMIT),
        name="moe_experts",
    )(tile_expert, n_used, xs, w1, w3, w2)


def _combine_kernel(d1_ref, d2_ref, ys_hbm, x1_ref, route_ref, g2_ref, o_ref, buf_ref, sem):
    i = pl.program_id(0)
    n_steps = pl.num_programs(0)
    tc = COMBINE_TOKENS
    half = D_MODEL // 2

    groups = tc // SUBLANES

    def issue(step, slot):
        base = step * tc

        def one_group(grp, carry):
            for r in range(SUBLANES):
                tok = base + grp * SUBLANES + r
                d1 = d1_ref[tok]
                d2 = d2_ref[tok]
                pltpu.make_async_copy(ys_hbm.at[d1 >> 3, pl.ds(d1 & (SUBLANES - 1), 1)],
                                      buf_ref.at[slot, grp, pl.ds(r, 1)], sem.at[slot]).start()
                pltpu.make_async_copy(ys_hbm.at[d2 >> 3, pl.ds(d2 & (SUBLANES - 1), 1)],
                                      buf_ref.at[slot, groups + grp, pl.ds(r, 1)],
                                      sem.at[slot]).start()
            return carry

        lax.fori_loop(0, groups, one_group, 0)

    slot = i % 2

    @pl.when(i == 0)
    def _():
        issue(0, 0)

    @pl.when(i + 1 < n_steps)
    def _():
        issue(i + 1, 1 - slot)

    pltpu.make_async_copy(ys_hbm.at[pl.ds(0, 2 * groups)], buf_ref.at[slot], sem.at[slot]).wait()

    rows = buf_ref[slot].reshape(2 * tc, half)
    a_lo, a_hi = _unpack_bf16_pair(rows[0:tc])
    b_lo, b_hi = _unpack_bf16_pair(rows[tc:2 * tc])
    route = route_ref[...]
    w_a = route[:, R_WEIGHT1:R_WEIGHT1 + 1]
    w_b = route[:, R_WEIGHT2:R_WEIGHT2 + 1]
    g2 = g2_ref[0]
    o_ref[:, 0:half] = x1_ref[:, 0:half] + g2[:, 0:half] * (w_a * a_lo + w_b * b_lo)
    o_ref[:, half:D_MODEL] = (x1_ref[:, half:D_MODEL]
                              + g2[:, half:D_MODEL] * (w_a * a_hi + w_b * b_hi))


def _combine_call(dest1, dest2, ys, x1, route, gate2, seq):
    tokens = x1.shape[0]
    tc = COMBINE_TOKENS
    per_seq = seq // tc
    half = D_MODEL // 2
    return pl.pallas_call(
        _combine_kernel,
        grid_spec=pltpu.PrefetchScalarGridSpec(
            num_scalar_prefetch=2,
            grid=(tokens // tc,),
            in_specs=[pl.BlockSpec(memory_space=pl.ANY),
                      pl.BlockSpec((tc, D_MODEL), lambda i, d1, d2: (i, 0)),
                      pl.BlockSpec((tc, ROUTER_LANES), lambda i, d1, d2: (i, 0)),
                      pl.BlockSpec((1, 1, D_MODEL), lambda i, d1, d2: (i // per_seq, 0, 0))],
            out_specs=pl.BlockSpec((tc, D_MODEL), lambda i, d1, d2: (i, 0)),
            scratch_shapes=[pltpu.VMEM((2, 2 * tc // SUBLANES, SUBLANES, half), jnp.uint32),
                            pltpu.SemaphoreType.DMA((2,))]),
        out_shape=jax.ShapeDtypeStruct((tokens, D_MODEL), F32),
        compiler_params=pltpu.CompilerParams(
            dimension_semantics=("arbitrary",), vmem_limit_bytes=VMEM_LIMIT),
        name="moe_combine",
    )(dest1, dest2, ys.reshape(-1, SUBLANES, half), x1, route, gate2)


def _moe(h2p, route, counts, x1, gate2, w1, w3, w2, seq):
    tokens = h2p.shape[0]
    n_rows = 2 * tokens + N_EXPERTS * EXPERT_ROWS
    n_rows -= n_rows % EXPERT_ROWS
    n_tiles = n_rows // EXPERT_ROWS

    count = counts[0, EXPERT_LANE0:EXPERT_LANE0 + N_EXPERTS].astype(jnp.int32)
    padded = (count + EXPERT_ROWS - 1) // EXPERT_ROWS * EXPERT_ROWS
    ends = jnp.cumsum(padded)
    starts = ends - padded
    tile_start = jnp.arange(n_tiles, dtype=jnp.int32) * EXPERT_ROWS
    tile_expert = jnp.minimum(jnp.sum(ends[None, :] <= tile_start[:, None], axis=1),
                              N_EXPERTS - 1).astype(jnp.int32)
    n_used = (ends[-1:] // EXPERT_ROWS).astype(jnp.int32)

    start_row = jnp.zeros((1, ROUTER_LANES), F32)
    start_row = start_row.at[0, EXPERT_LANE0:EXPERT_LANE0 + N_EXPERTS].set(starts.astype(F32))
    dest = _dest_call(route, start_row)[:, 0:2].astype(jnp.int32)
    dest1, dest2 = dest[:, 0], dest[:, 1]

    xs = _dispatch_call(dest1, dest2, ends.astype(jnp.int32), h2p, n_rows)
    ys = _expert_call(tile_expert, n_used, xs, w1, w3, w2)
    return _combine_call(dest1, dest2, ys, x1, route, gate2, seq)


def _rope_tables(seq):
    half = HEAD // 2
    inv_freq = ROPE_THETA ** (-jnp.arange(half, dtype=F32) / half)
    ang = jnp.arange(seq, dtype=F32)[:, None] * inv_freq[None, :]
    cos = jnp.cos(ang)
    sin = jnp.sin(ang)
    cos_head = jnp.concatenate([cos, cos], axis=1)
    sin_head = jnp.concatenate([-sin, sin], axis=1)
    return jnp.tile(cos_head, (1, RWKV_HEADS)), jnp.tile(sin_head, (1, RWKV_HEADS))


def _layer(x, c, w_ada, b_ada, ln1_g, ln2_g, w_in, mu_shift, w0, w_lora_up, a0, a_lora_up,
           g_lora_up, k_k, k_a, r_k, lnx_g, lnx_b, q_norm_g, k_norm_g, w_br_rwkv, w_br_moba,
           w_out, w_rg, b_rg, w_re, b_re, w1, w3, w2):
    batch, seq, _ = x.shape
    vec = lambda a: a.reshape(1, -1)

    mod = _mod_call(c, w_ada, b_ada)
    shift1, scale1, gate1, shift2, scale2, gate2 = (
        m.reshape(batch, 1, D_MODEL) for m in jnp.split(mod, 6, axis=-1))

    idx = np.arange(MXU_DIM)
    ones_bd = jnp.asarray(idx[:, None] // HEAD == idx[None, :] // HEAD, BF16)
    t_idx = np.arange(RWKV_ROWS)
    tri = jnp.asarray((t_idx[:, None] >= t_idx[None, :])
                      & (t_idx[:, None] // CHUNK == t_idx[None, :] // CHUNK), BF16)
    cos, sin_signed = _rope_tables(seq)

    z_rwkv, q, k, v, gates, sel_bias = _inproj_call(
        x, shift1, scale1, vec(ln1_g), w_in.astype(BF16),
        vec(jnp.tile(q_norm_g, RWKV_HEADS)), vec(jnp.tile(k_norm_g, RWKV_HEADS)),
        cos, sin_signed, ones_bd)

    lora_cat = jnp.zeros((DECAY_LORA + AAA_LORA, 2 * WIDTH), F32)
    lora_cat = lora_cat.at[:DECAY_LORA, :WIDTH].set(w_lora_up).at[DECAY_LORA:, WIDTH:].set(a_lora_up)
    y_a = _rwkv_call(z_rwkv, vec(mu_shift), vec(w0), vec(a0), lora_cat, g_lora_up, vec(k_k),
                     vec(k_a), vec(r_k), vec(lnx_g), vec(lnx_b), ones_bd, tri)

    y_b = _moba_call(q, k, v, sel_bias)

    tokens = batch * seq
    w_router = jnp.zeros((D_MODEL, ROUTER_LANES), F32)
    w_router = w_router.at[:, :N_GROUPS].set(w_rg).at[:, EXPERT_LANE0:EXPERT_LANE0 + N_EXPERTS].set(w_re)
    b_router = jnp.zeros((1, ROUTER_LANES), F32)
    b_router = b_router.at[0, :N_GROUPS].set(b_rg).at[0, EXPERT_LANE0:EXPERT_LANE0 + N_EXPERTS].set(b_re)
    m_idx = np.arange(MERGE_ROWS)
    tri_strict = jnp.asarray(m_idx[:, None] > m_idx[None, :], BF16)
    x1, h2p, route, counts = _merge_call(
        x.reshape(tokens, D_MODEL), y_a.reshape(tokens, WIDTH), y_b.reshape(tokens, WIDTH),
        gates.reshape(tokens, GATE_COLS), w_br_rwkv.astype(BF16), w_br_moba.astype(BF16),
        w_out.astype(BF16), gate1, vec(ln2_g), scale2, shift2, w_router, b_router, tri_strict,
        seq)

    out = _moe(h2p, route, counts, x1, gate2, w1, w3, w2, seq)
    return out.reshape(batch, seq, D_MODEL)


def kernel(x, c, w_ada, b_ada, ln1_g, ln2_g, w_in, mu_shift, w0, w_lora_up, a0, a_lora_up,
           g_lora_up, k_k, k_a, r_k, lnx_g, lnx_b, q_norm_g, k_norm_g, w_br_rwkv, w_br_moba,
           w_out, w_rg, b_rg, w_re, b_re, w1, w3, w2):
    assert w_ada.shape[0] == 1, "single-layer problem"
    layer_params = (w_ada, b_ada, ln1_g, ln2_g, w_in, mu_shift, w0, w_lora_up, a0, a_lora_up,
                    g_lora_up, k_k, k_a, r_k, lnx_g, lnx_b, q_norm_g, k_norm_g, w_br_rwkv,
                    w_br_moba, w_out, w_rg, b_rg, w_re, b_re, w1, w3, w2)
    return _layer(x, c, *(p[0] for p in layer_params))
```

```python
import jax
import jax.numpy as jnp
import numpy as np
from jax import lax
from jax.experimental import pallas as pl
from jax.experimental.pallas import tpu as pltpu

F32 = jnp.float32
BF16 = jnp.bfloat16
HIGHEST = lax.Precision.HIGHEST

D_MODEL = 1024
RWKV_HEADS = 8
HEAD = 64
WIDTH = RWKV_HEADS * HEAD
DECAY_LORA = 64
AAA_LORA = 64
GATE_LORA = 128
RWKV_COLS = 3 * WIDTH + DECAY_LORA + AAA_LORA + GATE_LORA
ATT_COLS = 3 * WIDTH
GATE_COLS = 2 * D_MODEL
IN_COLS = RWKV_COLS + ATT_COLS + GATE_COLS
DECAY_SCALE = 0.606531
LN_X_EPS = 64e-5
MOBA_BLOCK = 256
MOBA_TOPK = 3
ROPE_THETA = 10000.0
N_GROUPS = 4
EXPERTS_PER_GROUP = 8
N_EXPERTS = N_GROUPS * EXPERTS_PER_GROUP
D_EXPERT = D_MODEL // 2
NORM_EPS = 1e-6
NEG_INF = -1e30

LANES = 128
MXU_DIM = 256
SUBLANES = 8
PAIRS = WIDTH // LANES
CHUNK = 64
SOLVE_BLOCK = 16
RWKV_ROWS = 4 * CHUNK
MOBA_SUB = 256
INPROJ_ROWS = MOBA_BLOCK
MOBA_PAIRS = 4
VMEM_LIMIT = 56 * 1024 * 1024

ROUTER_LANES = LANES
EXPERT_LANE0 = N_GROUPS


def _dot(a, b):
    return jnp.dot(a.astype(BF16), b.astype(BF16), preferred_element_type=F32)


def _dot_nt(a, b):
    return lax.dot_general(a.astype(BF16), b.astype(BF16), (((1,), (1,)), ((), ())),
                           preferred_element_type=F32)


def _dot_f32(a, b):
    return jnp.dot(a, b, precision=HIGHEST, preferred_element_type=F32)


def _split_bf16(a):
    hi = a.astype(BF16)
    return hi, (a - hi.astype(F32)).astype(BF16)


def _dot_x3(a, b):
    a_hi, a_lo = _split_bf16(a)
    b_hi, b_lo = _split_bf16(b)
    return (jnp.dot(a_hi, b_hi, preferred_element_type=F32)
            + jnp.dot(a_hi, b_lo, preferred_element_type=F32)
            + jnp.dot(a_lo, b_hi, preferred_element_type=F32))


def _dot_wide_rhs(a, b):
    a = a.astype(BF16)
    hi, lo = _split_bf16(b)
    return (jnp.dot(a, hi, preferred_element_type=F32)
            + jnp.dot(a, lo, preferred_element_type=F32))


def _seg_sum(x, ones_bd, wide=True):
    group = ones_bd.shape[0]
    parts = _split_bf16(x) if wide else (x.astype(BF16),)
    cols = []
    for c0 in range(0, x.shape[1], group):
        cols.append(sum(jnp.dot(part[:, c0:c0 + group], ones_bd, preferred_element_type=F32)
                        for part in parts))
    return jnp.concatenate(cols, axis=1)


def _iota(shape, axis):
    return lax.broadcasted_iota(jnp.int32, shape, axis)


def _mod_kernel(c_ref, w_ref, b_ref, o_ref):
    c = c_ref[...]
    o_ref[...] = _dot_f32(c * jax.nn.sigmoid(c), w_ref[...]) + b_ref[...]


def _mod_call(c, w_ada, b_ada):
    batch = c.shape[0]
    n_out = w_ada.shape[1]
    tn = D_MODEL
    return pl.pallas_call(
        _mod_kernel,
        grid=(n_out // tn,),
        in_specs=[pl.BlockSpec((batch, D_MODEL), lambda j: (0, 0)),
                  pl.BlockSpec((D_MODEL, tn), lambda j: (0, j)),
                  pl.BlockSpec((1, tn), lambda j: (0, j))],
        out_specs=pl.BlockSpec((batch, tn), lambda j: (0, j)),
        out_shape=jax.ShapeDtypeStruct((batch, n_out), F32),
        name="adaln_mod",
    )(c, w_ada, b_ada.reshape(1, n_out))


def _swap_halves(x):
    first = (_iota(x.shape, 1) & (HEAD - 1)) < HEAD // 2
    up = pltpu.roll(x, LANES - HEAD // 2, axis=1)
    down = pltpu.roll(x, HEAD // 2, axis=1)
    return jnp.where(first, up, down)


def _head_norm_rope(x, gain, cos, sin_signed, ones_bd):
    ms = _seg_sum(x * x, ones_bd, wide=False) * (1.0 / HEAD)
    y = x * lax.rsqrt(ms + NORM_EPS) * gain
    cols = []
    for p in range(PAIRS):
        sl = slice(p * LANES, (p + 1) * LANES)
        yb = y[:, sl]
        cols.append(yb * cos[:, sl] + _swap_halves(yb) * sin_signed[:, sl])
    return jnp.concatenate(cols, axis=1)


SEL_GROUP = 8


def _block_bias(q, km_ref, blk_idx):
    q_hi, q_lo = _split_bf16(q)
    km_hi, km_lo = _split_bf16(km_ref[...])
    nt = lambda a, b: lax.dot_general(a, b, (((1,), (1,)), ((), ())), preferred_element_type=F32)
    gate = nt(q_hi, km_hi) + nt(q_hi, km_lo) + nt(q_lo, km_hi)
    lane = _iota(gate.shape, 1)
    blk = lane & (SEL_GROUP - 1)
    valid = (blk < blk_idx) & (lane < RWKV_HEADS * SEL_GROUP)
    g = jnp.where(valid, gate, NEG_INF)
    rank = jnp.zeros(gate.shape, F32)
    for shift in range(1, SEL_GROUP):
        wrapped = blk + shift >= SEL_GROUP
        partner = jnp.where(wrapped, pltpu.roll(g, SEL_GROUP - shift, axis=1),
                            pltpu.roll(g, LANES - shift, axis=1))
        ahead = (partner > g) | ((partner == g) & wrapped)
        rank = rank + ahead.astype(F32)
    return jnp.where(valid & (rank < MOBA_TOPK), 0.0, NEG_INF)


def _inproj_kernel(x_ref, shift_ref, scale_ref, g_ref, w_ref, qg_ref, kg_ref, cos_ref, sin_ref,
                   ones_ref, zr_ref, q_ref, k_ref, v_ref, gate_ref, bias_ref, km_ref):
    i = pl.program_id(1)

    @pl.when(i == 0)
    def _():
        km_ref[...] = jnp.zeros_like(km_ref)

    x = x_ref[0]
    ms = jnp.mean(x * x, axis=-1, keepdims=True)
    h = x * lax.rsqrt(ms + NORM_EPS) * g_ref[...]
    h = (h * (1.0 + scale_ref[0]) + shift_ref[0]).astype(BF16)

    zr_ref[0] = jnp.dot(h, w_ref[:, 0:RWKV_COLS], preferred_element_type=F32)

    za = jnp.dot(h, w_ref[:, RWKV_COLS:RWKV_COLS + ATT_COLS], preferred_element_type=F32)
    ones_bd = ones_ref[...]
    cos = cos_ref[...]
    sin = sin_ref[...]
    q = _head_norm_rope(za[:, 0:WIDTH], qg_ref[...], cos, sin, ones_bd)
    k = _head_norm_rope(za[:, WIDTH:2 * WIDTH], kg_ref[...], cos, sin, ones_bd)
    q_ref[0] = (q * (HEAD ** -0.5)).astype(BF16)
    k_ref[0] = k.astype(BF16)
    v_ref[0] = za[:, 2 * WIDTH:3 * WIDTH].astype(BF16)

    head_of_lane = _iota((1, WIDTH), 1) // HEAD
    for sub in range(x.shape[0] // MOBA_BLOCK):
        rows = slice(sub * MOBA_BLOCK, (sub + 1) * MOBA_BLOCK)
        blk = i * (x.shape[0] // MOBA_BLOCK) + sub
        bias_ref[0, rows, :] = _block_bias(q[rows], km_ref, blk).astype(BF16)
        k_mean = jnp.mean(k[rows], axis=0, keepdims=True)
        for head in range(RWKV_HEADS):
            km_ref[pl.ds(head * SEL_GROUP + blk, 1), :] = jnp.where(head_of_lane == head,
                                                                   k_mean, 0.0)

    zg = jnp.dot(h, w_ref[:, RWKV_COLS + ATT_COLS:IN_COLS], preferred_element_type=F32)
    gate_ref[0] = jax.nn.sigmoid(zg).astype(BF16)


def _inproj_call(x, shift1, scale1, ln1_g, w_in_bf16, q_gain, k_gain, cos, sin_signed, ones_bd):
    batch, seq, _ = x.shape
    tm = INPROJ_ROWS
    n_t = seq // tm
    row = lambda width: pl.BlockSpec((1, tm, width), lambda b, i: (b, i, 0))
    per_batch = pl.BlockSpec((1, 1, D_MODEL), lambda b, i: (b, 0, 0))
    const = lambda shape: pl.BlockSpec(shape, lambda b, i: (0,) * len(shape))
    return pl.pallas_call(
        _inproj_kernel,
        grid=(batch, n_t),
        in_specs=[row(D_MODEL), per_batch, per_batch, const((1, D_MODEL)),
                  const((D_MODEL, IN_COLS)), const((1, WIDTH)), const((1, WIDTH)),
                  pl.BlockSpec((tm, WIDTH), lambda b, i: (i, 0)),
                  pl.BlockSpec((tm, WIDTH), lambda b, i: (i, 0)),
                  const((MXU_DIM, MXU_DIM))],
        out_specs=[row(RWKV_COLS), row(WIDTH), row(WIDTH), row(WIDTH), row(GATE_COLS),
                   row(LANES)],
        out_shape=[jax.ShapeDtypeStruct((batch, seq, RWKV_COLS), F32),
                   jax.ShapeDtypeStruct((batch, seq, WIDTH), BF16),
                   jax.ShapeDtypeStruct((batch, seq, WIDTH), BF16),
                   jax.ShapeDtypeStruct((batch, seq, WIDTH), BF16),
                   jax.ShapeDtypeStruct((batch, seq, GATE_COLS), BF16),
                   jax.ShapeDtypeStruct((batch, seq, LANES), BF16)],
        scratch_shapes=[pltpu.VMEM((LANES, WIDTH), F32)],
        compiler_params=pltpu.CompilerParams(
            dimension_semantics=("arbitrary", "arbitrary"), vmem_limit_bytes=VMEM_LIMIT),
        name="inproj",
    )(x, shift1, scale1, ln1_g, w_in_bf16, q_gain, k_gain, cos, sin_signed, ones_bd)


def _stack_heads(x):
    first = _iota(x.shape, 1) < HEAD
    return jnp.concatenate([jnp.where(first, x, 0.0), jnp.where(first, 0.0, x)], axis=0)


def _add_eye(x, eye):
    return jnp.where(eye, x + 1.0, x)


def _rwkv_kernel(z_ref, mu_ref, w0_ref, a0_ref, lora_ref, glora_ref, kk_ref, ka_ref, rk_ref,
                 lng_ref, lnb_ref, ones_ref, tri_ref, o_ref, prev_ref, state_ref):
    c = pl.program_id(1)

    @pl.when(c == 0)
    def _():
        prev_ref[...] = jnp.zeros_like(prev_ref)
        state_ref[...] = jnp.zeros_like(state_ref)

    z = z_ref[0]
    rows = z.shape[0]
    n_chunks = rows // CHUNK
    row = _iota(z.shape, 0)
    z_prev = jnp.where(row == 0, prev_ref[...], pltpu.roll(z, 1, axis=0))
    prev_ref[...] = z[rows - 1:rows, :]
    zs = z + (z_prev - z) * mu_ref[...]

    r = zs[:, 0:WIDTH]
    k = zs[:, WIDTH:2 * WIDTH]
    v = zs[:, 2 * WIDTH:3 * WIDTH]
    lo = 3 * WIDTH
    d_wa = zs[:, lo:lo + DECAY_LORA + AAA_LORA]
    d_g = zs[:, lo + DECAY_LORA + AAA_LORA:RWKV_COLS]
    is_decay = _iota(d_wa.shape, 1) < DECAY_LORA
    pre = _dot_x3(jnp.where(is_decay, jnp.tanh(d_wa), d_wa), lora_ref[...])
    log_w = -DECAY_SCALE * jax.nn.sigmoid(w0_ref[...] + pre[:, 0:WIDTH])
    a = jax.nn.sigmoid(a0_ref[...] + pre[:, WIDTH:2 * WIDTH])
    g = _dot(jax.nn.sigmoid(d_g), glora_ref[...])

    ones_bd = ones_ref[...]
    kk = k * kk_ref[...]
    kk = kk / jnp.maximum(jnp.sqrt(_seg_sum(kk * kk, ones_bd)), 1e-12)
    k = k * (1.0 + (a - 1.0) * ka_ref[...])
    bonus = _seg_sum(r * k * rk_ref[...], ones_bd, wide=False) * v

    cl = _dot_wide_rhs(tri_ref[...], log_w)
    p_end_rows = [cl[(c_i + 1) * CHUNK - 1:(c_i + 1) * CHUNK, :] for c_i in range(n_chunks)]
    cl_last = p_end_rows[0]
    chunk_of_row = _iota(cl.shape, 0) // CHUNK
    for c_i in range(1, n_chunks):
        cl_last = jnp.where(chunk_of_row == c_i, p_end_rows[c_i], cl_last)
    a_t = -kk * jnp.exp(cl - log_w)
    e_neg = jnp.exp(-cl)
    b_t = kk * a * e_neg
    k_t = k * e_neg
    r_t = r * jnp.exp(cl)
    e_end = jnp.exp(cl_last - cl)
    b_end = kk * a * e_end
    k_end = k * e_end
    p_end = jnp.exp(cl_last)

    n2 = 2 * CHUNK
    ri = _iota((n2, n2), 0)
    ci = _iota((n2, n2), 1)
    eye = ri == ci
    same_blk = (ri // SOLVE_BLOCK) == (ci // SOLVE_BLOCK)
    ri4 = _iota((2 * n2, 2 * n2), 0)
    ci4 = _iota((2 * n2, 2 * n2), 1) & (n2 - 1)
    causal4 = ((ri4 < n2) & (ri4 > ci4)) | ((ri4 >= n2) & ((ri4 - n2) >= ci4))
    zeros = jnp.zeros((n2, n2), BF16)

    all_items = [(c_i, p) for c_i in range(n_chunks) for p in range(PAIRS)]

    def tile(t, item):
        c_i, p = item
        return t[c_i * CHUNK:(c_i + 1) * CHUNK, p * LANES:(p + 1) * LANES]

    def bf16_all(xs):
        return [x.astype(BF16) for x in xs]

    def pair_dot(xs, ys, prod=_dot):
        out = []
        for i in range(0, len(xs), 2):
            z = jnp.zeros_like(ys[i])
            wide = prod(jnp.concatenate([xs[i], xs[i + 1]], axis=1),
                        jnp.concatenate([jnp.concatenate([ys[i], z], axis=1),
                                         jnp.concatenate([z, ys[i + 1]], axis=1)], axis=0))
            out += [wide[:, 0:n2], wide[:, n2:2 * n2]]
        return out

    def plus_eye(xs):
        return [_add_eye(x, eye).astype(BF16) for x in xs]

    def solve(items):
        sa = [_stack_heads(tile(a_t, it)).astype(BF16) for it in items]
        sr = [_stack_heads(tile(r_t, it)) for it in items]
        sv = [_stack_heads(tile(v, it)).astype(BF16) for it in items]
        big = [jnp.where(causal4,
                         _dot_nt(jnp.concatenate([sa[i], sr[i].astype(BF16)], axis=0),
                                 jnp.concatenate([_stack_heads(tile(b_t, it)).astype(BF16),
                                                  _stack_heads(tile(k_t, it)).astype(BF16)],
                                                 axis=0)), 0.0)
               for i, it in enumerate(items)]
        a_ab = [b[0:n2, 0:n2] for b in big]
        a_ak = bf16_all(b[0:n2, n2:2 * n2] for b in big)
        a_rbk = bf16_all(b[n2:2 * n2, :] for b in big)

        d1_f = [jnp.where(same_blk, a, 0.0) for a in a_ab]
        e1 = bf16_all(jnp.where(same_blk, 0.0, a) for a in a_ab)
        d1 = bf16_all(d1_f)
        d2_f = pair_dot(d1, d1)
        d2 = bf16_all(d2_f)
        d4_f = pair_dot(d2, d2)
        d4 = bf16_all(d4_f)
        d8_f = pair_dot(d4, d4)
        p12 = bf16_all(pair_dot(plus_eye(d1_f), plus_eye(d2_f)))
        p48 = bf16_all(pair_dot(plus_eye(d4_f), plus_eye(d8_f)))
        t_d = bf16_all(pair_dot(p12, p48))
        g1_f = pair_dot(t_d, e1)
        g1 = bf16_all(g1_f)
        g2_f = pair_dot(g1, g1)
        gx = bf16_all(pair_dot(plus_eye(g1_f), plus_eye(g2_f)))
        t_inv = bf16_all(pair_dot(gx, t_d))

        akv = bf16_all(pair_dot(a_ak, sv))
        wu = bf16_all(_dot(t, jnp.concatenate([s, x], axis=1))
                      for t, s, x in zip(t_inv, sa, akv))
        rhs = [jnp.concatenate([w, jnp.concatenate([zeros, s], axis=1)], axis=0)
               for w, s in zip(wu, sv)]
        out_c = [_dot(a, x) for a, x in zip(a_rbk, rhs)]
        end_t = [jnp.concatenate([_stack_heads(tile(b_end, it)).T,
                                  _stack_heads(tile(k_end, it)).T], axis=1).astype(BF16)
                 for it in items]
        end_c = [_dot(e, x) for e, x in zip(end_t, rhs)]
        return sr, out_c, end_c

    sr, out_c, end_c = solve(all_items)

    y_rows = []
    for c_i in range(n_chunks):
        psi_phi = []
        for p in range(PAIRS):
            i = c_i * PAIRS + p
            psi = sr[i] + out_c[i][:, 0:n2]
            pe = p_end[c_i * CHUNK:c_i * CHUNK + 1, p * LANES:(p + 1) * LANES]
            phi = jnp.where(eye, pe, 0.0) + end_c[i][:, 0:n2]
            psi_phi.append(jnp.concatenate([psi, phi], axis=0))
        both = pair_dot(psi_phi, [state_ref[p] for p in range(PAIRS)], prod=_dot_wide_rhs)
        y_cols = []
        for p in range(PAIRS):
            i = c_i * PAIRS + p
            y = both[p][0:n2, :] + out_c[i][:, n2:2 * n2]
            state_ref[p] = both[p][n2:2 * n2, :] + end_c[i][:, n2:2 * n2]
            y_cols.append(y[0:CHUNK, :] + y[CHUNK:n2, :])
        y_rows.append(jnp.concatenate(y_cols, axis=1))
    y = jnp.concatenate(y_rows, axis=0)
    mean = _seg_sum(y, ones_bd) * (1.0 / HEAD)
    yc = y - mean
    var = _seg_sum(yc * yc, ones_bd, wide=False) * (1.0 / HEAD)
    yn = yc * lax.rsqrt(var + LN_X_EPS) * lng_ref[...] + lnb_ref[...]
    o_ref[0] = ((yn + bonus) * g).astype(o_ref.dtype)


def _rwkv_call(z_rwkv, mu_shift, w0, a0, lora_cat, g_lora_up, k_k, k_a, r_k, lnx_g, lnx_b,
               ones_bd, tri):
    batch, seq, _ = z_rwkv.shape
    rows = RWKV_ROWS
    const = lambda shape: pl.BlockSpec(shape, lambda b, c: (0,) * len(shape))
    vec = const((1, WIDTH))
    return pl.pallas_call(
        _rwkv_kernel,
        grid=(batch, seq // rows),
        in_specs=[pl.BlockSpec((1, rows, RWKV_COLS), lambda b, c: (b, c, 0)),
                  const((1, RWKV_COLS)), vec, vec,
                  const((DECAY_LORA + AAA_LORA, 2 * WIDTH)), const((GATE_LORA, WIDTH)),
                  vec, vec, vec, vec, vec, const((MXU_DIM, MXU_DIM)), const((rows, rows))],
        out_specs=pl.BlockSpec((1, rows, WIDTH), lambda b, c: (b, c, 0)),
        out_shape=jax.ShapeDtypeStruct((batch, seq, WIDTH), BF16),
        scratch_shapes=[pltpu.VMEM((1, RWKV_COLS), F32),
                        pltpu.VMEM((PAIRS, LANES, LANES), F32)],
        compiler_params=pltpu.CompilerParams(
            dimension_semantics=("arbitrary", "arbitrary"), vmem_limit_bytes=VMEM_LIMIT),
        name="rwkv7",
    )(z_rwkv, mu_shift, w0, a0, lora_cat, g_lora_up, k_k, k_a, r_k, lnx_g, lnx_b, ones_bd, tri)


def _moba_kernel(q_ref, k_ref, v_ref, sel_ref, o_ref, m_ref, l_ref, acc_ref):
    qi = pl.program_id(2)
    tq = q_ref.shape[1]
    n_pairs = q_ref.shape[2] // LANES
    n_heads = 2 * n_pairs
    first = _iota((tq, LANES), 1) < HEAD
    pair_lanes = [slice(p * LANES, (p + 1) * LANES) for p in range(n_pairs)]
    qh = []
    for lanes in pair_lanes:
        q = q_ref[0, :, lanes]
        zero = jnp.zeros_like(q)
        qh += [jnp.where(first, q, zero), jnp.where(first, zero, q)]

    m_ref[...] = jnp.full(m_ref.shape, NEG_INF, F32)
    l_ref[...] = jnp.zeros(l_ref.shape, F32)
    acc_ref[...] = jnp.zeros(acc_ref.shape, F32)
    tiles = [(h, r0) for h in range(n_heads) for r0 in range(0, tq, MOBA_SUB)]
    q_tiles = [qh[h][r0:r0 + MOBA_SUB] for h, r0 in tiles]

    def update(scores, v_blks):
        n_keys = v_blks[0].shape[0]
        v_ext = [jnp.concatenate([v, jnp.ones((n_keys, LANES), BF16)], axis=1) for v in v_blks]
        refs = [(h, slice(r0, r0 + MOBA_SUB)) for h, r0 in tiles]
        m_old = [m_ref[h, rows, :] for h, rows in refs]
        l_old = [l_ref[h, rows, :] for h, rows in refs]
        acc_old = [acc_ref[h, rows, :] for h, rows in refs]
        m_new = [jnp.maximum(m, jnp.broadcast_to(jnp.max(s, axis=-1, keepdims=True),
                                                 (MOBA_SUB, LANES)))
                 for m, s in zip(m_old, scores)]
        alpha = [jnp.exp(m - n) for m, n in zip(m_old, m_new)]
        p = [jnp.exp(s - jnp.concatenate([n] * (n_keys // LANES), axis=1)).astype(BF16)
             for s, n in zip(scores, m_new)]
        pv = [jnp.dot(x, v_ext[h // 2], preferred_element_type=F32)
              for x, (h, _) in zip(p, tiles)]
        for (h, rows), m, a, l, acc, y in zip(refs, m_new, alpha, l_old, acc_old, pv):
            m_ref[h, rows, :] = m
            l_ref[h, rows, :] = a * l + y[:, LANES:2 * LANES]
            acc_ref[h, rows, :] = a * acc + y[:, 0:LANES]

    start = pl.multiple_of(qi * MOBA_BLOCK, MOBA_BLOCK)
    k_own = [k_ref[0, pl.ds(start, MOBA_BLOCK), lanes] for lanes in pair_lanes]
    v_own = [v_ref[0, pl.ds(start, MOBA_BLOCK), lanes] for lanes in pair_lanes]
    row = _iota((MOBA_SUB, MOBA_BLOCK), 0)
    col = _iota((MOBA_SUB, MOBA_BLOCK), 1)
    update([jnp.where(col <= row + r0, _dot_nt(qt, k_own[h // 2]), NEG_INF)
            for qt, (h, r0) in zip(q_tiles, tiles)], v_own)

    sel = sel_ref[0]
    sel_head = _iota(sel.shape, 1) // SEL_GROUP
    head0 = pl.program_id(1) * n_heads
    sel_h = [jnp.where(sel_head == head0 + h, sel, jnp.zeros_like(sel)) for h in range(n_heads)]
    q_ext = [jnp.concatenate([qt, sel_h[h][r0:r0 + MOBA_SUB]], axis=1)
             for qt, (h, r0) in zip(q_tiles, tiles)]
    def past_blocks(first_blk, n_blocks):
        n_keys = n_blocks * MOBA_BLOCK
        start = pl.multiple_of(first_blk * MOBA_BLOCK, MOBA_BLOCK)
        key_blk = first_blk + _iota((n_keys, LANES), 0) // MOBA_BLOCK
        key_lane = _iota((n_keys, LANES), 1)
        block_onehot = ((key_lane < RWKV_HEADS * SEL_GROUP)
                        & ((key_lane & (SEL_GROUP - 1)) == key_blk)).astype(BF16)
        k_ext = [jnp.concatenate([k_ref[0, pl.ds(start, n_keys), lanes], block_onehot], axis=1)
                 for lanes in pair_lanes]
        v_blks = [v_ref[0, pl.ds(start, n_keys), lanes] for lanes in pair_lanes]
        update([_dot_nt(qe, k_ext[h // 2]) for qe, (h, _) in zip(q_ext, tiles)], v_blks)

    def two_blocks(j, carry):
        past_blocks(2 * j, 2)
        return carry

    lax.fori_loop(0, qi // 2, two_blocks, 0)
    pl.when(qi % 2 == 1)(lambda: past_blocks(qi - 1, 1))

    for p, lanes in enumerate(pair_lanes):
        o_ref[0, :, lanes] = jnp.where(first, acc_ref[2 * p] / l_ref[2 * p],
                                       acc_ref[2 * p + 1] / l_ref[2 * p + 1]).astype(o_ref.dtype)


def _moba_call(q, k, v, sel_bias):
    batch, seq, _ = q.shape
    tq = MOBA_BLOCK
    width = MOBA_PAIRS * LANES
    n_heads = 2 * MOBA_PAIRS
    return pl.pallas_call(
        _moba_kernel,
        grid=(batch, PAIRS // MOBA_PAIRS, seq // tq),
        in_specs=[pl.BlockSpec((1, tq, width), lambda b, p, i: (b, i, p)),
                  pl.BlockSpec((1, seq, width), lambda b, p, i: (b, 0, p)),
                  pl.BlockSpec((1, seq, width), lambda b, p, i: (b, 0, p)),
                  pl.BlockSpec((1, tq, LANES), lambda b, p, i: (b, i, 0))],
        out_specs=pl.BlockSpec((1, tq, width), lambda b, p, i: (b, i, p)),
        out_shape=jax.ShapeDtypeStruct((batch, seq, WIDTH), BF16),
        scratch_shapes=[pltpu.VMEM((n_heads, tq, LANES), F32),
                        pltpu.VMEM((n_heads, tq, LANES), F32),
                        pltpu.VMEM((n_heads, tq, LANES), F32)],
        compiler_params=pltpu.CompilerParams(
            dimension_semantics=("arbitrary", "arbitrary", "arbitrary"),
            vmem_limit_bytes=VMEM_LIMIT),
        name="moba",
    )(q, k, v, sel_bias)


def _pack_bf16_pair(lo, hi):
    lo_bits = lax.bitcast_convert_type(lo.astype(BF16).astype(F32), jnp.uint32)
    hi_bits = lax.bitcast_convert_type(hi.astype(BF16).astype(F32), jnp.uint32)
    return (lo_bits >> 16) | (hi_bits & jnp.uint32(0xFFFF0000))


def _unpack_bf16_pair(u):
    lo = lax.bitcast_convert_type(u << 16, F32)
    hi = lax.bitcast_convert_type(u & jnp.uint32(0xFFFF0000), F32)
    return lo, hi


def _merge_kernel(x_ref, ya_ref, yb_ref, gate_ref, wa_ref, wb_ref, wo_ref, g1_ref, ln_ref,
                  scale_ref, shift_ref, wr_ref, br_ref, tri_ref, x1_ref, h2_ref, route_ref,
                  count_ref, carry_ref):
    @pl.when(pl.program_id(0) == 0)
    def _():
        carry_ref[...] = jnp.zeros_like(carry_ref)

    ya = jnp.dot(ya_ref[...], wa_ref[...], preferred_element_type=F32)
    yb = jnp.dot(yb_ref[...], wb_ref[...], preferred_element_type=F32)
    gates = gate_ref[...]
    merged = (gates[:, 0:D_MODEL].astype(F32) * ya + gates[:, D_MODEL:GATE_COLS].astype(F32) * yb)
    x1 = x_ref[...] + g1_ref[0] * jnp.dot(merged.astype(BF16), wo_ref[...],
                                          preferred_element_type=F32)
    x1_ref[...] = x1

    ms = jnp.mean(x1 * x1, axis=-1, keepdims=True)
    h2 = x1 * lax.rsqrt(ms + NORM_EPS) * ln_ref[...]
    h2 = h2 * (1.0 + scale_ref[0]) + shift_ref[0]
    half = D_MODEL // 2
    h2_ref[...] = _pack_bf16_pair(h2[:, 0:half], h2[:, half:D_MODEL])

    logits = _dot_x3(h2, wr_ref[...]) + br_ref[...]
    lane = _iota(logits.shape, 1)
    lane_f = lane.astype(F32)
    far = float(ROUTER_LANES)

    def top(vals):
        m = jnp.max(vals, axis=-1, keepdims=True)
        idx = jnp.min(jnp.where(vals == m, lane_f, far), axis=-1, keepdims=True)
        return m, idx

    grp = jnp.where(lane < N_GROUPS, logits, NEG_INF)
    g_max, g_idx = top(grp)
    p_group = 1.0 / jnp.sum(jnp.exp(grp - g_max), axis=-1, keepdims=True)

    e_lo = EXPERT_LANE0 + EXPERTS_PER_GROUP * g_idx
    in_grp = (lane_f >= e_lo) & (lane_f < e_lo + EXPERTS_PER_GROUP)
    el = jnp.where(in_grp, logits, NEG_INF)
    m1, i1 = top(el)
    m2, i2 = top(jnp.where(lane_f == i1, NEG_INF, el))
    ratio = jnp.exp(m2 - m1)
    w_first = p_group / (1.0 + ratio)
    w_second = w_first * ratio

    first = lane_f == i1
    second = lane_f == i2
    hits = (first | second).astype(BF16)
    before = carry_ref[...] + jnp.dot(tri_ref[...], hits, preferred_element_type=F32)
    rank1 = jnp.sum(jnp.where(first, before, 0.0), axis=-1, keepdims=True)
    rank2 = jnp.sum(jnp.where(second, before, 0.0), axis=-1, keepdims=True)
    carry = carry_ref[...] + jnp.sum(hits.astype(F32), axis=0, keepdims=True)
    carry_ref[...] = carry
    count_ref[...] = carry

    fields = (i1 - EXPERT_LANE0, i2 - EXPERT_LANE0, w_first, w_second, rank1, rank2)
    route = jnp.zeros(logits.shape, F32)
    for n, field in enumerate(fields):
        route = jnp.where(lane == n, field, route)
    route_ref[...] = route


R_EXPERT1, R_EXPERT2, R_WEIGHT1, R_WEIGHT2, R_RANK1, R_RANK2 = range(6)
MERGE_ROWS = 1024


def _merge_call(x2d, ya, yb, gates, w_br_rwkv, w_br_moba, w_out, gate1, ln2_g, scale2, shift2,
                w_router, b_router, tri, seq):
    tokens = x2d.shape[0]
    tm = MERGE_ROWS
    per_seq = seq // tm
    row = lambda width: pl.BlockSpec((tm, width), lambda i: (i, 0))
    per_batch = pl.BlockSpec((1, 1, D_MODEL), lambda i: (i // per_seq, 0, 0))
    const = lambda shape: pl.BlockSpec(shape, lambda i: (0,) * len(shape))
    return pl.pallas_call(
        _merge_kernel,
        grid=(tokens // tm,),
        in_specs=[row(D_MODEL), row(WIDTH), row(WIDTH), row(GATE_COLS),
                  const((WIDTH, D_MODEL)), const((WIDTH, D_MODEL)), const((D_MODEL, D_MODEL)),
                  per_batch, const((1, D_MODEL)), per_batch, per_batch,
                  const((D_MODEL, ROUTER_LANES)), const((1, ROUTER_LANES)), const((tm, tm))],
        out_specs=[row(D_MODEL), row(D_MODEL // 2), row(ROUTER_LANES),
                   const((1, ROUTER_LANES))],
        out_shape=[jax.ShapeDtypeStruct((tokens, D_MODEL), F32),
                   jax.ShapeDtypeStruct((tokens, D_MODEL // 2), jnp.uint32),
                   jax.ShapeDtypeStruct((tokens, ROUTER_LANES), F32),
                   jax.ShapeDtypeStruct((1, ROUTER_LANES), F32)],
        scratch_shapes=[pltpu.VMEM((1, ROUTER_LANES), F32)],
        compiler_params=pltpu.CompilerParams(
            dimension_semantics=("arbitrary",), vmem_limit_bytes=VMEM_LIMIT),
        name="merge",
    )(x2d, ya, yb, gates, w_br_rwkv, w_br_moba, w_out, gate1, ln2_g, scale2, shift2,
      w_router, b_router, tri)


EXPERT_ROWS = 512
DISPATCH_TOKENS = 1024
COMBINE_TOKENS = 512


def _dest_kernel(route_ref, start_ref, o_ref):
    route = route_ref[...]
    lane = _iota(route.shape, 1)
    lane_f = lane.astype(F32)
    starts = start_ref[...]
    out = jnp.zeros(route.shape, F32)
    for n, (e_lane, r_lane) in enumerate(((R_EXPERT1, R_RANK1), (R_EXPERT2, R_RANK2))):
        e_col = route[:, e_lane:e_lane + 1] + EXPERT_LANE0
        first_row = jnp.sum(jnp.where(lane_f == e_col, starts, 0.0), axis=-1, keepdims=True)
        out = jnp.where(lane == n, first_row + route[:, r_lane:r_lane + 1], out)
    o_ref[...] = out


def _dest_call(route, start_row):
    tokens = route.shape[0]
    tm = DISPATCH_TOKENS
    return pl.pallas_call(
        _dest_kernel,
        grid=(tokens // tm,),
        in_specs=[pl.BlockSpec((tm, ROUTER_LANES), lambda i: (i, 0)),
                  pl.BlockSpec((1, ROUTER_LANES), lambda i: (0, 0))],
        out_specs=pl.BlockSpec((tm, ROUTER_LANES), lambda i: (i, 0)),
        out_shape=jax.ShapeDtypeStruct((tokens, ROUTER_LANES), F32),
        name="moe_dest",
    )(route, start_row)


def _dispatch_kernel(d1_ref, d2_ref, ends_ref, h_ref, xs_hbm, zero_ref, sem, zero_sem):
    base = pl.program_id(0) * DISPATCH_TOKENS

    @pl.when(pl.program_id(0) == 0)
    def _():
        zero_ref[...] = jnp.zeros_like(zero_ref)

        def fill(e):
            end = ends_ref[e]
            start = pl.multiple_of(end - EXPERT_ROWS, EXPERT_ROWS)
            return pltpu.make_async_copy(zero_ref, xs_hbm.at[pl.ds(start, EXPERT_ROWS)],
                                         zero_sem)

        def non_empty(e):
            return ends_ref[e] > (ends_ref[e - 1] if e else 0)

        for e in range(N_EXPERTS):
            pl.when(non_empty(e))(lambda e=e: fill(e).start())
        for e in range(N_EXPERTS):
            pl.when(non_empty(e))(lambda e=e: fill(e).wait())

        def tail(t):
            start = pl.multiple_of(t * EXPERT_ROWS, EXPERT_ROWS)
            return pltpu.make_async_copy(zero_ref, xs_hbm.at[pl.ds(start, EXPERT_ROWS)],
                                         zero_sem)

        first_tail = ends_ref[N_EXPERTS - 1] // EXPERT_ROWS
        n_tiles = xs_hbm.shape[0] // EXPERT_ROWS
        lax.fori_loop(first_tail, n_tiles, lambda t, c: (tail(t).start(), c)[1], 0)
        lax.fori_loop(first_tail, n_tiles, lambda t, c: (tail(t).wait(), c)[1], 0)

    def issue(t, carry):
        src = h_ref.at[pl.ds(t, 1)]
        pltpu.make_async_copy(src, xs_hbm.at[pl.ds(d1_ref[base + t], 1)], sem).start(priority=0)
        pltpu.make_async_copy(src, xs_hbm.at[pl.ds(d2_ref[base + t], 1)], sem).start(priority=1)
        return carry

    lax.fori_loop(0, DISPATCH_TOKENS, issue, 0, unroll=8)
    for _ in range(2):
        pltpu.make_async_copy(h_ref, xs_hbm.at[pl.ds(0, DISPATCH_TOKENS)], sem).wait()


def _dispatch_call(dest1, dest2, ends, h2p, n_rows):
    tokens = h2p.shape[0]
    half = D_MODEL // 2
    return pl.pallas_call(
        _dispatch_kernel,
        grid_spec=pltpu.PrefetchScalarGridSpec(
            num_scalar_prefetch=3,
            grid=(tokens // DISPATCH_TOKENS,),
            in_specs=[pl.BlockSpec((DISPATCH_TOKENS, half), lambda i, d1, d2, ends: (i, 0))],
            out_specs=pl.BlockSpec(memory_space=pl.ANY),
            scratch_shapes=[pltpu.VMEM((EXPERT_ROWS, half), jnp.uint32),
                            pltpu.SemaphoreType.DMA(()), pltpu.SemaphoreType.DMA(())]),
        out_shape=jax.ShapeDtypeStruct((n_rows, half), jnp.uint32),
        compiler_params=pltpu.CompilerParams(dimension_semantics=("arbitrary",)),
        name="moe_dispatch",
    )(dest1, dest2, ends, h2p)


def _expert_kernel(te_ref, nu_ref, xs_ref, w1_ref, w3_ref, w2_ref, ys_ref, w1b_ref, w3b_ref,
                   w2b_ref):
    half = D_MODEL // 2
    j = pl.program_id(0)

    @pl.when((j == 0) | (te_ref[j] != te_ref[jnp.maximum(j - 1, 0)]))
    def _():
        w1b_ref[...] = w1_ref[0].astype(BF16)
        w3b_ref[...] = w3_ref[0].astype(BF16)
        w2b_ref[...] = w2_ref[0].astype(BF16)

    @pl.when(j < nu_ref[0])
    def _():
        x_lo, x_hi = _unpack_bf16_pair(xs_ref[...])
        x_lo = x_lo.astype(BF16)
        x_hi = x_hi.astype(BF16)

        def proj(w_ref):
            return (jnp.dot(x_lo, w_ref[0:half, :], preferred_element_type=F32)
                    + jnp.dot(x_hi, w_ref[half:D_MODEL, :], preferred_element_type=F32))

        a = proj(w1b_ref)
        hid = (a * jax.nn.sigmoid(a) * proj(w3b_ref)).astype(BF16)
        y = jnp.dot(hid, w2b_ref[...], preferred_element_type=F32)
        ys_ref[...] = _pack_bf16_pair(y[:, 0:half], y[:, half:D_MODEL])

    @pl.when(j >= nu_ref[0])
    def _():
        ys_ref[...] = jnp.zeros_like(ys_ref)


def _expert_call(tile_expert, n_used, xs, w1, w3, w2):
    n_rows = xs.shape[0]
    half = D_MODEL // 2
    w_spec = lambda shape: pl.BlockSpec((1,) + shape, lambda j, te, nu: (te[j], 0, 0))
    return pl.pallas_call(
        _expert_kernel,
        grid_spec=pltpu.PrefetchScalarGridSpec(
            num_scalar_prefetch=2,
            grid=(n_rows // EXPERT_ROWS,),
            in_specs=[pl.BlockSpec((EXPERT_ROWS, half),
                                   lambda j, te, nu: (jnp.minimum(j, nu[0] - 1), 0)),
                      w_spec((D_MODEL, D_EXPERT)), w_spec((D_MODEL, D_EXPERT)),
                      w_spec((D_EXPERT, D_MODEL))],
            out_specs=pl.BlockSpec((EXPERT_ROWS, half), lambda j, te, nu: (j, 0)),
            scratch_shapes=[pltpu.VMEM((D_MODEL, D_EXPERT), BF16),
                            pltpu.VMEM((D_MODEL, D_EXPERT), BF16),
                            pltpu.VMEM((D_EXPERT, D_MODEL), BF16)]),
        out_shape=jax.ShapeDtypeStruct((n_rows, half), jnp.uint32),
        compiler_params=pltpu.CompilerParams(
            dimension_semantics=("arbitrary",), vmem_limit_bytes=VMEM_LIMIT),
        name="moe_experts",
    )(tile_expert, n_used, xs, w1, w3, w2)


def _combine_kernel(d1_ref, d2_ref, ys_hbm, x1_ref, route_ref, g2_ref, o_ref, buf_ref, sem):
    i = pl.program_id(0)
    n_steps = pl.num_programs(0)
    tc = COMBINE_TOKENS
    half = D_MODEL // 2

    groups = tc // SUBLANES

    def issue(step, slot):
        base = step * tc

        def one_group(grp, carry):
            for r in range(SUBLANES):
                tok = base + grp * SUBLANES + r
                d1 = d1_ref[tok]
                d2 = d2_ref[tok]
                pltpu.make_async_copy(ys_hbm.at[d1 >> 3, pl.ds(d1 & (SUBLANES - 1), 1)],
                                      buf_ref.at[slot, grp, pl.ds(r, 1)],
                                      sem.at[slot]).start(priority=0)
                pltpu.make_async_copy(ys_hbm.at[d2 >> 3, pl.ds(d2 & (SUBLANES - 1), 1)],
                                      buf_ref.at[slot, groups + grp, pl.ds(r, 1)],
                                      sem.at[slot]).start(priority=1)
            return carry

        lax.fori_loop(0, groups, one_group, 0)

    slot = i % 2

    @pl.when(i == 0)
    def _():
        issue(0, 0)

    @pl.when(i + 1 < n_steps)
    def _():
        issue(i + 1, 1 - slot)

    pltpu.make_async_copy(ys_hbm.at[pl.ds(0, 2 * groups)], buf_ref.at[slot], sem.at[slot]).wait()

    rows = buf_ref[slot].reshape(2 * tc, half)
    a_lo, a_hi = _unpack_bf16_pair(rows[0:tc])
    b_lo, b_hi = _unpack_bf16_pair(rows[tc:2 * tc])
    route = route_ref[...]
    w_a = route[:, R_WEIGHT1:R_WEIGHT1 + 1]
    w_b = route[:, R_WEIGHT2:R_WEIGHT2 + 1]
    g2 = g2_ref[0]
    o_ref[:, 0:half] = x1_ref[:, 0:half] + g2[:, 0:half] * (w_a * a_lo + w_b * b_lo)
    o_ref[:, half:D_MODEL] = (x1_ref[:, half:D_MODEL]
                              + g2[:, half:D_MODEL] * (w_a * a_hi + w_b * b_hi))


def _combine_call(dest1, dest2, ys, x1, route, gate2, seq):
    tokens = x1.shape[0]
    tc = COMBINE_TOKENS
    per_seq = seq // tc
    half = D_MODEL // 2
    return pl.pallas_call(
        _combine_kernel,
        grid_spec=pltpu.PrefetchScalarGridSpec(
            num_scalar_prefetch=2,
            grid=(tokens // tc,),
            in_specs=[pl.BlockSpec(memory_space=pl.ANY),
                      pl.BlockSpec((tc, D_MODEL), lambda i, d1, d2: (i, 0)),
                      pl.BlockSpec((tc, ROUTER_LANES), lambda i, d1, d2: (i, 0)),
                      pl.BlockSpec((1, 1, D_MODEL), lambda i, d1, d2: (i // per_seq, 0, 0))],
            out_specs=pl.BlockSpec((tc, D_MODEL), lambda i, d1, d2: (i, 0)),
            scratch_shapes=[pltpu.VMEM((2, 2 * tc // SUBLANES, SUBLANES, half), jnp.uint32),
                            pltpu.SemaphoreType.DMA((2,))]),
        out_shape=jax.ShapeDtypeStruct((tokens, D_MODEL), F32),
        compiler_params=pltpu.CompilerParams(
            dimension_semantics=("arbitrary",), vmem_limit_bytes=VMEM_LIMIT),
        name="moe_combine",
    )(dest1, dest2, ys.reshape(-1, SUBLANES, half), x1, route, gate2)


def _moe(h2p, route, counts, x1, gate2, w1, w3, w2, seq):
    tokens = h2p.shape[0]
    n_rows = 2 * tokens + N_EXPERTS * EXPERT_ROWS
    n_rows -= n_rows % EXPERT_ROWS
    n_tiles = n_rows // EXPERT_ROWS

    count = counts[0, EXPERT_LANE0:EXPERT_LANE0 + N_EXPERTS].astype(jnp.int32)
    padded = (count + EXPERT_ROWS - 1) // EXPERT_ROWS * EXPERT_ROWS
    ends = jnp.cumsum(padded)
    starts = ends - padded
    tile_start = jnp.arange(n_tiles, dtype=jnp.int32) * EXPERT_ROWS
    tile_expert = jnp.minimum(jnp.sum(ends[None, :] <= tile_start[:, None], axis=1),
                              N_EXPERTS - 1).astype(jnp.int32)
    n_used = (ends[-1:] // EXPERT_ROWS).astype(jnp.int32)

    start_row = jnp.zeros((1, ROUTER_LANES), F32)
    start_row = start_row.at[0, EXPERT_LANE0:EXPERT_LANE0 + N_EXPERTS].set(starts.astype(F32))
    dest = _dest_call(route, start_row)[:, 0:2].astype(jnp.int32)
    dest1, dest2 = dest[:, 0], dest[:, 1]

    xs = _dispatch_call(dest1, dest2, ends.astype(jnp.int32), h2p, n_rows)
    ys = _expert_call(tile_expert, n_used, xs, w1, w3, w2)
    return _combine_call(dest1, dest2, ys, x1, route, gate2, seq)


def _rope_tables(seq):
    half = HEAD // 2
    inv_freq = ROPE_THETA ** (-jnp.arange(half, dtype=F32) / half)
    ang = jnp.arange(seq, dtype=F32)[:, None] * inv_freq[None, :]
    cos = jnp.cos(ang)
    sin = jnp.sin(ang)
    cos_head = jnp.concatenate([cos, cos], axis=1)
    sin_head = jnp.concatenate([-sin, sin], axis=1)
    return jnp.tile(cos_head, (1, RWKV_HEADS)), jnp.tile(sin_head, (1, RWKV_HEADS))


def _layer(x, c, w_ada, b_ada, ln1_g, ln2_g, w_in, mu_shift, w0, w_lora_up, a0, a_lora_up,
           g_lora_up, k_k, k_a, r_k, lnx_g, lnx_b, q_norm_g, k_norm_g, w_br_rwkv, w_br_moba,
           w_out, w_rg, b_rg, w_re, b_re, w1, w3, w2):
    batch, seq, _ = x.shape
    vec = lambda a: a.reshape(1, -1)

    mod = _mod_call(c, w_ada, b_ada)
    shift1, scale1, gate1, shift2, scale2, gate2 = (
        m.reshape(batch, 1, D_MODEL) for m in jnp.split(mod, 6, axis=-1))

    idx = np.arange(MXU_DIM)
    ones_bd = jnp.asarray(idx[:, None] // HEAD == idx[None, :] // HEAD, BF16)
    t_idx = np.arange(RWKV_ROWS)
    tri = jnp.asarray((t_idx[:, None] >= t_idx[None, :])
                      & (t_idx[:, None] // CHUNK == t_idx[None, :] // CHUNK), BF16)
    cos, sin_signed = _rope_tables(seq)

    z_rwkv, q, k, v, gates, sel_bias = _inproj_call(
        x, shift1, scale1, vec(ln1_g), w_in.astype(BF16),
        vec(jnp.tile(q_norm_g, RWKV_HEADS)), vec(jnp.tile(k_norm_g, RWKV_HEADS)),
        cos, sin_signed, ones_bd)

    lora_cat = jnp.zeros((DECAY_LORA + AAA_LORA, 2 * WIDTH), F32)
    lora_cat = lora_cat.at[:DECAY_LORA, :WIDTH].set(w_lora_up).at[DECAY_LORA:, WIDTH:].set(a_lora_up)
    y_a = _rwkv_call(z_rwkv, vec(mu_shift), vec(w0), vec(a0), lora_cat, g_lora_up, vec(k_k),
                     vec(k_a), vec(r_k), vec(lnx_g), vec(lnx_b), ones_bd, tri)

    y_b = _moba_call(q, k, v, sel_bias)

    tokens = batch * seq
    w_router = jnp.zeros((D_MODEL, ROUTER_LANES), F32)
    w_router = w_router.at[:, :N_GROUPS].set(w_rg).at[:, EXPERT_LANE0:EXPERT_LANE0 + N_EXPERTS].set(w_re)
    b_router = jnp.zeros((1, ROUTER_LANES), F32)
    b_router = b_router.at[0, :N_GROUPS].set(b_rg).at[0, EXPERT_LANE0:EXPERT_LANE0 + N_EXPERTS].set(b_re)
    m_idx = np.arange(MERGE_ROWS)
    tri_strict = jnp.asarray(m_idx[:, None] > m_idx[None, :], BF16)
    x1, h2p, route, counts = _merge_call(
        x.reshape(tokens, D_MODEL), y_a.reshape(tokens, WIDTH), y_b.reshape(tokens, WIDTH),
        gates.reshape(tokens, GATE_COLS), w_br_rwkv.astype(BF16), w_br_moba.astype(BF16),
        w_out.astype(BF16), gate1, vec(ln2_g), scale2, shift2, w_router, b_router, tri_strict,
        seq)

    out = _moe(h2p, route, counts, x1, gate2, w1, w3, w2, seq)
    return out.reshape(batch, seq, D_MODEL)


def kernel(x, c, w_ada, b_ada, ln1_g, ln2_g, w_in, mu_shift, w0, w_lora_up, a0, a_lora_up,
           g_lora_up, k_k, k_a, r_k, lnx_g, lnx_b, q_norm_g, k_norm_g, w_br_rwkv, w_br_moba,
           w_out, w_rg, b_rg, w_re, b_re, w1, w3, w2):
    assert w_ada.shape[0] == 1, "single-layer problem"
    layer_params = (w_ada, b_ada, ln1_g, ln2_g, w_in, mu_shift, w0, w_lora_up, a0, a_lora_up,
                    g_lora_up, k_k, k_a, r_k, lnx_g, lnx_b, q_norm_g, k_norm_g, w_br_rwkv,
                    w_br_moba, w_out, w_rg, b_rg, w_re, b_re, w1, w3, w2)
    return _layer(x, c, *(p[0] for p in layer_params))
```
